```python
import jax, jax.numpy as jnp
from jax import lax
import numpy as np

D_MODEL = 1024
BATCH = 8
SEQ = 2048
DEPTH = 1
DEC_BATCH = 128
DEC_SEQ = 4
PAST_LEN = 16384
PAGE_SIZE = 128

CHUNK = 128
A_GROUPS = 8
A_WIDTH = D_MODEL
A_GROUP_DIM = A_WIDTH // A_GROUPS
CONV_CH = D_MODEL
CONV_WIDTH = 31
N_EXPERTS = 32
TOP_K = 4
D_EXPERT = D_MODEL
SWIGLU_LIMIT = 7.0
SWIGLU_ALPHA = 1.702
EXPERT_BLOCK = 128
EPS = 1e-5
PROJ_COLS = 2 * A_WIDTH + 2 * CONV_CH + 2 * D_MODEL

kernel_name = "gated_chunk_gmlp_conformer_moe_step"


def rmsnorm(x, g):
    xf = x.astype(jnp.float32)
    y = xf * lax.rsqrt(jnp.mean(xf * xf, axis=-1, keepdims=True) + EPS)
    return (y * g.astype(jnp.float32)).astype(x.dtype)


def layernorm(x, g, b):
    xf = x.astype(jnp.float32)
    mu = jnp.mean(xf, axis=-1, keepdims=True)
    var = jnp.mean(jnp.square(xf - mu), axis=-1, keepdims=True)
    y = (xf - mu) * lax.rsqrt(var + EPS)
    return (y * g.astype(jnp.float32) + b.astype(jnp.float32)).astype(x.dtype)


def chunk_spatial_mix(v, w_s, b_s):
    bsz, t, _ = v.shape
    t_pad = -(-t // CHUNK) * CHUNK
    vp = jnp.pad(v, ((0, 0), (0, t_pad - t), (0, 0)))
    vp = vp.reshape(bsz, t_pad // CHUNK, CHUNK, A_GROUPS, A_GROUP_DIM)
    mask = jnp.tril(jnp.ones((CHUNK, CHUNK), dtype=bool))
    w = jnp.where(mask[None], w_s, jnp.zeros((), w_s.dtype)).astype(v.dtype)
    s = jnp.einsum('gij,bcjgd->bcigd', w, vp) + b_s.T.astype(v.dtype)[None, None, :, :, None]
    return s.reshape(bsz, t_pad, A_WIDTH)[:, :t]


def token_mixers(h, conv_buf, w_in, ln_v_g, ln_v_b, w_spatial, b_spatial, w_branch_a,
                 conv_w, conv_b, ln_conv_g, ln_conv_b, w_branch_b, b_branch_b,
                 b_gate_a, b_gate_b, w_out):
    p = jnp.einsum('btd,dk->btk', h, w_in)
    c0 = A_WIDTH
    c1 = 2 * A_WIDTH
    c2 = c1 + CONV_CH
    c3 = c2 + CONV_CH
    c4 = c3 + D_MODEL
    u_a, v_a, c_val, c_gate, g_a, g_b = jnp.split(p, [c0, c1, c2, c3, c4], axis=-1)
    u_a = jax.nn.gelu(u_a, approximate=False)
    v_a = layernorm(jax.nn.gelu(v_a, approximate=False), ln_v_g, ln_v_b)
    y_a = jnp.einsum('btk,kd->btd', u_a * chunk_spatial_mix(v_a, w_spatial, b_spatial), w_branch_a)
    glu = c_val * jax.nn.sigmoid(c_gate)
    padded = jnp.concatenate([conv_buf.astype(glu.dtype), glu], axis=1)
    conv = lax.conv_general_dilated(
        padded, conv_w[:, None, :].astype(padded.dtype), window_strides=(1,), padding='VALID',
        dimension_numbers=('NWC', 'WIO', 'NWC'), feature_group_count=CONV_CH) + conv_b
    y_b = jnp.einsum('btc,cd->btd', jax.nn.silu(layernorm(conv, ln_conv_g, ln_conv_b)), w_branch_b) + b_branch_b
    merged = jax.nn.sigmoid(g_a + b_gate_a) * y_a + jax.nn.sigmoid(g_b + b_gate_b) * y_b
    new_buf = padded[:, -(CONV_WIDTH - 1):]
    return jnp.einsum('btd,de->bte', merged, w_out), new_buf, v_a


def moe(h, w_router, b_router, w_up, b_up, w_down, b_down):
    bsz, t, d = h.shape
    n_tok = bsz * t
    x2 = h.reshape(n_tok, d)
    logits = jnp.einsum('td,de->te', x2.astype(jnp.float32), w_router.astype(jnp.float32)) + b_router.astype(jnp.float32)
    top_vals, top_idx = lax.top_k(logits, TOP_K)
    gates = jax.nn.softmax(top_vals, axis=-1)
    n_assign = n_tok * TOP_K
    flat_e = top_idx.reshape(-1).astype(jnp.int32)
    flat_tok = jnp.repeat(jnp.arange(n_tok, dtype=jnp.int32), TOP_K)
    flat_g = gates.reshape(-1)
    order = jnp.argsort(flat_e)
    sorted_e = flat_e[order]
    counts = jnp.bincount(flat_e, length=N_EXPERTS).astype(jnp.int32)
    start = jnp.cumsum(counts) - counts
    padded_counts = ((counts + EXPERT_BLOCK - 1) // EXPERT_BLOCK) * EXPERT_BLOCK
    padded_end = jnp.cumsum(padded_counts)
    padded_start = padded_end - padded_counts
    rank = jnp.arange(n_assign, dtype=jnp.int32) - start[sorted_e]
    dest = padded_start[sorted_e] + rank
    n_blk = -(-(n_assign + N_EXPERTS * (EXPERT_BLOCK - 1)) // EXPERT_BLOCK)
    n_rows = n_blk * EXPERT_BLOCK
    row_tok = jnp.zeros((n_rows,), jnp.int32).at[dest].set(flat_tok[order])
    row_gate = jnp.zeros((n_rows,), jnp.float32).at[dest].set(flat_g[order])
    block_expert = jnp.clip(
        jnp.searchsorted(padded_end, jnp.arange(n_blk, dtype=jnp.int32) * EXPERT_BLOCK, side='right'),
        0, N_EXPERTS - 1)
    xs = x2[row_tok].reshape(n_blk, EXPERT_BLOCK, d)

    def expert_block(args):
        xb, e = args
        gu = xb @ w_up[e] + b_up[e]
        gate, up = gu[:, :D_EXPERT], gu[:, D_EXPERT:]
        gate = jnp.minimum(gate, SWIGLU_LIMIT)
        up = jnp.clip(up, -SWIGLU_LIMIT, SWIGLU_LIMIT)
        act = (up + 1.0) * (gate * jax.nn.sigmoid(SWIGLU_ALPHA * gate))
        return act @ w_down[e] + b_down[e]

    out = lax.map(expert_block, (xs, block_expert)).reshape(n_rows, d)
    y = jax.ops.segment_sum(out * row_gate[:, None].astype(out.dtype), row_tok, num_segments=n_tok)
    return y.reshape(bsz, t, d).astype(h.dtype)


def layer(x, conv_buf, norm_mix_g, w_in, ln_v_g, ln_v_b, w_spatial, b_spatial, w_branch_a,
          conv_w, conv_b, ln_conv_g, ln_conv_b, w_branch_b, b_branch_b, b_gate_a, b_gate_b,
          w_out, norm_ffn_g, w_router, b_router, w_up, b_up, w_down, b_down):
    mix, new_buf, v_rows = token_mixers(
        rmsnorm(x, norm_mix_g), conv_buf, w_in, ln_v_g, ln_v_b, w_spatial, b_spatial, w_branch_a,
        conv_w, conv_b, ln_conv_g, ln_conv_b, w_branch_b, b_branch_b, b_gate_a, b_gate_b, w_out)
    x = x + mix
    x = x + moe(rmsnorm(x, norm_ffn_g), w_router, b_router, w_up, b_up, w_down, b_down)
    return x, new_buf, v_rows


def setup_inputs(seed: int = 0) -> dict:
    key = jax.random.key(seed)
    ks = iter(jax.random.split(key, 40))

    def nrm(shape, scale):
        return jax.random.normal(next(ks), shape, jnp.float32) * scale

    L, D, E, F = DEPTH, D_MODEL, N_EXPERTS, D_EXPERT
    return {
        "x_prompt": nrm((BATCH, SEQ, D), 1.0),
        "x_sample": nrm((DEC_BATCH, DEC_SEQ, D), 1.0),
        "cache_conv": nrm((L, DEC_BATCH, CONV_WIDTH - 1, CONV_CH), 0.5),
        "norm_mix_g": 1.0 + nrm((L, D), 0.02),
        "w_in": nrm((L, D, PROJ_COLS), D ** -0.5),
        "ln_v_g": 1.0 + nrm((L, A_WIDTH), 0.02),
        "ln_v_b": nrm((L, A_WIDTH), 0.02),
        "w_spatial": nrm((L, A_GROUPS, CHUNK, CHUNK), CHUNK ** -0.5),
        "b_spatial": 1.0 + nrm((L, A_GROUPS, CHUNK), 0.02),
        "w_branch_a": nrm((L, A_WIDTH, D), A_WIDTH ** -0.5),
        "conv_w": nrm((L, CONV_WIDTH, CONV_CH), CONV_WIDTH ** -0.5),
        "conv_b": nrm((L, CONV_CH), 0.02),
        "ln_conv_g": 1.0 + nrm((L, CONV_CH), 0.02),
        "ln_conv_b": nrm((L, CONV_CH), 0.02),
        "w_branch_b": nrm((L, CONV_CH, D), CONV_CH ** -0.5),
        "b_branch_b": nrm((L, D), 0.02),
        "b_gate_a": nrm((L, D), 0.02),
        "b_gate_b": nrm((L, D), 0.02),
        "w_out": nrm((L, D, D), D ** -0.5),
        "norm_ffn_g": 1.0 + nrm((L, D), 0.02),
        "w_router": nrm((L, D, E), D ** -0.5),
        "b_router": nrm((L, E), 0.01),
        "w_up": nrm((L, E, D, 2 * F), D ** -0.5),
        "b_up": nrm((L, E, 2 * F), 0.02),
        "w_down": nrm((L, E, F, D), F ** -0.5),
        "b_down": nrm((L, E, D), 0.02),
        "norm_final_g": 1.0 + nrm((D,), 0.02),
    }


def reference(x_prompt, x_sample, cache_conv, norm_mix_g, w_in, ln_v_g, ln_v_b, w_spatial,
              b_spatial, w_branch_a, conv_w, conv_b, ln_conv_g, ln_conv_b, w_branch_b,
              b_branch_b, b_gate_a, b_gate_b, w_out, norm_ffn_g, w_router, b_router, w_up,
              b_up, w_down, b_down, norm_final_g):
    xp, xs = x_prompt, x_sample
    conv_p, conv_s, v_s = [], [], []
    for l in range(DEPTH):
        lp = (norm_mix_g[l], w_in[l], ln_v_g[l], ln_v_b[l], w_spatial[l], b_spatial[l],
              w_branch_a[l], conv_w[l], conv_b[l], ln_conv_g[l], ln_conv_b[l], w_branch_b[l],
              b_branch_b[l], b_gate_a[l], b_gate_b[l], w_out[l], norm_ffn_g[l], w_router[l],
              b_router[l], w_up[l], b_up[l], w_down[l], b_down[l])
        buf0 = jnp.zeros((xp.shape[0], CONV_WIDTH - 1, CONV_CH), xp.dtype)
        xp, bp, _ = layer(xp, buf0, *lp)
        xs, bs, vs = layer(xs, cache_conv[l], *lp)
        conv_p.append(bp)
        conv_s.append(bs)
        v_s.append(vs)
    y_prompt = rmsnorm(xp, norm_final_g)
    y_sample = rmsnorm(xs, norm_final_g)
    conv_state_prompt = jnp.stack(conv_p)
    conv_state_sample = jnp.stack(conv_s)
    chunk_v_sample = jnp.stack(v_s)
    return (y_prompt, y_sample, conv_state_prompt, conv_state_sample, chunk_v_sample)
```

```python
import functools

import jax
import jax.numpy as jnp
from jax import lax
from jax.experimental import pallas as pl
from jax.experimental.pallas import tpu as pltpu

D_MODEL = 1024
CHUNK = 128
A_GROUPS = 8
CONV_WIDTH = 31
HIST = 32
N_EXPERTS = 32
TOP_K = 4
D_EXPERT = 1024
SWIGLU_LIMIT = 7.0
SWIGLU_ALPHA = 1.702
EPS = 1e-5
LANES = 128
SUBLANES = 8
ROW_TILES = D_MODEL // LANES
TT = 256
SB = 32
BLK = 256
VMEM_LIMIT = 56 * 1024 * 1024

_F32 = jnp.float32
_BF16 = jnp.bfloat16
_INV_SQRT2 = 0.7071067811865476


def _rms(x, g):
    return x * lax.rsqrt(jnp.mean(x * x, axis=-1, keepdims=True) + EPS) * g


def _ln(x, g, b):
    mu = jnp.mean(x, axis=-1, keepdims=True)
    xc = x - mu
    var = jnp.mean(xc * xc, axis=-1, keepdims=True)
    return xc * lax.rsqrt(var + EPS) * g + b


def _gelu(x):
    return 0.5 * x * (1.0 + lax.erf(x * _INV_SQRT2))


def _sigmoid(x):
    return 1.0 / (1.0 + jnp.exp(-x))


def _dot(a, b):
    return jnp.dot(a, b, preferred_element_type=_F32)


def _store_row_tiles(ref, lead, n, val):
    for j in range(ROW_TILES):
        ref[lead + (pl.ds(j, n, stride=ROW_TILES), slice(None))] = val[:, j * LANES:(j + 1) * LANES]


def _load_row_tiles(ref, start, n):
    return jnp.concatenate(
        [ref[pl.ds(start + j, n, stride=ROW_TILES), :] for j in range(ROW_TILES)], axis=1)


def _route(h2, wr_ref, br_ref, carry_ref):
    n = h2.shape[0]
    logits = jnp.dot(h2, wr_ref[...], preferred_element_type=_F32,
                     precision=lax.Precision.HIGHEST) + br_ref[...]
    lane = lax.broadcasted_iota(jnp.int32, (n, LANES), 1)
    l = jnp.where(lane < N_EXPERTS, logits, -jnp.inf)
    onehots, vals, idxs = [], [], []
    for _ in range(TOP_K):
        m = jnp.max(l, axis=-1, keepdims=True)
        idx = jnp.min(jnp.where(l == m, lane, LANES), axis=-1, keepdims=True)
        oh = lane == idx
        onehots.append(oh)
        vals.append(m)
        idxs.append(idx)
        l = jnp.where(oh, -jnp.inf, l)
    exps = [jnp.exp(v - vals[0]) for v in vals]
    denom = exps[0] + exps[1] + exps[2] + exps[3]
    gates = [e / denom for e in exps]

    oh_all = jnp.zeros((n, LANES), _F32)
    for oh in onehots:
        oh_all = oh_all + jnp.where(oh, 1.0, 0.0)
    ri = lax.broadcasted_iota(jnp.int32, (n, n), 0)
    ci = lax.broadcasted_iota(jnp.int32, (n, n), 1)
    tri = jnp.where(ci < ri, 1.0, 0.0).astype(_BF16)
    base = _dot(tri, oh_all.astype(_BF16)) + carry_ref[0:1, :]
    ranks = [jnp.sum(jnp.where(oh, base, 0.0), axis=-1, keepdims=True) for oh in onehots]
    carry_ref[...] = carry_ref[...] + jnp.sum(oh_all, axis=0, keepdims=True)

    meta_i = jnp.zeros((n, LANES), jnp.int32)
    meta_f = jnp.zeros((n, LANES), _F32)
    for k in range(TOP_K):
        meta_i = jnp.where(lane == k, idxs[k], meta_i)
        meta_i = jnp.where(lane == TOP_K + k, ranks[k].astype(jnp.int32), meta_i)
        meta_f = jnp.where(lane == k, gates[k], meta_f)
    return meta_i, meta_f


def _mixer_prompt_kernel(x_ref, gmix_ref, win_ref, lnvg_ref, lnvb_ref, wsp_ref, bsp_ref, wa_ref,
                         cw_ref, cb_ref, lncg_ref, lncb_ref, wb_ref, bb_ref, bga_ref, bgb_ref,
                         wout_ref, gffn_ref, wr_ref, br_ref,
                         x1_ref, h2_ref, mi_ref, mf_ref, cnt_ref, cst_ref,
                         cbuf, carry):
    b = pl.program_id(0)
    t = pl.program_id(1)

    @pl.when((b == 0) & (t == 0))
    def _():
        carry[...] = jnp.zeros_like(carry)

    @pl.when(t == 0)
    def _():
        cbuf[0:HIST, :] = jnp.zeros((HIST, D_MODEL), _F32)

    x = x_ref[...]
    hb = _rms(x, gmix_ref[...]).astype(_BF16)

    def proj(s):
        return _dot(hb, win_ref[:, s * D_MODEL:(s + 1) * D_MODEL])

    u = _gelu(proj(0))
    v = _ln(_gelu(proj(1)), lnvg_ref[...], lnvb_ref[...])
    ri = lax.broadcasted_iota(jnp.int32, (CHUNK, CHUNK), 0)
    ci = lax.broadcasted_iota(jnp.int32, (CHUNK, CHUNK), 1)
    s_rows = []
    for c in range(TT // CHUNK):
        s_cols = []
        for g in range(A_GROUPS):
            wm = jnp.where(ci <= ri, wsp_ref[g], 0.0).astype(_BF16)
            vb = v[c * CHUNK:(c + 1) * CHUNK, g * LANES:(g + 1) * LANES].astype(_BF16)
            s_cols.append(_dot(wm, vb))
        s_rows.append(jnp.concatenate(s_cols, axis=1) + bsp_ref[...])
    s = jnp.concatenate(s_rows, axis=0)
    y_a = _dot((u * s).astype(_BF16), wa_ref[...])

    glu = proj(2) * _sigmoid(proj(3))
    cbuf[HIST:HIST + TT, :] = glu
    conv = jnp.zeros((TT, D_MODEL), _F32) + cb_ref[...]
    off = HIST - (CONV_WIDTH - 1)
    for k in range(CONV_WIDTH):
        conv = conv + cw_ref[k:k + 1, :] * cbuf[off + k:off + k + TT, :]
    cbuf[0:HIST, :] = cbuf[TT:TT + HIST, :]
    cn = _ln(conv, lncg_ref[...], lncb_ref[...])
    y_b = _dot((cn * _sigmoid(cn)).astype(_BF16), wb_ref[...]) + bb_ref[...]

    @pl.when(t == pl.num_programs(1) - 1)
    def _():
        cst_ref[...] = glu[TT - (CONV_WIDTH - 1):, :]

    merged = (_sigmoid(proj(4) + bga_ref[...]) * y_a + _sigmoid(proj(5) + bgb_ref[...]) * y_b)
    x1 = x + _dot(merged.astype(_BF16), wout_ref[...])
    x1_ref[...] = x1

    h2 = _rms(x1, gffn_ref[...])
    _store_row_tiles(h2_ref, (), TT, h2)
    meta_i, meta_f = _route(h2, wr_ref, br_ref, carry)
    mi_ref[...] = meta_i
    mf_ref[...] = meta_f
    cnt_ref[...] = carry[...]


def _full(shape):
    return pl.BlockSpec(shape, lambda *_: (0,) * len(shape))


def _mixer_prompt(x, w):
    bsz, seq, d = x.shape
    n_tok = bsz * seq
    nt = seq // TT
    tok = lambda b, t: (b * nt + t, 0)
    weights = (w["gmix"], w["win"], w["lnvg"], w["lnvb"], w["wsp"], w["bsp"], w["wa"], w["cw"],
               w["cb"], w["lncg"], w["lncb"], w["wb"], w["bb"], w["bga"], w["bgb"], w["wout"],
               w["gffn"], w["wr"], w["br"])
    return pl.pallas_call(
        _mixer_prompt_kernel,
        grid=(bsz, nt),
        in_specs=[pl.BlockSpec((None, TT, d), lambda b, t: (b, t, 0))]
                 + [_full(a.shape) for a in weights],
        out_specs=[
            pl.BlockSpec((TT, d), tok),
            pl.BlockSpec((TT * ROW_TILES, LANES), tok),
            pl.BlockSpec((TT, LANES), tok),
            pl.BlockSpec((TT, LANES), tok),
            _full((SUBLANES, LANES)),
            pl.BlockSpec((None, CONV_WIDTH - 1, d), lambda b, t: (b, 0, 0)),
        ],
        out_shape=[
            jax.ShapeDtypeStruct((n_tok, d), _F32),
            jax.ShapeDtypeStruct((n_tok * ROW_TILES, LANES), _F32),
            jax.ShapeDtypeStruct((n_tok, LANES), jnp.int32),
            jax.ShapeDtypeStruct((n_tok, LANES), _F32),
            jax.ShapeDtypeStruct((SUBLANES, LANES), _F32),
            jax.ShapeDtypeStruct((bsz, CONV_WIDTH - 1, d), _F32),
        ],
        scratch_shapes=[pltpu.VMEM((HIST + TT, d), _F32), pltpu.VMEM((SUBLANES, LANES), _F32)],
        compiler_params=pltpu.CompilerParams(
            dimension_semantics=("arbitrary", "arbitrary"), vmem_limit_bytes=VMEM_LIMIT),
        name="mixer_prompt",
    )(x, *weights)


def _mixer_sample_kernel(x_ref, cache_ref, cnt0_ref, gmix_ref, win_ref, lnvg_ref, lnvb_ref,
                         wcs_ref, bcs_ref, wa_ref, cw_ref, cb_ref, lncg_ref, lncb_ref, wb_ref,
                         bb_ref, bga_ref, bgb_ref, wout_ref, gffn_ref, wr_ref, br_ref,
                         x1_ref, h2_ref, mi_ref, mf_ref, cnt_ref, glu_ref, v_ref,
                         carry):
    nt = x_ref.shape[0]
    n = nt * SB

    @pl.when(pl.program_id(0) == 0)
    def _():
        carry[...] = cnt0_ref[...]

    x = x_ref[...].reshape(n, D_MODEL)
    hb = _rms(x, gmix_ref[...]).astype(_BF16)

    def proj(s):
        return _dot(hb, win_ref[:, s * D_MODEL:(s + 1) * D_MODEL])

    u = _gelu(proj(0))
    v = _ln(_gelu(proj(1)), lnvg_ref[...], lnvb_ref[...])
    v_ref[...] = v.reshape(nt, SB, D_MODEL)
    s_rows = []
    for i in range(nt):
        acc = jnp.zeros((SB, D_MODEL), _F32) + bcs_ref[i:i + 1, :]
        for j in range(i + 1):
            acc = acc + wcs_ref[i * nt + j:i * nt + j + 1, :] * v[j * SB:(j + 1) * SB, :]
        s_rows.append(acc)
    s = jnp.concatenate(s_rows, axis=0)
    y_a = _dot((u * s).astype(_BF16), wa_ref[...])

    glu = proj(2) * _sigmoid(proj(3))
    glu_ref[...] = glu.reshape(nt, SB, D_MODEL)
    hist = CONV_WIDTH - 1
    conv_rows = []
    for i in range(nt):
        acc = jnp.zeros((SB, D_MODEL), _F32) + cb_ref[...]
        for k in range(CONV_WIDTH):
            p = i + k
            src = cache_ref[p] if p < hist else glu[(p - hist) * SB:(p - hist + 1) * SB, :]
            acc = acc + cw_ref[k:k + 1, :] * src
        conv_rows.append(acc)
    cn = _ln(jnp.concatenate(conv_rows, axis=0), lncg_ref[...], lncb_ref[...])
    y_b = _dot((cn * _sigmoid(cn)).astype(_BF16), wb_ref[...]) + bb_ref[...]

    merged = (_sigmoid(proj(4) + bga_ref[...]) * y_a + _sigmoid(proj(5) + bgb_ref[...]) * y_b)
    x1 = x + _dot(merged.astype(_BF16), wout_ref[...])
    x1_ref[...] = x1.reshape(nt, SB, D_MODEL)

    h2 = _rms(x1, gffn_ref[...])
    for i in range(nt):
        _store_row_tiles(h2_ref, (i,), SB, h2[i * SB:(i + 1) * SB, :])
    meta_i, meta_f = _route(h2, wr_ref, br_ref, carry)
    mi_ref[...] = meta_i.reshape(nt, SB, LANES)
    mf_ref[...] = meta_f.reshape(nt, SB, LANES)
    cnt_ref[...] = carry[...]


def _mixer_sample(x_t, cache_t, cnt0, w):
    nt, nseq, d = x_t.shape
    hist = cache_t.shape[0]
    weights = (w["gmix"], w["win"], w["lnvg"], w["lnvb"], w["wcs"], w["bcs"], w["wa"], w["cw"],
               w["cb"], w["lncg"], w["lncb"], w["wb"], w["bb"], w["bga"], w["bgb"], w["wout"],
               w["gffn"], w["wr"], w["br"])
    seqs = lambda i: (0, i, 0)
    return pl.pallas_call(
        _mixer_sample_kernel,
        grid=(nseq // SB,),
        in_specs=[pl.BlockSpec((nt, SB, d), seqs), pl.BlockSpec((hist, SB, d), seqs),
                  _full(cnt0.shape)] + [_full(a.shape) for a in weights],
        out_specs=[
            pl.BlockSpec((nt, SB, d), seqs),
            pl.BlockSpec((nt, SB * ROW_TILES, LANES), seqs),
            pl.BlockSpec((nt, SB, LANES), seqs),
            pl.BlockSpec((nt, SB, LANES), seqs),
            _full((SUBLANES, LANES)),
            pl.BlockSpec((nt, SB, d), seqs),
            pl.BlockSpec((nt, SB, d), seqs),
        ],
        out_shape=[
            jax.ShapeDtypeStruct((nt, nseq, d), _F32),
            jax.ShapeDtypeStruct((nt, nseq * ROW_TILES, LANES), _F32),
            jax.ShapeDtypeStruct((nt, nseq, LANES), jnp.int32),
            jax.ShapeDtypeStruct((nt, nseq, LANES), _F32),
            jax.ShapeDtypeStruct((SUBLANES, LANES), _F32),
            jax.ShapeDtypeStruct((nt, nseq, d), _F32),
            jax.ShapeDtypeStruct((nt, nseq, d), _F32),
        ],
        scratch_shapes=[pltpu.VMEM((SUBLANES, LANES), _F32)],
        compiler_params=pltpu.CompilerParams(
            dimension_semantics=("arbitrary",), vmem_limit_bytes=VMEM_LIMIT),
        name="mixer_sample",
    )(x_t, cache_t, cnt0, *weights)


def _row_tile(ref, row):
    return ref.at[pl.ds(pl.multiple_of(row * ROW_TILES, ROW_TILES), ROW_TILES), :]


def _dispatch_kernel(dest_ref, h2_hbm, xs_in_hbm, xs_hbm, sem):
    del xs_in_hbm
    base = pl.program_id(0) * TT

    def issue(t, carry):
        src = _row_tile(h2_hbm, base + t)
        for k in range(TOP_K):
            pltpu.make_async_copy(src, _row_tile(xs_hbm, dest_ref[0, 0, t * TOP_K + k]), sem).start()
        return carry

    lax.fori_loop(0, TT, issue, 0)
    n = TT * TOP_K * ROW_TILES
    pltpu.make_async_copy(h2_hbm.at[pl.ds(0, n), :], xs_hbm.at[pl.ds(0, n), :], sem).wait()


def _dispatch(dest, h2, xs):
    n_tiles = dest.shape[0]
    return pl.pallas_call(
        _dispatch_kernel,
        grid=(n_tiles,),
        in_specs=[
            pl.BlockSpec((1, 1, TT * TOP_K), lambda i: (i, 0, 0), memory_space=pltpu.SMEM),
            pl.BlockSpec(memory_space=pl.ANY),
            pl.BlockSpec(memory_space=pl.ANY),
        ],
        out_specs=pl.BlockSpec(memory_space=pl.ANY),
        out_shape=jax.ShapeDtypeStruct(xs.shape, xs.dtype),
        scratch_shapes=[pltpu.SemaphoreType.DMA],
        input_output_aliases={2: 0},
        compiler_params=pltpu.CompilerParams(
            dimension_semantics=("arbitrary",), has_side_effects=True),
        name="dispatch",
    )(dest, h2, xs)


def _experts_kernel(be_ref, nused_ref, xs_ref, wup_ref, bup_ref, wdn_ref, bdn_ref, out_ref,
                    wup_bf, wdn_bf):
    i = pl.program_id(0)
    prev = be_ref[jnp.maximum(i - 1, 0)]

    @pl.when((i == 0) | (be_ref[i] != prev))
    def _():
        rows = 128

        def cast(r, carry):
            sl = pl.ds(pl.multiple_of(r * rows, rows), rows)
            wup_bf[sl, :] = wup_ref[sl, :].astype(_BF16)
            wdn_bf[sl, :] = wdn_ref[sl, :].astype(_BF16)
            return carry

        lax.fori_loop(0, D_MODEL // rows, cast, 0)

    @pl.when(i < nused_ref[0])
    def _():
        xb = _load_row_tiles(xs_ref, 0, BLK).astype(_BF16)
        gate = _dot(xb, wup_bf[:, :D_EXPERT]) + bup_ref[:, :D_EXPERT]
        up = _dot(xb, wup_bf[:, D_EXPERT:]) + bup_ref[:, D_EXPERT:]
        gate = jnp.minimum(gate, SWIGLU_LIMIT)
        up = jnp.clip(up, -SWIGLU_LIMIT, SWIGLU_LIMIT)
        act = (up + 1.0) * (gate * _sigmoid(SWIGLU_ALPHA * gate))
        o = _dot(act.astype(_BF16), wdn_bf[...]) + bdn_ref[...]
        _store_row_tiles(out_ref, (), BLK, o)

    @pl.when(i >= nused_ref[0])
    def _():
        out_ref[...] = jnp.zeros_like(out_ref)


def _experts(block_expert, nused, xs, w_up, b_up, w_down, b_down):
    n_blk = block_expert.shape[0]
    rows = lambda i, be, nu: (jnp.minimum(i, nu[0] - 1), 0)
    expert = lambda i, be, nu: (be[i], 0, 0)
    d, f2 = w_up.shape[1], w_up.shape[2]
    grid_spec = pltpu.PrefetchScalarGridSpec(
        num_scalar_prefetch=2,
        grid=(n_blk,),
        in_specs=[
            pl.BlockSpec((BLK * ROW_TILES, LANES), rows),
            pl.BlockSpec((None, d, f2), expert),
            pl.BlockSpec((None, 1, f2), expert),
            pl.BlockSpec((None, w_down.shape[1], d), expert),
            pl.BlockSpec((None, 1, d), expert),
        ],
        out_specs=pl.BlockSpec((BLK * ROW_TILES, LANES), lambda i, be, nu: (i, 0)),
        scratch_shapes=[pltpu.VMEM((d, f2), _BF16), pltpu.VMEM((w_down.shape[1], d), _BF16)],
    )
    return pl.pallas_call(
        _experts_kernel,
        grid_spec=grid_spec,
        out_shape=jax.ShapeDtypeStruct(xs.shape, _F32),
        compiler_params=pltpu.CompilerParams(
            dimension_semantics=("arbitrary",), vmem_limit_bytes=VMEM_LIMIT),
        name="experts",
    )(block_expert, nused, xs, w_up, b_up, w_down, b_down)


def _combine_kernel(dest_ref, x1_ref, mf_ref, gfin_ref, outs_hbm, y_ref, buf, sem):
    def issue(t, carry):
        for k in range(TOP_K):
            dst = buf.at[pl.ds(pl.multiple_of((k * TT + t) * ROW_TILES, ROW_TILES), ROW_TILES), :]
            pltpu.make_async_copy(_row_tile(outs_hbm, dest_ref[0, 0, t * TOP_K + k]), dst, sem).start()
        return carry

    lax.fori_loop(0, TT, issue, 0)
    pltpu.make_async_copy(outs_hbm.at[pl.ds(0, buf.shape[0]), :], buf, sem).wait()

    y = x1_ref[...]
    gates = mf_ref[...]
    for k in range(TOP_K):
        y = y + gates[:, k:k + 1] * _load_row_tiles(buf, k * TT * ROW_TILES, TT)
    y_ref[...] = _rms(y, gfin_ref[...])


def _combine(dest, x1, meta_f, g_final, outs):
    n_tok, d = x1.shape
    tok = lambda i: (i, 0)
    return pl.pallas_call(
        _combine_kernel,
        grid=(n_tok // TT,),
        in_specs=[
            pl.BlockSpec((1, 1, TT * TOP_K), lambda i: (i, 0, 0), memory_space=pltpu.SMEM),
            pl.BlockSpec((TT, d), tok),
            pl.BlockSpec((TT, LANES), tok),
            _full(g_final.shape),
            pl.BlockSpec(memory_space=pl.ANY),
        ],
        out_specs=pl.BlockSpec((TT, d), tok),
        out_shape=jax.ShapeDtypeStruct((n_tok, d), _F32),
        scratch_shapes=[pltpu.VMEM((TOP_K * TT * ROW_TILES, LANES), _F32), pltpu.SemaphoreType.DMA],
        compiler_params=pltpu.CompilerParams(
            dimension_semantics=("arbitrary",), vmem_limit_bytes=VMEM_LIMIT),
        name="combine",
    )(dest, x1, meta_f, g_final, outs)


def _prep_weights(norm_mix_g, w_in, ln_v_g, ln_v_b, w_spatial, b_spatial, w_branch_a, conv_w,
                  conv_b, ln_conv_g, ln_conv_b, w_branch_b, b_branch_b, b_gate_a, b_gate_b, w_out,
                  norm_ffn_g, w_router, b_router, n_new):
    row = lambda a: a.reshape(1, -1)
    pad = LANES - N_EXPERTS
    wcs = jnp.repeat(w_spatial[:, :n_new, :n_new].transpose(1, 2, 0).reshape(n_new * n_new, A_GROUPS),
                     LANES, axis=1)
    return dict(
        gmix=row(norm_mix_g), win=w_in.astype(_BF16), lnvg=row(ln_v_g), lnvb=row(ln_v_b),
        wsp=w_spatial, bsp=jnp.repeat(b_spatial.T, LANES, axis=1), wcs=wcs,
        bcs=jnp.repeat(b_spatial[:, :n_new].T, LANES, axis=1),
        wa=w_branch_a.astype(_BF16), cw=conv_w, cb=row(conv_b), lncg=row(ln_conv_g),
        lncb=row(ln_conv_b), wb=w_branch_b.astype(_BF16), bb=row(b_branch_b), bga=row(b_gate_a),
        bgb=row(b_gate_b), wout=w_out.astype(_BF16), gffn=row(norm_ffn_g),
        wr=jnp.pad(w_router, ((0, 0), (0, pad))), br=jnp.pad(row(b_router), ((0, 0), (0, pad))))


def kernel(x_prompt, x_sample, cache_conv, norm_mix_g, w_in, ln_v_g, ln_v_b, w_spatial, b_spatial, w_branch_a, conv_w, conv_b, ln_conv_g, ln_conv_b, w_branch_b, b_branch_b, b_gate_a, b_gate_b, w_out, norm_ffn_g, w_router, b_router, w_up, b_up, w_down, b_down, norm_final_g):
    assert w_in.shape[0] == 1, "single trunk layer"
    bsz, seq, d = x_prompt.shape
    nseq, n_new, _ = x_sample.shape
    assert d == D_MODEL and seq % TT == 0 and TT % CHUNK == 0 and nseq % SB == 0
    assert n_new <= CHUNK and (nseq * n_new) % TT == 0
    w = _prep_weights(norm_mix_g[0], w_in[0], ln_v_g[0], ln_v_b[0], w_spatial[0], b_spatial[0],
                      w_branch_a[0], conv_w[0], conv_b[0], ln_conv_g[0], ln_conv_b[0],
                      w_branch_b[0], b_branch_b[0], b_gate_a[0], b_gate_b[0], w_out[0],
                      norm_ffn_g[0], w_router[0], b_router[0], n_new)

    x1_p, h2_p, mi_p, mf_p, cnt_p, cst_p = _mixer_prompt(x_prompt, w)
    x_t = x_sample.transpose(1, 0, 2)
    cache_t = cache_conv[0].transpose(1, 0, 2)
    x1_s, h2_s, mi_s, mf_s, cnt, glu_s, v_s = _mixer_sample(x_t, cache_t, cnt_p, w)
    n_p, n_s = bsz * seq, nseq * n_new
    x1_s = x1_s.reshape(n_s, d)
    h2_s = h2_s.reshape(n_s * ROW_TILES, LANES)
    mi_s = mi_s.reshape(n_s, LANES)
    mf_s = mf_s.reshape(n_s, LANES)

    counts = cnt[0, :N_EXPERTS].astype(jnp.int32)
    padded = ((counts + BLK - 1) // BLK) * BLK
    pend = jnp.cumsum(padded)
    pstart = pend - padded
    n_assign = (n_p + n_s) * TOP_K
    n_blk = -(-(n_assign + N_EXPERTS * (BLK - 1)) // BLK)
    nused = (pend[-1] // BLK).astype(jnp.int32)
    blk_ids = jnp.arange(n_blk, dtype=jnp.int32)
    be = jnp.clip(jnp.searchsorted(pend, blk_ids * BLK, side="right"), 0, N_EXPERTS - 1)
    be = jnp.where(blk_ids < nused, be, be[jnp.maximum(nused - 1, 0)]).astype(jnp.int32)

    def dest_of(mi):
        dest = pstart[mi[:, :TOP_K]] + mi[:, TOP_K:2 * TOP_K]
        return dest.reshape(-1, 1, TT * TOP_K)

    dest_p, dest_s = dest_of(mi_p), dest_of(mi_s)
    xs = jnp.zeros((n_blk * BLK * ROW_TILES, LANES), _F32)
    xs = _dispatch(dest_p, h2_p, xs)
    xs = _dispatch(dest_s, h2_s, xs)
    outs = _experts(be, nused.reshape(1), xs, w_up[0], b_up[0][:, None, :], w_down[0],
                    b_down[0][:, None, :])
    g_final = norm_final_g.reshape(1, d)
    y_p = _combine(dest_p, x1_p, mf_p, g_final, outs)
    y_s = _combine(dest_s, x1_s, mf_s, g_final, outs)

    y_prompt = y_p.reshape(bsz, seq, d)
    y_sample = y_s.reshape(n_new, nseq, d).transpose(1, 0, 2)
    conv_state_prompt = cst_p[None]
    glu_new = glu_s.transpose(1, 0, 2)
    conv_state_sample = jnp.concatenate([cache_conv[0][:, n_new:], glu_new], axis=1)[None]
    chunk_v_sample = v_s.transpose(1, 0, 2)[None]
    return (y_prompt, y_sample, conv_state_prompt, conv_state_sample, chunk_v_sample)
```

```python
import functools

import jax
import jax.numpy as jnp
from jax import lax
from jax.experimental import pallas as pl
from jax.experimental.pallas import tpu as pltpu

D_MODEL = 1024
CHUNK = 128
A_GROUPS = 8
CONV_WIDTH = 31
HIST = 32
N_EXPERTS = 32
TOP_K = 4
D_EXPERT = 1024
SWIGLU_LIMIT = 7.0
SWIGLU_ALPHA = 1.702
EPS = 1e-5
LANES = 128
SUBLANES = 8
ROW_TILES = D_MODEL // LANES
TT = 256
SB = 32
BLK = 256
VMEM_LIMIT = 56 * 1024 * 1024

_F32 = jnp.float32
_BF16 = jnp.bfloat16
_INV_SQRT2 = 0.7071067811865476


def _rms(x, g):
    return x * lax.rsqrt(jnp.mean(x * x, axis=-1, keepdims=True) + EPS) * g


def _ln(x, g, b):
    mu = jnp.mean(x, axis=-1, keepdims=True)
    xc = x - mu
    var = jnp.mean(xc * xc, axis=-1, keepdims=True)
    return xc * lax.rsqrt(var + EPS) * g + b


def _gelu(x):
    return 0.5 * x * (1.0 + lax.erf(x * _INV_SQRT2))


def _sigmoid(x):
    return 1.0 / (1.0 + jnp.exp(-x))


def _dot(a, b):
    return jnp.dot(a, b, preferred_element_type=_F32)


def _store_row_tiles(ref, lead, n, val):
    for j in range(ROW_TILES):
        ref[lead + (pl.ds(j, n, stride=ROW_TILES), slice(None))] = val[:, j * LANES:(j + 1) * LANES]


def _load_row_tiles(ref, start, n):
    return jnp.concatenate(
        [ref[pl.ds(start + j, n, stride=ROW_TILES), :] for j in range(ROW_TILES)], axis=1)


def _route(h2, wr_ref, br_ref, carry_ref):
    n = h2.shape[0]
    logits = jnp.dot(h2, wr_ref[...], preferred_element_type=_F32,
                     precision=lax.Precision.HIGHEST) + br_ref[...]
    lane = lax.broadcasted_iota(jnp.int32, (n, LANES), 1)
    l = jnp.where(lane < N_EXPERTS, logits, -jnp.inf)
    onehots, vals, idxs = [], [], []
    for _ in range(TOP_K):
        m = jnp.max(l, axis=-1, keepdims=True)
        idx = jnp.min(jnp.where(l == m, lane, LANES), axis=-1, keepdims=True)
        oh = lane == idx
        onehots.append(oh)
        vals.append(m)
        idxs.append(idx)
        l = jnp.where(oh, -jnp.inf, l)
    exps = [jnp.exp(v - vals[0]) for v in vals]
    denom = exps[0] + exps[1] + exps[2] + exps[3]
    gates = [e / denom for e in exps]

    oh_all = jnp.zeros((n, LANES), _F32)
    for oh in onehots:
        oh_all = oh_all + jnp.where(oh, 1.0, 0.0)
    ri = lax.broadcasted_iota(jnp.int32, (n, n), 0)
    ci = lax.broadcasted_iota(jnp.int32, (n, n), 1)
    tri = jnp.where(ci < ri, 1.0, 0.0).astype(_BF16)
    base = _dot(tri, oh_all.astype(_BF16)) + carry_ref[0:1, :]
    ranks = [jnp.sum(jnp.where(oh, base, 0.0), axis=-1, keepdims=True) for oh in onehots]
    carry_ref[...] = carry_ref[...] + jnp.sum(oh_all, axis=0, keepdims=True)

    meta_i = jnp.zeros((n, LANES), jnp.int32)
    meta_f = jnp.zeros((n, LANES), _F32)
    for k in range(TOP_K):
        meta_i = jnp.where(lane == k, idxs[k], meta_i)
        meta_i = jnp.where(lane == TOP_K + k, ranks[k].astype(jnp.int32), meta_i)
        meta_f = jnp.where(lane == k, gates[k], meta_f)
    return meta_i, meta_f


def _mixer_prompt_kernel(x_ref, gmix_ref, win_ref, lnvg_ref, lnvb_ref, wsp_ref, bsp_ref, wa_ref,
                         cw_ref, cb_ref, lncg_ref, lncb_ref, wb_ref, bb_ref, bga_ref, bgb_ref,
                         wout_ref, gffn_ref, wr_ref, br_ref,
                         x1_ref, h2_ref, mi_ref, mf_ref, cnt_ref, cst_ref,
                         cbuf, carry):
    b = pl.program_id(0)
    t = pl.program_id(1)

    @pl.when((b == 0) & (t == 0))
    def _():
        carry[...] = jnp.zeros_like(carry)

    @pl.when(t == 0)
    def _():
        cbuf[0:HIST, :] = jnp.zeros((HIST, D_MODEL), _F32)

    x = x_ref[...]
    hb = _rms(x, gmix_ref[...]).astype(_BF16)

    def proj(s):
        return _dot(hb, win_ref[:, s * D_MODEL:(s + 1) * D_MODEL])

    u = _gelu(proj(0))
    v = _ln(_gelu(proj(1)), lnvg_ref[...], lnvb_ref[...])
    ri = lax.broadcasted_iota(jnp.int32, (CHUNK, CHUNK), 0)
    ci = lax.broadcasted_iota(jnp.int32, (CHUNK, CHUNK), 1)
    s_rows = []
    for c in range(TT // CHUNK):
        s_cols = []
        for g in range(A_GROUPS):
            wm = jnp.where(ci <= ri, wsp_ref[g], 0.0).astype(_BF16)
            vb = v[c * CHUNK:(c + 1) * CHUNK, g * LANES:(g + 1) * LANES].astype(_BF16)
            s_cols.append(_dot(wm, vb))
        s_rows.append(jnp.concatenate(s_cols, axis=1) + bsp_ref[...])
    s = jnp.concatenate(s_rows, axis=0)
    y_a = _dot((u * s).astype(_BF16), wa_ref[...])

    glu = proj(2) * _sigmoid(proj(3))
    cbuf[HIST:HIST + TT, :] = glu
    conv = jnp.zeros((TT, D_MODEL), _F32) + cb_ref[...]
    off = HIST - (CONV_WIDTH - 1)
    for k in range(CONV_WIDTH):
        conv = conv + cw_ref[k:k + 1, :] * cbuf[off + k:off + k + TT, :]
    cbuf[0:HIST, :] = cbuf[TT:TT + HIST, :]
    cn = _ln(conv, lncg_ref[...], lncb_ref[...])
    y_b = _dot((cn * _sigmoid(cn)).astype(_BF16), wb_ref[...]) + bb_ref[...]

    @pl.when(t == pl.num_programs(1) - 1)
    def _():
        cst_ref[...] = glu[TT - (CONV_WIDTH - 1):, :]

    merged = (_sigmoid(proj(4) + bga_ref[...]) * y_a + _sigmoid(proj(5) + bgb_ref[...]) * y_b)
    x1 = x + _dot(merged.astype(_BF16), wout_ref[...])
    x1_ref[...] = x1

    h2 = _rms(x1, gffn_ref[...])
    _store_row_tiles(h2_ref, (), TT, h2)
    meta_i, meta_f = _route(h2, wr_ref, br_ref, carry)
    mi_ref[...] = meta_i
    mf_ref[...] = meta_f
    cnt_ref[...] = carry[...]


def _full(shape):
    return pl.BlockSpec(shape, lambda *_: (0,) * len(shape))


def _mixer_prompt(x, w):
    bsz, seq, d = x.shape
    n_tok = bsz * seq
    nt = seq // TT
    tok = lambda b, t: (b * nt + t, 0)
    weights = (w["gmix"], w["win"], w["lnvg"], w["lnvb"], w["wsp"], w["bsp"], w["wa"], w["cw"],
               w["cb"], w["lncg"], w["lncb"], w["wb"], w["bb"], w["bga"], w["bgb"], w["wout"],
               w["gffn"], w["wr"], w["br"])
    return pl.pallas_call(
        _mixer_prompt_kernel,
        grid=(bsz, nt),
        in_specs=[pl.BlockSpec((None, TT, d), lambda b, t: (b, t, 0))]
                 + [_full(a.shape) for a in weights],
        out_specs=[
            pl.BlockSpec((TT, d), tok),
            pl.BlockSpec((TT * ROW_TILES, LANES), tok),
            pl.BlockSpec((TT, LANES), tok),
            pl.BlockSpec((TT, LANES), tok),
            _full((SUBLANES, LANES)),
            pl.BlockSpec((None, CONV_WIDTH - 1, d), lambda b, t: (b, 0, 0)),
        ],
        out_shape=[
            jax.ShapeDtypeStruct((n_tok, d), _F32),
            jax.ShapeDtypeStruct((n_tok * ROW_TILES, LANES), _F32),
            jax.ShapeDtypeStruct((n_tok, LANES), jnp.int32),
            jax.ShapeDtypeStruct((n_tok, LANES), _F32),
            jax.ShapeDtypeStruct((SUBLANES, LANES), _F32),
            jax.ShapeDtypeStruct((bsz, CONV_WIDTH - 1, d), _F32),
        ],
        scratch_shapes=[pltpu.VMEM((HIST + TT, d), _F32), pltpu.VMEM((SUBLANES, LANES), _F32)],
        compiler_params=pltpu.CompilerParams(
            dimension_semantics=("arbitrary", "arbitrary"), vmem_limit_bytes=VMEM_LIMIT),
        name="mixer_prompt",
    )(x, *weights)


def _mixer_sample_kernel(x_ref, cache_ref, cnt0_ref, gmix_ref, win_ref, lnvg_ref, lnvb_ref,
                         wcs_ref, bcs_ref, wa_ref, cw_ref, cb_ref, lncg_ref, lncb_ref, wb_ref,
                         bb_ref, bga_ref, bgb_ref, wout_ref, gffn_ref, wr_ref, br_ref,
                         x1_ref, h2_ref, mi_ref, mf_ref, cnt_ref, glu_ref, v_ref,
                         carry):
    nt = x_ref.shape[0]
    n = nt * SB

    @pl.when(pl.program_id(0) == 0)
    def _():
        carry[...] = cnt0_ref[...]

    x = x_ref[...].reshape(n, D_MODEL)
    hb = _rms(x, gmix_ref[...]).astype(_BF16)

    def proj(s):
        return _dot(hb, win_ref[:, s * D_MODEL:(s + 1) * D_MODEL])

    u = _gelu(proj(0))
    v = _ln(_gelu(proj(1)), lnvg_ref[...], lnvb_ref[...])
    v_ref[...] = v.reshape(nt, SB, D_MODEL)
    s_rows = []
    for i in range(nt):
        acc = jnp.zeros((SB, D_MODEL), _F32) + bcs_ref[i:i + 1, :]
        for j in range(i + 1):
            acc = acc + wcs_ref[i * nt + j:i * nt + j + 1, :] * v[j * SB:(j + 1) * SB, :]
        s_rows.append(acc)
    s = jnp.concatenate(s_rows, axis=0)
    y_a = _dot((u * s).astype(_BF16), wa_ref[...])

    glu = proj(2) * _sigmoid(proj(3))
    glu_ref[...] = glu.reshape(nt, SB, D_MODEL)
    hist = CONV_WIDTH - 1
    conv_rows = []
    for i in range(nt):
        acc = jnp.zeros((SB, D_MODEL), _F32) + cb_ref[...]
        for k in range(CONV_WIDTH):
            p = i + k
            src = cache_ref[p] if p < hist else glu[(p - hist) * SB:(p - hist + 1) * SB, :]
            acc = acc + cw_ref[k:k + 1, :] * src
        conv_rows.append(acc)
    cn = _ln(jnp.concatenate(conv_rows, axis=0), lncg_ref[...], lncb_ref[...])
    y_b = _dot((cn * _sigmoid(cn)).astype(_BF16), wb_ref[...]) + bb_ref[...]

    merged = (_sigmoid(proj(4) + bga_ref[...]) * y_a + _sigmoid(proj(5) + bgb_ref[...]) * y_b)
    x1 = x + _dot(merged.astype(_BF16), wout_ref[...])
    x1_ref[...] = x1.reshape(nt, SB, D_MODEL)

    h2 = _rms(x1, gffn_ref[...])
    for i in range(nt):
        _store_row_tiles(h2_ref, (i,), SB, h2[i * SB:(i + 1) * SB, :])
    meta_i, meta_f = _route(h2, wr_ref, br_ref, carry)
    mi_ref[...] = meta_i.reshape(nt, SB, LANES)
    mf_ref[...] = meta_f.reshape(nt, SB, LANES)
    cnt_ref[...] = carry[...]


def _mixer_sample(x_t, cache_t, cnt0, w):
    nt, nseq, d = x_t.shape
    hist = cache_t.shape[0]
    weights = (w["gmix"], w["win"], w["lnvg"], w["lnvb"], w["wcs"], w["bcs"], w["wa"], w["cw"],
               w["cb"], w["lncg"], w["lncb"], w["wb"], w["bb"], w["bga"], w["bgb"], w["wout"],
               w["gffn"], w["wr"], w["br"])
    seqs = lambda i: (0, i, 0)
    return pl.pallas_call(
        _mixer_sample_kernel,
        grid=(nseq // SB,),
        in_specs=[pl.BlockSpec((nt, SB, d), seqs), pl.BlockSpec((hist, SB, d), seqs),
                  _full(cnt0.shape)] + [_full(a.shape) for a in weights],
        out_specs=[
            pl.BlockSpec((nt, SB, d), seqs),
            pl.BlockSpec((nt, SB * ROW_TILES, LANES), seqs),
            pl.BlockSpec((nt, SB, LANES), seqs),
            pl.BlockSpec((nt, SB, LANES), seqs),
            _full((SUBLANES, LANES)),
            pl.BlockSpec((nt, SB, d), seqs),
            pl.BlockSpec((nt, SB, d), seqs),
        ],
        out_shape=[
            jax.ShapeDtypeStruct((nt, nseq, d), _F32),
            jax.ShapeDtypeStruct((nt, nseq * ROW_TILES, LANES), _F32),
            jax.ShapeDtypeStruct((nt, nseq, LANES), jnp.int32),
            jax.ShapeDtypeStruct((nt, nseq, LANES), _F32),
            jax.ShapeDtypeStruct((SUBLANES, LANES), _F32),
            jax.ShapeDtypeStruct((nt, nseq, d), _F32),
            jax.ShapeDtypeStruct((nt, nseq, d), _F32),
        ],
        scratch_shapes=[pltpu.VMEM((SUBLANES, LANES), _F32)],
        compiler_params=pltpu.CompilerParams(
            dimension_semantics=("arbitrary",), vmem_limit_bytes=VMEM_LIMIT),
        name="mixer_sample",
    )(x_t, cache_t, cnt0, *weights)


def _row_tile(ref, row):
    return ref.at[pl.ds(pl.multiple_of(row * ROW_TILES, ROW_TILES), ROW_TILES), :]


def _dispatch_kernel(dest_ref, h2_ref, xs_in_hbm, xs_hbm, sem):
    del xs_in_hbm

    def issue(t, carry):
        src = _row_tile(h2_ref, t)
        for k in range(TOP_K):
            pltpu.make_async_copy(src, _row_tile(xs_hbm, dest_ref[0, 0, t * TOP_K + k]), sem).start()
        return carry

    lax.fori_loop(0, TT, issue, 0)
    for _ in range(TOP_K):
        pltpu.make_async_copy(h2_ref, xs_hbm.at[pl.ds(0, TT * ROW_TILES), :], sem).wait()


def _dispatch(dest, h2, xs):
    n_tiles = dest.shape[0]
    return pl.pallas_call(
        _dispatch_kernel,
        grid=(n_tiles,),
        in_specs=[
            pl.BlockSpec((1, 1, TT * TOP_K), lambda i: (i, 0, 0), memory_space=pltpu.SMEM),
            pl.BlockSpec((TT * ROW_TILES, LANES), lambda i: (i, 0)),
            pl.BlockSpec(memory_space=pl.ANY),
        ],
        out_specs=pl.BlockSpec(memory_space=pl.ANY),
        out_shape=jax.ShapeDtypeStruct(xs.shape, xs.dtype),
        scratch_shapes=[pltpu.SemaphoreType.DMA],
        input_output_aliases={2: 0},
        compiler_params=pltpu.CompilerParams(
            dimension_semantics=("arbitrary",), has_side_effects=True),
        name="dispatch",
    )(dest, h2, xs)


def _experts_kernel(be_ref, nused_ref, xs_ref, wup_ref, bup_ref, wdn_ref, bdn_ref, out_ref,
                    wup_bf, wdn_bf):
    i = pl.program_id(0)
    prev = be_ref[jnp.maximum(i - 1, 0)]

    @pl.when((i == 0) | (be_ref[i] != prev))
    def _():
        rows = 128

        def cast(r, carry):
            sl = pl.ds(pl.multiple_of(r * rows, rows), rows)
            wup_bf[sl, :] = wup_ref[sl, :].astype(_BF16)
            wdn_bf[sl, :] = wdn_ref[sl, :].astype(_BF16)
            return carry

        lax.fori_loop(0, D_MODEL // rows, cast, 0)

    @pl.when(i < nused_ref[0])
    def _():
        xb = _load_row_tiles(xs_ref, 0, BLK).astype(_BF16)
        gate = _dot(xb, wup_bf[:, :D_EXPERT]) + bup_ref[:, :D_EXPERT]
        up = _dot(xb, wup_bf[:, D_EXPERT:]) + bup_ref[:, D_EXPERT:]
        gate = jnp.minimum(gate, SWIGLU_LIMIT)
        up = jnp.clip(up, -SWIGLU_LIMIT, SWIGLU_LIMIT)
        act = (up + 1.0) * (gate * _sigmoid(SWIGLU_ALPHA * gate))
        o = _dot(act.astype(_BF16), wdn_bf[...]) + bdn_ref[...]
        _store_row_tiles(out_ref, (), BLK, o)

    @pl.when(i >= nused_ref[0])
    def _():
        out_ref[...] = jnp.zeros_like(out_ref)


def _experts(block_expert, nused, xs, w_up, b_up, w_down, b_down):
    n_blk = block_expert.shape[0]
    rows = lambda i, be, nu: (jnp.minimum(i, nu[0] - 1), 0)
    expert = lambda i, be, nu: (be[i], 0, 0)
    d, f2 = w_up.shape[1], w_up.shape[2]
    grid_spec = pltpu.PrefetchScalarGridSpec(
        num_scalar_prefetch=2,
        grid=(n_blk,),
        in_specs=[
            pl.BlockSpec((BLK * ROW_TILES, LANES), rows),
            pl.BlockSpec((None, d, f2), expert),
            pl.BlockSpec((None, 1, f2), expert),
            pl.BlockSpec((None, w_down.shape[1], d), expert),
            pl.BlockSpec((None, 1, d), expert),
        ],
        out_specs=pl.BlockSpec((BLK * ROW_TILES, LANES), lambda i, be, nu: (i, 0)),
        scratch_shapes=[pltpu.VMEM((d, f2), _BF16), pltpu.VMEM((w_down.shape[1], d), _BF16)],
    )
    return pl.pallas_call(
        _experts_kernel,
        grid_spec=grid_spec,
        out_shape=jax.ShapeDtypeStruct(xs.shape, _F32),
        compiler_params=pltpu.CompilerParams(
            dimension_semantics=("arbitrary",), vmem_limit_bytes=VMEM_LIMIT),
        name="experts",
    )(block_expert, nused, xs, w_up, b_up, w_down, b_down)


def _combine_kernel(dest_ref, x1_ref, mf_ref, gfin_ref, outs_hbm, y_ref, buf, sem):
    def issue(t, carry):
        for k in range(TOP_K):
            dst = buf.at[pl.ds(pl.multiple_of((k * TT + t) * ROW_TILES, ROW_TILES), ROW_TILES), :]
            pltpu.make_async_copy(_row_tile(outs_hbm, dest_ref[0, 0, t * TOP_K + k]), dst, sem).start()
        return carry

    lax.fori_loop(0, TT, issue, 0)
    pltpu.make_async_copy(outs_hbm.at[pl.ds(0, buf.shape[0]), :], buf, sem).wait()

    y = x1_ref[...]
    gates = mf_ref[...]
    for k in range(TOP_K):
        y = y + gates[:, k:k + 1] * _load_row_tiles(buf, k * TT * ROW_TILES, TT)
    y_ref[...] = _rms(y, gfin_ref[...])


def _combine(dest, x1, meta_f, g_final, outs):
    n_tok, d = x1.shape
    tok = lambda i: (i, 0)
    return pl.pallas_call(
        _combine_kernel,
        grid=(n_tok // TT,),
        in_specs=[
            pl.BlockSpec((1, 1, TT * TOP_K), lambda i: (i, 0, 0), memory_space=pltpu.SMEM),
            pl.BlockSpec((TT, d), tok),
            pl.BlockSpec((TT, LANES), tok),
            _full(g_final.shape),
            pl.BlockSpec(memory_space=pl.ANY),
        ],
        out_specs=pl.BlockSpec((TT, d), tok),
        out_shape=jax.ShapeDtypeStruct((n_tok, d), _F32),
        scratch_shapes=[pltpu.VMEM((TOP_K * TT * ROW_TILES, LANES), _F32), pltpu.SemaphoreType.DMA],
        compiler_params=pltpu.CompilerParams(
            dimension_semantics=("arbitrary",), vmem_limit_bytes=VMEM_LIMIT),
        name="combine",
    )(dest, x1, meta_f, g_final, outs)


def _prep_weights(norm_mix_g, w_in, ln_v_g, ln_v_b, w_spatial, b_spatial, w_branch_a, conv_w,
                  conv_b, ln_conv_g, ln_conv_b, w_branch_b, b_branch_b, b_gate_a, b_gate_b, w_out,
                  norm_ffn_g, w_router, b_router, n_new):
    row = lambda a: a.reshape(1, -1)
    pad = LANES - N_EXPERTS
    wcs = jnp.repeat(w_spatial[:, :n_new, :n_new].transpose(1, 2, 0).reshape(n_new * n_new, A_GROUPS),
                     LANES, axis=1)
    return dict(
        gmix=row(norm_mix_g), win=w_in.astype(_BF16), lnvg=row(ln_v_g), lnvb=row(ln_v_b),
        wsp=w_spatial, bsp=jnp.repeat(b_spatial.T, LANES, axis=1), wcs=wcs,
        bcs=jnp.repeat(b_spatial[:, :n_new].T, LANES, axis=1),
        wa=w_branch_a.astype(_BF16), cw=conv_w, cb=row(conv_b), lncg=row(ln_conv_g),
        lncb=row(ln_conv_b), wb=w_branch_b.astype(_BF16), bb=row(b_branch_b), bga=row(b_gate_a),
        bgb=row(b_gate_b), wout=w_out.astype(_BF16), gffn=row(norm_ffn_g),
        wr=jnp.pad(w_router, ((0, 0), (0, pad))), br=jnp.pad(row(b_router), ((0, 0), (0, pad))))


def kernel(x_prompt, x_sample, cache_conv, norm_mix_g, w_in, ln_v_g, ln_v_b, w_spatial, b_spatial, w_branch_a, conv_w, conv_b, ln_conv_g, ln_conv_b, w_branch_b, b_branch_b, b_gate_a, b_gate_b, w_out, norm_ffn_g, w_router, b_router, w_up, b_up, w_down, b_down, norm_final_g):
    assert w_in.shape[0] == 1, "single trunk layer"
    bsz, seq, d = x_prompt.shape
    nseq, n_new, _ = x_sample.shape
    assert d == D_MODEL and seq % TT == 0 and TT % CHUNK == 0 and nseq % SB == 0
    assert n_new <= CHUNK and (nseq * n_new) % TT == 0
    w = _prep_weights(norm_mix_g[0], w_in[0], ln_v_g[0], ln_v_b[0], w_spatial[0], b_spatial[0],
                      w_branch_a[0], conv_w[0], conv_b[0], ln_conv_g[0], ln_conv_b[0],
                      w_branch_b[0], b_branch_b[0], b_gate_a[0], b_gate_b[0], w_out[0],
                      norm_ffn_g[0], w_router[0], b_router[0], n_new)

    x1_p, h2_p, mi_p, mf_p, cnt_p, cst_p = _mixer_prompt(x_prompt, w)
    x_t = x_sample.transpose(1, 0, 2)
    cache_t = cache_conv[0].transpose(1, 0, 2)
    x1_s, h2_s, mi_s, mf_s, cnt, glu_s, v_s = _mixer_sample(x_t, cache_t, cnt_p, w)
    n_p, n_s = bsz * seq, nseq * n_new
    x1_s = x1_s.reshape(n_s, d)
    h2_s = h2_s.reshape(n_s * ROW_TILES, LANES)
    mi_s = mi_s.reshape(n_s, LANES)
    mf_s = mf_s.reshape(n_s, LANES)

    counts = cnt[0, :N_EXPERTS].astype(jnp.int32)
    padded = ((counts + BLK - 1) // BLK) * BLK
    pend = jnp.cumsum(padded)
    pstart = pend - padded
    n_assign = (n_p + n_s) * TOP_K
    n_blk = -(-(n_assign + N_EXPERTS * (BLK - 1)) // BLK)
    nused = (pend[-1] // BLK).astype(jnp.int32)
    blk_ids = jnp.arange(n_blk, dtype=jnp.int32)
    first_row = jnp.minimum(blk_ids, nused - 1) * BLK
    be = jnp.sum((pend[None, :] <= first_row[:, None]).astype(jnp.int32), axis=1)
    be = jnp.minimum(be, N_EXPERTS - 1)
    experts_row = jnp.arange(N_EXPERTS, dtype=jnp.int32)

    def dest_of(mi):
        sel = mi[:, :TOP_K, None] == experts_row[None, None, :]
        dest = jnp.sum(jnp.where(sel, pstart[None, None, :], 0), axis=-1) + mi[:, TOP_K:2 * TOP_K]
        return dest.reshape(-1, 1, TT * TOP_K)

    dest_p, dest_s = dest_of(mi_p), dest_of(mi_s)
    xs = jnp.zeros((n_blk * BLK * ROW_TILES, LANES), _F32)
    xs = _dispatch(dest_p, h2_p, xs)
    xs = _dispatch(dest_s, h2_s, xs)
    outs = _experts(be, nused.reshape(1), xs, w_up[0], b_up[0][:, None, :], w_down[0],
                    b_down[0][:, None, :])
    g_final = norm_final_g.reshape(1, d)
    y_p = _combine(dest_p, x1_p, mf_p, g_final, outs)
    y_s = _combine(dest_s, x1_s, mf_s, g_final, outs)

    y_prompt = y_p.reshape(bsz, seq, d)
    y_sample = y_s.reshape(n_new, nseq, d).transpose(1, 0, 2)
    conv_state_prompt = cst_p[None]
    glu_new = glu_s.transpose(1, 0, 2)
    conv_state_sample = jnp.concatenate([cache_conv[0][:, n_new:], glu_new], axis=1)[None]
    chunk_v_sample = v_s.transpose(1, 0, 2)[None]
    return (y_prompt, y_sample, conv_state_prompt, conv_state_sample, chunk_v_sample)
```

```python
import functools

import jax
import jax.numpy as jnp
from jax import lax
from jax.experimental import pallas as pl
from jax.experimental.pallas import tpu as pltpu

D_MODEL = 1024
CHUNK = 128
A_GROUPS = 8
CONV_WIDTH = 31
HIST = 32
N_EXPERTS = 32
TOP_K = 4
D_EXPERT = 1024
SWIGLU_LIMIT = 7.0
SWIGLU_ALPHA = 1.702
EPS = 1e-5
LANES = 128
SUBLANES = 8
ROW_TILES = D_MODEL // LANES
TT = 256
SB = 32
BLK = 512
VMEM_LIMIT = 56 * 1024 * 1024

_F32 = jnp.float32
_BF16 = jnp.bfloat16
_INV_SQRT2 = 0.7071067811865476


def _rms(x, g):
    return x * lax.rsqrt(jnp.mean(x * x, axis=-1, keepdims=True) + EPS) * g


def _ln(x, g, b):
    mu = jnp.mean(x, axis=-1, keepdims=True)
    xc = x - mu
    var = jnp.mean(xc * xc, axis=-1, keepdims=True)
    return xc * lax.rsqrt(var + EPS) * g + b


def _gelu(x):
    return 0.5 * x * (1.0 + lax.erf(x * _INV_SQRT2))


def _sigmoid(x):
    return 1.0 / (1.0 + jnp.exp(-x))


def _dot(a, b):
    return jnp.dot(a, b, preferred_element_type=_F32)


def _store_row_tiles(ref, lead, n, val):
    for j in range(ROW_TILES):
        ref[lead + (pl.ds(j, n, stride=ROW_TILES), slice(None))] = val[:, j * LANES:(j + 1) * LANES]


def _load_row_tiles(ref, start, n):
    return jnp.concatenate(
        [ref[pl.ds(start + j, n, stride=ROW_TILES), :] for j in range(ROW_TILES)], axis=1)


def _route(h2, wr_ref, br_ref, carry_ref):
    n = h2.shape[0]
    h_hi = h2.astype(_BF16)
    h_lo = (h2 - h_hi.astype(_F32)).astype(_BF16)
    by_hi = _dot(h_hi, wr_ref[...])
    logits = (by_hi[:, :LANES] + by_hi[:, LANES:] + _dot(h_lo, wr_ref[:, :LANES])) + br_ref[...]
    lane = lax.broadcasted_iota(jnp.int32, (n, LANES), 1)
    l = jnp.where(lane < N_EXPERTS, logits, -jnp.inf)
    onehots, vals, idxs = [], [], []
    for _ in range(TOP_K):
        m = jnp.max(l, axis=-1, keepdims=True)
        idx = jnp.min(jnp.where(l == m, lane, LANES), axis=-1, keepdims=True)
        oh = lane == idx
        onehots.append(oh)
        vals.append(m)
        idxs.append(idx)
        l = jnp.where(oh, -jnp.inf, l)
    exps = [jnp.exp(v - vals[0]) for v in vals]
    denom = exps[0] + exps[1] + exps[2] + exps[3]
    gates = [e / denom for e in exps]

    oh_all = jnp.zeros((n, LANES), _F32)
    for oh in onehots:
        oh_all = oh_all + jnp.where(oh, 1.0, 0.0)
    ri = lax.broadcasted_iota(jnp.int32, (n, n), 0)
    ci = lax.broadcasted_iota(jnp.int32, (n, n), 1)
    tri = jnp.where(ci < ri, 1.0, 0.0).astype(_BF16)
    base = _dot(tri, oh_all.astype(_BF16)) + carry_ref[0:1, :]
    ranks = [jnp.sum(jnp.where(oh, base, 0.0), axis=-1, keepdims=True) for oh in onehots]
    carry_ref[...] = carry_ref[...] + jnp.sum(oh_all, axis=0, keepdims=True)

    meta_i = jnp.zeros((n, LANES), jnp.int32)
    meta_f = jnp.zeros((n, LANES), _F32)
    for k in range(TOP_K):
        meta_i = jnp.where(lane == k, idxs[k], meta_i)
        meta_i = jnp.where(lane == TOP_K + k, ranks[k].astype(jnp.int32), meta_i)
        meta_f = jnp.where(lane == k, gates[k], meta_f)
    return meta_i, meta_f


def _mixer_prompt_kernel(x_ref, gmix_ref, win_ref, lnvg_ref, lnvb_ref, wsp_ref, bsp_ref, wa_ref,
                         cw_ref, cb_ref, lncg_ref, lncb_ref, wb_ref, bb_ref, bga_ref, bgb_ref,
                         wout_ref, gffn_ref, wr_ref, br_ref,
                         x1_ref, h2_ref, mi_ref, mf_ref, cnt_ref, cst_ref,
                         cbuf, carry):
    b = pl.program_id(0)
    t = pl.program_id(1)

    @pl.when((b == 0) & (t == 0))
    def _():
        carry[...] = jnp.zeros_like(carry)

    @pl.when(t == 0)
    def _():
        cbuf[0:HIST, :] = jnp.zeros((HIST, D_MODEL), _F32)
        cbuf[HIST + TT:, :] = jnp.zeros((SUBLANES, D_MODEL), _F32)

    x = x_ref[...]
    hb = _rms(x, gmix_ref[...]).astype(_BF16)

    def proj(s):
        return _dot(hb, win_ref[:, s * D_MODEL:(s + 1) * D_MODEL])

    u = _gelu(proj(0))
    v = _ln(_gelu(proj(1)), lnvg_ref[...], lnvb_ref[...])
    ri = lax.broadcasted_iota(jnp.int32, (CHUNK, CHUNK), 0)
    ci = lax.broadcasted_iota(jnp.int32, (CHUNK, CHUNK), 1)
    s_rows = []
    for c in range(TT // CHUNK):
        s_cols = []
        for g in range(A_GROUPS):
            wm = jnp.where(ci <= ri, wsp_ref[g], 0.0).astype(_BF16)
            vb = v[c * CHUNK:(c + 1) * CHUNK, g * LANES:(g + 1) * LANES].astype(_BF16)
            s_cols.append(_dot(wm, vb))
        s_rows.append(jnp.concatenate(s_cols, axis=1) + bsp_ref[...])
    s = jnp.concatenate(s_rows, axis=0)
    y_a = _dot((u * s).astype(_BF16), wa_ref[...])

    glu = proj(2) * _sigmoid(proj(3))
    cbuf[HIST:HIST + TT, :] = glu
    off = HIST - (CONV_WIDTH - 1)
    span = HIST + TT
    conv_cols = []
    for c in range(D_MODEL // LANES):
        cols = slice(c * LANES, (c + 1) * LANES)
        acc = jnp.zeros((TT, LANES), _F32) + cb_ref[:, cols]
        for sh in range(SUBLANES):
            window = cbuf[0:span + SUBLANES, cols]
            shifted = window if sh == 0 else pltpu.roll(window, span + SUBLANES - sh, axis=0)
            for a in range(HIST // SUBLANES + 1):
                k = a * SUBLANES + sh - off
                if 0 <= k < CONV_WIDTH:
                    acc = acc + cw_ref[k:k + 1, cols] * shifted[a * SUBLANES:a * SUBLANES + TT, :]
        conv_cols.append(acc)
    conv = jnp.concatenate(conv_cols, axis=1)
    cbuf[0:HIST, :] = cbuf[TT:TT + HIST, :]
    cn = _ln(conv, lncg_ref[...], lncb_ref[...])
    y_b = _dot((cn * _sigmoid(cn)).astype(_BF16), wb_ref[...]) + bb_ref[...]

    @pl.when(t == pl.num_programs(1) - 1)
    def _():
        cst_ref[...] = glu[TT - (CONV_WIDTH - 1):, :]

    merged = (_sigmoid(proj(4) + bga_ref[...]) * y_a + _sigmoid(proj(5) + bgb_ref[...]) * y_b)
    x1 = x + _dot(merged.astype(_BF16), wout_ref[...])
    x1_ref[...] = x1

    h2 = _rms(x1, gffn_ref[...])
    _store_row_tiles(h2_ref, (), TT, h2)
    meta_i, meta_f = _route(h2, wr_ref, br_ref, carry)
    mi_ref[...] = meta_i
    mf_ref[...] = meta_f
    cnt_ref[...] = carry[...]


def _full(shape):
    return pl.BlockSpec(shape, lambda *_: (0,) * len(shape))


def _mixer_prompt(x, w):
    bsz, seq, d = x.shape
    n_tok = bsz * seq
    nt = seq // TT
    tok = lambda b, t: (b * nt + t, 0)
    weights = (w["gmix"], w["win"], w["lnvg"], w["lnvb"], w["wsp"], w["bsp"], w["wa"], w["cw"],
               w["cb"], w["lncg"], w["lncb"], w["wb"], w["bb"], w["bga"], w["bgb"], w["wout"],
               w["gffn"], w["wr"], w["br"])
    return pl.pallas_call(
        _mixer_prompt_kernel,
        grid=(bsz, nt),
        in_specs=[pl.BlockSpec((None, TT, d), lambda b, t: (b, t, 0))]
                 + [_full(a.shape) for a in weights],
        out_specs=[
            pl.BlockSpec((TT, d), tok),
            pl.BlockSpec((TT * ROW_TILES, LANES), tok),
            pl.BlockSpec((TT, LANES), tok),
            pl.BlockSpec((TT, LANES), tok),
            _full((SUBLANES, LANES)),
            pl.BlockSpec((None, CONV_WIDTH - 1, d), lambda b, t: (b, 0, 0)),
        ],
        out_shape=[
            jax.ShapeDtypeStruct((n_tok, d), _F32),
            jax.ShapeDtypeStruct((n_tok * ROW_TILES, LANES), _F32),
            jax.ShapeDtypeStruct((n_tok, LANES), jnp.int32),
            jax.ShapeDtypeStruct((n_tok, LANES), _F32),
            jax.ShapeDtypeStruct((SUBLANES, LANES), _F32),
            jax.ShapeDtypeStruct((bsz, CONV_WIDTH - 1, d), _F32),
        ],
        scratch_shapes=[pltpu.VMEM((HIST + TT + SUBLANES, d), _F32),
                        pltpu.VMEM((SUBLANES, LANES), _F32)],
        compiler_params=pltpu.CompilerParams(
            dimension_semantics=("arbitrary", "arbitrary"), vmem_limit_bytes=VMEM_LIMIT),
        name="mixer_prompt",
    )(x, *weights)


def _mixer_sample_kernel(x_ref, cache_ref, cnt0_ref, gmix_ref, win_ref, lnvg_ref, lnvb_ref,
                         wcs_ref, bcs_ref, wa_ref, cw_ref, cb_ref, lncg_ref, lncb_ref, wb_ref,
                         bb_ref, bga_ref, bgb_ref, wout_ref, gffn_ref, wr_ref, br_ref,
                         x1_ref, h2_ref, mi_ref, mf_ref, cnt_ref, glu_ref, v_ref,
                         carry):
    nt = x_ref.shape[0]
    n = nt * SB

    @pl.when(pl.program_id(0) == 0)
    def _():
        carry[...] = cnt0_ref[...]

    x = x_ref[...].reshape(n, D_MODEL)
    hb = _rms(x, gmix_ref[...]).astype(_BF16)

    def proj(s):
        return _dot(hb, win_ref[:, s * D_MODEL:(s + 1) * D_MODEL])

    u = _gelu(proj(0))
    v = _ln(_gelu(proj(1)), lnvg_ref[...], lnvb_ref[...])
    v_ref[...] = v.reshape(nt, SB, D_MODEL)
    s_rows = []
    for i in range(nt):
        acc = jnp.zeros((SB, D_MODEL), _F32) + bcs_ref[i:i + 1, :]
        for j in range(i + 1):
            acc = acc + wcs_ref[i * nt + j:i * nt + j + 1, :] * v[j * SB:(j + 1) * SB, :]
        s_rows.append(acc)
    s = jnp.concatenate(s_rows, axis=0)
    y_a = _dot((u * s).astype(_BF16), wa_ref[...])

    glu = proj(2) * _sigmoid(proj(3))
    glu_ref[...] = glu.reshape(nt, SB, D_MODEL)
    hist = CONV_WIDTH - 1
    conv_rows = []
    for i in range(nt):
        acc = jnp.zeros((SB, D_MODEL), _F32) + cb_ref[...]
        for k in range(CONV_WIDTH):
            p = i + k
            src = cache_ref[p] if p < hist else glu[(p - hist) * SB:(p - hist + 1) * SB, :]
            acc = acc + cw_ref[k:k + 1, :] * src
        conv_rows.append(acc)
    cn = _ln(jnp.concatenate(conv_rows, axis=0), lncg_ref[...], lncb_ref[...])
    y_b = _dot((cn * _sigmoid(cn)).astype(_BF16), wb_ref[...]) + bb_ref[...]

    merged = (_sigmoid(proj(4) + bga_ref[...]) * y_a + _sigmoid(proj(5) + bgb_ref[...]) * y_b)
    x1 = x + _dot(merged.astype(_BF16), wout_ref[...])
    x1_ref[...] = x1.reshape(nt, SB, D_MODEL)

    h2 = _rms(x1, gffn_ref[...])
    for i in range(nt):
        _store_row_tiles(h2_ref, (i,), SB, h2[i * SB:(i + 1) * SB, :])
    meta_i, meta_f = _route(h2, wr_ref, br_ref, carry)
    mi_ref[...] = meta_i.reshape(nt, SB, LANES)
    mf_ref[...] = meta_f.reshape(nt, SB, LANES)
    cnt_ref[...] = carry[...]


def _mixer_sample(x_t, cache_t, cnt0, w):
    nt, nseq, d = x_t.shape
    hist = cache_t.shape[0]
    weights = (w["gmix"], w["win"], w["lnvg"], w["lnvb"], w["wcs"], w["bcs"], w["wa"], w["cw"],
               w["cb"], w["lncg"], w["lncb"], w["wb"], w["bb"], w["bga"], w["bgb"], w["wout"],
               w["gffn"], w["wr"], w["br"])
    seqs = lambda i: (0, i, 0)
    return pl.pallas_call(
        _mixer_sample_kernel,
        grid=(nseq // SB,),
        in_specs=[pl.BlockSpec((nt, SB, d), seqs), pl.BlockSpec((hist, SB, d), seqs),
                  _full(cnt0.shape)] + [_full(a.shape) for a in weights],
        out_specs=[
            pl.BlockSpec((nt, SB, d), seqs),
            pl.BlockSpec((nt, SB * ROW_TILES, LANES), seqs),
            pl.BlockSpec((nt, SB, LANES), seqs),
            pl.BlockSpec((nt, SB, LANES), seqs),
            _full((SUBLANES, LANES)),
            pl.BlockSpec((nt, SB, d), seqs),
            pl.BlockSpec((nt, SB, d), seqs),
        ],
        out_shape=[
            jax.ShapeDtypeStruct((nt, nseq, d), _F32),
            jax.ShapeDtypeStruct((nt, nseq * ROW_TILES, LANES), _F32),
            jax.ShapeDtypeStruct((nt, nseq, LANES), jnp.int32),
            jax.ShapeDtypeStruct((nt, nseq, LANES), _F32),
            jax.ShapeDtypeStruct((SUBLANES, LANES), _F32),
            jax.ShapeDtypeStruct((nt, nseq, d), _F32),
            jax.ShapeDtypeStruct((nt, nseq, d), _F32),
        ],
        scratch_shapes=[pltpu.VMEM((SUBLANES, LANES), _F32)],
        compiler_params=pltpu.CompilerParams(
            dimension_semantics=("arbitrary",), vmem_limit_bytes=VMEM_LIMIT),
        name="mixer_sample",
    )(x_t, cache_t, cnt0, *weights)


def _row_tile(ref, row):
    return ref.at[pl.ds(pl.multiple_of(row * ROW_TILES, ROW_TILES), ROW_TILES), :]


def _dispatch_kernel(dest_ref, h2_ref, xs_in_hbm, xs_hbm, sem):
    del xs_in_hbm

    def issue(t, carry):
        src = _row_tile(h2_ref, t)
        for k in range(TOP_K):
            pltpu.make_async_copy(src, _row_tile(xs_hbm, dest_ref[0, 0, t * TOP_K + k]), sem).start()
        return carry

    lax.fori_loop(0, TT, issue, 0)
    for _ in range(TOP_K):
        pltpu.make_async_copy(h2_ref, xs_hbm.at[pl.ds(0, TT * ROW_TILES), :], sem).wait()


def _dispatch(dest, h2, xs):
    n_tiles = dest.shape[0]
    return pl.pallas_call(
        _dispatch_kernel,
        grid=(n_tiles,),
        in_specs=[
            pl.BlockSpec((1, 1, TT * TOP_K), lambda i: (i, 0, 0), memory_space=pltpu.SMEM),
            pl.BlockSpec((TT * ROW_TILES, LANES), lambda i: (i, 0)),
            pl.BlockSpec(memory_space=pl.ANY),
        ],
        out_specs=pl.BlockSpec(memory_space=pl.ANY),
        out_shape=jax.ShapeDtypeStruct(xs.shape, xs.dtype),
        scratch_shapes=[pltpu.SemaphoreType.DMA],
        input_output_aliases={2: 0},
        compiler_params=pltpu.CompilerParams(
            dimension_semantics=("arbitrary",), has_side_effects=True),
        name="dispatch",
    )(dest, h2, xs)


def _experts_kernel(be_ref, nused_ref, xs_ref, wup_ref, bup_ref, wdn_ref, bdn_ref, out_ref,
                    wup_bf, wdn_bf):
    i = pl.program_id(0)
    prev = be_ref[jnp.maximum(i - 1, 0)]

    @pl.when((i == 0) | (be_ref[i] != prev))
    def _():
        rows = 128

        def cast(r, carry):
            sl = pl.ds(pl.multiple_of(r * rows, rows), rows)
            wup_bf[sl, :] = wup_ref[sl, :].astype(_BF16)
            wdn_bf[sl, :] = wdn_ref[sl, :].astype(_BF16)
            return carry

        lax.fori_loop(0, D_MODEL // rows, cast, 0)

    @pl.when(i < nused_ref[0])
    def _():
        xb = _load_row_tiles(xs_ref, 0, BLK).astype(_BF16)
        gate = _dot(xb, wup_bf[:, :D_EXPERT]) + bup_ref[:, :D_EXPERT]
        up = _dot(xb, wup_bf[:, D_EXPERT:]) + bup_ref[:, D_EXPERT:]
        gate = jnp.minimum(gate, SWIGLU_LIMIT)
        up = jnp.clip(up, -SWIGLU_LIMIT, SWIGLU_LIMIT)
        act = (up + 1.0) * (gate * _sigmoid(SWIGLU_ALPHA * gate))
        o = _dot(act.astype(_BF16), wdn_bf[...]) + bdn_ref[...]
        _store_row_tiles(out_ref, (), BLK, o)

    @pl.when(i >= nused_ref[0])
    def _():
        out_ref[...] = jnp.zeros_like(out_ref)


def _experts(block_expert, nused, xs, w_up, b_up, w_down, b_down):
    n_blk = block_expert.shape[0]
    rows = lambda i, be, nu: (jnp.minimum(i, nu[0] - 1), 0)
    expert = lambda i, be, nu: (be[i], 0, 0)
    d, f2 = w_up.shape[1], w_up.shape[2]
    grid_spec = pltpu.PrefetchScalarGridSpec(
        num_scalar_prefetch=2,
        grid=(n_blk,),
        in_specs=[
            pl.BlockSpec((BLK * ROW_TILES, LANES), rows),
            pl.BlockSpec((None, d, f2), expert),
            pl.BlockSpec((None, 1, f2), expert),
            pl.BlockSpec((None, w_down.shape[1], d), expert),
            pl.BlockSpec((None, 1, d), expert),
        ],
        out_specs=pl.BlockSpec((BLK * ROW_TILES, LANES), lambda i, be, nu: (i, 0)),
        scratch_shapes=[pltpu.VMEM((d, f2), _BF16), pltpu.VMEM((w_down.shape[1], d), _BF16)],
    )
    return pl.pallas_call(
        _experts_kernel,
        grid_spec=grid_spec,
        out_shape=jax.ShapeDtypeStruct(xs.shape, _F32),
        compiler_params=pltpu.CompilerParams(
            dimension_semantics=("arbitrary",), vmem_limit_bytes=VMEM_LIMIT),
        name="experts",
    )(block_expert, nused, xs, w_up, b_up, w_down, b_down)


def _combine_kernel(dest_ref, x1_ref, mf_ref, gfin_ref, outs_hbm, y_ref, buf, sem):
    def issue(t, carry):
        for k in range(TOP_K):
            dst = buf.at[pl.ds(pl.multiple_of((k * TT + t) * ROW_TILES, ROW_TILES), ROW_TILES), :]
            pltpu.make_async_copy(_row_tile(outs_hbm, dest_ref[0, 0, t * TOP_K + k]), dst, sem).start()
        return carry

    lax.fori_loop(0, TT, issue, 0)
    pltpu.make_async_copy(outs_hbm.at[pl.ds(0, buf.shape[0]), :], buf, sem).wait()

    y = x1_ref[...]
    gates = mf_ref[...]
    for k in range(TOP_K):
        y = y + gates[:, k:k + 1] * _load_row_tiles(buf, k * TT * ROW_TILES, TT)
    y_ref[...] = _rms(y, gfin_ref[...])


def _combine(dest, x1, meta_f, g_final, outs):
    n_tok, d = x1.shape
    tok = lambda i: (i, 0)
    return pl.pallas_call(
        _combine_kernel,
        grid=(n_tok // TT,),
        in_specs=[
            pl.BlockSpec((1, 1, TT * TOP_K), lambda i: (i, 0, 0), memory_space=pltpu.SMEM),
            pl.BlockSpec((TT, d), tok),
            pl.BlockSpec((TT, LANES), tok),
            _full(g_final.shape),
            pl.BlockSpec(memory_space=pl.ANY),
        ],
        out_specs=pl.BlockSpec((TT, d), tok),
        out_shape=jax.ShapeDtypeStruct((n_tok, d), _F32),
        scratch_shapes=[pltpu.VMEM((TOP_K * TT * ROW_TILES, LANES), _F32), pltpu.SemaphoreType.DMA],
        compiler_params=pltpu.CompilerParams(
            dimension_semantics=("arbitrary",), vmem_limit_bytes=VMEM_LIMIT),
        name="combine",
    )(dest, x1, meta_f, g_final, outs)


def _prep_weights(norm_mix_g, w_in, ln_v_g, ln_v_b, w_spatial, b_spatial, w_branch_a, conv_w,
                  conv_b, ln_conv_g, ln_conv_b, w_branch_b, b_branch_b, b_gate_a, b_gate_b, w_out,
                  norm_ffn_g, w_router, b_router, n_new):
    row = lambda a: a.reshape(1, -1)
    pad = LANES - N_EXPERTS
    wr = jnp.pad(w_router, ((0, 0), (0, pad)))
    wr_hi = wr.astype(_BF16)
    wr_lo = (wr - wr_hi.astype(_F32)).astype(_BF16)
    wcs = jnp.repeat(w_spatial[:, :n_new, :n_new].transpose(1, 2, 0).reshape(n_new * n_new, A_GROUPS),
                     LANES, axis=1)
    return dict(
        gmix=row(norm_mix_g), win=w_in.astype(_BF16), lnvg=row(ln_v_g), lnvb=row(ln_v_b),
        wsp=w_spatial, bsp=jnp.repeat(b_spatial.T, LANES, axis=1), wcs=wcs,
        bcs=jnp.repeat(b_spatial[:, :n_new].T, LANES, axis=1),
        wa=w_branch_a.astype(_BF16), cw=conv_w, cb=row(conv_b), lncg=row(ln_conv_g),
        lncb=row(ln_conv_b), wb=w_branch_b.astype(_BF16), bb=row(b_branch_b), bga=row(b_gate_a),
        bgb=row(b_gate_b), wout=w_out.astype(_BF16), gffn=row(norm_ffn_g),
        wr=jnp.concatenate([wr_hi, wr_lo], axis=1), br=jnp.pad(row(b_router), ((0, 0), (0, pad))))


def kernel(x_prompt, x_sample, cache_conv, norm_mix_g, w_in, ln_v_g, ln_v_b, w_spatial, b_spatial, w_branch_a, conv_w, conv_b, ln_conv_g, ln_conv_b, w_branch_b, b_branch_b, b_gate_a, b_gate_b, w_out, norm_ffn_g, w_router, b_router, w_up, b_up, w_down, b_down, norm_final_g):
    assert w_in.shape[0] == 1, "single trunk layer"
    bsz, seq, d = x_prompt.shape
    nseq, n_new, _ = x_sample.shape
    assert d == D_MODEL and seq % TT == 0 and TT % CHUNK == 0 and nseq % SB == 0
    assert n_new <= CHUNK and (nseq * n_new) % TT == 0
    w = _prep_weights(norm_mix_g[0], w_in[0], ln_v_g[0], ln_v_b[0], w_spatial[0], b_spatial[0],
                      w_branch_a[0], conv_w[0], conv_b[0], ln_conv_g[0], ln_conv_b[0],
                      w_branch_b[0], b_branch_b[0], b_gate_a[0], b_gate_b[0], w_out[0],
                      norm_ffn_g[0], w_router[0], b_router[0], n_new)

    x1_p, h2_p, mi_p, mf_p, cnt_p, cst_p = _mixer_prompt(x_prompt, w)
    x_t = x_sample.transpose(1, 0, 2)
    cache_t = cache_conv[0].transpose(1, 0, 2)
    x1_s, h2_s, mi_s, mf_s, cnt, glu_s, v_s = _mixer_sample(x_t, cache_t, cnt_p, w)
    n_p, n_s = bsz * seq, nseq * n_new
    x1_s = x1_s.reshape(n_s, d)
    h2_s = h2_s.reshape(n_s * ROW_TILES, LANES)
    mi_s = mi_s.reshape(n_s, LANES)
    mf_s = mf_s.reshape(n_s, LANES)

    counts = cnt[0, :N_EXPERTS].astype(jnp.int32)
    padded = ((counts + BLK - 1) // BLK) * BLK
    pend = jnp.cumsum(padded)
    pstart = pend - padded
    n_assign = (n_p + n_s) * TOP_K
    n_blk = -(-(n_assign + N_EXPERTS * (BLK - 1)) // BLK)
    nused = (pend[-1] // BLK).astype(jnp.int32)
    blk_ids = jnp.arange(n_blk, dtype=jnp.int32)
    first_row = jnp.minimum(blk_ids, nused - 1) * BLK
    be = jnp.sum((pend[None, :] <= first_row[:, None]).astype(jnp.int32), axis=1)
    be = jnp.minimum(be, N_EXPERTS - 1)
    experts_row = jnp.arange(N_EXPERTS, dtype=jnp.int32)

    def dest_of(mi):
        sel = mi[:, :TOP_K, None] == experts_row[None, None, :]
        dest = jnp.sum(jnp.where(sel, pstart[None, None, :], 0), axis=-1) + mi[:, TOP_K:2 * TOP_K]
        return dest.reshape(-1, 1, TT * TOP_K)

    dest_p, dest_s = dest_of(mi_p), dest_of(mi_s)
    xs = jnp.zeros((n_blk * BLK * ROW_TILES, LANES), _F32)
    xs = _dispatch(dest_p, h2_p, xs)
    xs = _dispatch(dest_s, h2_s, xs)
    outs = _experts(be, nused.reshape(1), xs, w_up[0], b_up[0][:, None, :], w_down[0],
                    b_down[0][:, None, :])
    g_final = norm_final_g.reshape(1, d)
    y_p = _combine(dest_p, x1_p, mf_p, g_final, outs)
    y_s = _combine(dest_s, x1_s, mf_s, g_final, outs)

    y_prompt = y_p.reshape(bsz, seq, d)
    y_sample = y_s.reshape(n_new, nseq, d).transpose(1, 0, 2)
    conv_state_prompt = cst_p[None]
    glu_new = glu_s.transpose(1, 0, 2)
    conv_state_sample = jnp.concatenate([cache_conv[0][:, n_new:], glu_new], axis=1)[None]
    chunk_v_sample = v_s.transpose(1, 0, 2)[None]
    return (y_prompt, y_sample, conv_state_prompt, conv_state_sample, chunk_v_sample)
```

```python
import functools

import jax
import jax.numpy as jnp
from jax import lax
from jax.experimental import pallas as pl
from jax.experimental.pallas import tpu as pltpu

D_MODEL = 1024
CHUNK = 128
A_GROUPS = 8
CONV_WIDTH = 31
HIST = 32
N_EXPERTS = 32
TOP_K = 4
D_EXPERT = 1024
SWIGLU_LIMIT = 7.0
SWIGLU_ALPHA = 1.702
EPS = 1e-5
LANES = 128
SUBLANES = 8
ROW_TILES = D_MODEL // LANES
TT = 256
SB = 32
BLK = 512
VMEM_LIMIT = 56 * 1024 * 1024

_F32 = jnp.float32
_BF16 = jnp.bfloat16
_INV_SQRT2 = 0.7071067811865476


def _rms(x, g):
    return x * lax.rsqrt(jnp.mean(x * x, axis=-1, keepdims=True) + EPS) * g


def _ln(x, g, b):
    mu = jnp.mean(x, axis=-1, keepdims=True)
    xc = x - mu
    var = jnp.mean(xc * xc, axis=-1, keepdims=True)
    return xc * lax.rsqrt(var + EPS) * g + b


def _gelu(x):
    return 0.5 * x * (1.0 + lax.erf(x * _INV_SQRT2))


def _sigmoid(x):
    return 1.0 / (1.0 + jnp.exp(-x))


def _dot(a, b):
    return jnp.dot(a, b, preferred_element_type=_F32)


def _store_row_tiles(ref, lead, n, val):
    for j in range(ROW_TILES):
        ref[lead + (pl.ds(j, n, stride=ROW_TILES), slice(None))] = val[:, j * LANES:(j + 1) * LANES]


def _load_row_tiles(ref, start, n):
    return jnp.concatenate(
        [ref[pl.ds(start + j, n, stride=ROW_TILES), :] for j in range(ROW_TILES)], axis=1)


def _route(h2, wr_ref, br_ref, carry_ref):
    n = h2.shape[0]
    h_hi = h2.astype(_BF16)
    h_lo = (h2 - h_hi.astype(_F32)).astype(_BF16)
    by_hi = _dot(h_hi, wr_ref[...])
    logits = (by_hi[:, :LANES] + by_hi[:, LANES:] + _dot(h_lo, wr_ref[:, :LANES])) + br_ref[...]
    lane = lax.broadcasted_iota(jnp.int32, (n, LANES), 1)
    l = jnp.where(lane < N_EXPERTS, logits, -jnp.inf)
    onehots, vals, idxs = [], [], []
    for _ in range(TOP_K):
        m = jnp.max(l, axis=-1, keepdims=True)
        idx = jnp.min(jnp.where(l == m, lane, LANES), axis=-1, keepdims=True)
        oh = lane == idx
        onehots.append(oh)
        vals.append(m)
        idxs.append(idx)
        l = jnp.where(oh, -jnp.inf, l)
    exps = [jnp.exp(v - vals[0]) for v in vals]
    denom = exps[0] + exps[1] + exps[2] + exps[3]
    gates = [e / denom for e in exps]

    oh_all = jnp.zeros((n, LANES), _F32)
    for oh in onehots:
        oh_all = oh_all + jnp.where(oh, 1.0, 0.0)
    ri = lax.broadcasted_iota(jnp.int32, (n, n), 0)
    ci = lax.broadcasted_iota(jnp.int32, (n, n), 1)
    tri = jnp.where(ci < ri, 1.0, 0.0).astype(_BF16)
    base = _dot(tri, oh_all.astype(_BF16)) + carry_ref[0:1, :]
    ranks = [jnp.sum(jnp.where(oh, base, 0.0), axis=-1, keepdims=True) for oh in onehots]
    carry_ref[...] = carry_ref[...] + jnp.sum(oh_all, axis=0, keepdims=True)

    meta_i = jnp.zeros((n, LANES), jnp.int32)
    meta_f = jnp.zeros((n, LANES), _F32)
    for k in range(TOP_K):
        meta_i = jnp.where(lane == k, idxs[k], meta_i)
        meta_i = jnp.where(lane == TOP_K + k, ranks[k].astype(jnp.int32), meta_i)
        meta_f = jnp.where(lane == k, gates[k], meta_f)
    return meta_i, meta_f


def _mixer_prompt_kernel(x_ref, gmix_ref, win_ref, lnvg_ref, lnvb_ref, wsp_ref, bsp_ref, wa_ref,
                         cw_ref, cb_ref, lncg_ref, lncb_ref, wb_ref, bb_ref, bga_ref, bgb_ref,
                         wout_ref, gffn_ref, wr_ref, br_ref,
                         x1_ref, h2_ref, mi_ref, mf_ref, cnt_ref, cst_ref,
                         cbuf, carry):
    b = pl.program_id(0)
    t = pl.program_id(1)

    @pl.when((b == 0) & (t == 0))
    def _():
        carry[...] = jnp.zeros_like(carry)

    @pl.when(t == 0)
    def _():
        cbuf[0:HIST, :] = jnp.zeros((HIST, D_MODEL), _F32)
        cbuf[HIST + TT:, :] = jnp.zeros((SUBLANES, D_MODEL), _F32)

    x = x_ref[...]
    hb = _rms(x, gmix_ref[...]).astype(_BF16)

    def proj(s):
        return _dot(hb, win_ref[:, s * D_MODEL:(s + 1) * D_MODEL])

    u = _gelu(proj(0))
    v = _ln(_gelu(proj(1)), lnvg_ref[...], lnvb_ref[...])
    ri = lax.broadcasted_iota(jnp.int32, (CHUNK, CHUNK), 0)
    ci = lax.broadcasted_iota(jnp.int32, (CHUNK, CHUNK), 1)
    s_rows = []
    for c in range(TT // CHUNK):
        s_cols = []
        for g in range(A_GROUPS):
            wm = jnp.where(ci <= ri, wsp_ref[g], 0.0).astype(_BF16)
            vb = v[c * CHUNK:(c + 1) * CHUNK, g * LANES:(g + 1) * LANES].astype(_BF16)
            s_cols.append(_dot(wm, vb))
        s_rows.append(jnp.concatenate(s_cols, axis=1) + bsp_ref[...])
    s = jnp.concatenate(s_rows, axis=0)
    y_a = _dot((u * s).astype(_BF16), wa_ref[...])

    glu = proj(2) * _sigmoid(proj(3))
    cbuf[HIST:HIST + TT, :] = glu
    off = HIST - (CONV_WIDTH - 1)
    span = HIST + TT
    conv_cols = []
    for c in range(D_MODEL // LANES):
        cols = slice(c * LANES, (c + 1) * LANES)
        acc = jnp.zeros((TT, LANES), _F32) + cb_ref[:, cols]
        for sh in range(SUBLANES):
            window = cbuf[0:span + SUBLANES, cols]
            shifted = window if sh == 0 else pltpu.roll(window, span + SUBLANES - sh, axis=0)
            for a in range(HIST // SUBLANES + 1):
                k = a * SUBLANES + sh - off
                if 0 <= k < CONV_WIDTH:
                    acc = acc + cw_ref[k:k + 1, cols] * shifted[a * SUBLANES:a * SUBLANES + TT, :]
        conv_cols.append(acc)
    conv = jnp.concatenate(conv_cols, axis=1)
    cbuf[0:HIST, :] = cbuf[TT:TT + HIST, :]
    cn = _ln(conv, lncg_ref[...], lncb_ref[...])
    y_b = _dot((cn * _sigmoid(cn)).astype(_BF16), wb_ref[...]) + bb_ref[...]

    @pl.when(t == pl.num_programs(1) - 1)
    def _():
        cst_ref[...] = glu[TT - (CONV_WIDTH - 1):, :]

    merged = (_sigmoid(proj(4) + bga_ref[...]) * y_a + _sigmoid(proj(5) + bgb_ref[...]) * y_b)
    x1 = x + _dot(merged.astype(_BF16), wout_ref[...])
    x1_ref[...] = x1

    h2 = _rms(x1, gffn_ref[...])
    _store_row_tiles(h2_ref, (), TT, h2)
    meta_i, meta_f = _route(h2, wr_ref, br_ref, carry)
    mi_ref[...] = meta_i
    mf_ref[...] = meta_f
    cnt_ref[...] = carry[...]


def _full(shape):
    return pl.BlockSpec(shape, lambda *_: (0,) * len(shape))


def _mixer_prompt(x, w):
    bsz, seq, d = x.shape
    n_tok = bsz * seq
    nt = seq // TT
    tok = lambda b, t: (b * nt + t, 0)
    weights = (w["gmix"], w["win"], w["lnvg"], w["lnvb"], w["wsp"], w["bsp"], w["wa"], w["cw"],
               w["cb"], w["lncg"], w["lncb"], w["wb"], w["bb"], w["bga"], w["bgb"], w["wout"],
               w["gffn"], w["wr"], w["br"])
    return pl.pallas_call(
        _mixer_prompt_kernel,
        grid=(bsz, nt),
        in_specs=[pl.BlockSpec((None, TT, d), lambda b, t: (b, t, 0))]
                 + [_full(a.shape) for a in weights],
        out_specs=[
            pl.BlockSpec((TT, d), tok),
            pl.BlockSpec((TT * ROW_TILES, LANES), tok),
            pl.BlockSpec((TT, LANES), tok),
            pl.BlockSpec((TT, LANES), tok),
            _full((SUBLANES, LANES)),
            pl.BlockSpec((None, CONV_WIDTH - 1, d), lambda b, t: (b, 0, 0)),
        ],
        out_shape=[
            jax.ShapeDtypeStruct((n_tok, d), _F32),
            jax.ShapeDtypeStruct((n_tok * ROW_TILES, LANES), _F32),
            jax.ShapeDtypeStruct((n_tok, LANES), jnp.int32),
            jax.ShapeDtypeStruct((n_tok, LANES), _F32),
            jax.ShapeDtypeStruct((SUBLANES, LANES), _F32),
            jax.ShapeDtypeStruct((bsz, CONV_WIDTH - 1, d), _F32),
        ],
        scratch_shapes=[pltpu.VMEM((HIST + TT + SUBLANES, d), _F32),
                        pltpu.VMEM((SUBLANES, LANES), _F32)],
        compiler_params=pltpu.CompilerParams(
            dimension_semantics=("arbitrary", "arbitrary"), vmem_limit_bytes=VMEM_LIMIT),
        name="mixer_prompt",
    )(x, *weights)


def _mixer_sample_kernel(x_ref, cache_ref, cnt0_ref, gmix_ref, win_ref, lnvg_ref, lnvb_ref,
                         wcs_ref, bcs_ref, wa_ref, cw_ref, cb_ref, lncg_ref, lncb_ref, wb_ref,
                         bb_ref, bga_ref, bgb_ref, wout_ref, gffn_ref, wr_ref, br_ref,
                         x1_ref, h2_ref, mi_ref, mf_ref, cnt_ref, glu_ref, v_ref,
                         carry):
    nt = x_ref.shape[0]
    n = nt * SB

    @pl.when(pl.program_id(0) == 0)
    def _():
        carry[...] = cnt0_ref[...]

    x = x_ref[...].reshape(n, D_MODEL)
    hb = _rms(x, gmix_ref[...]).astype(_BF16)

    def proj(s):
        return _dot(hb, win_ref[:, s * D_MODEL:(s + 1) * D_MODEL])

    u = _gelu(proj(0))
    v = _ln(_gelu(proj(1)), lnvg_ref[...], lnvb_ref[...])
    v_ref[...] = v.reshape(nt, SB, D_MODEL)
    s_rows = []
    for i in range(nt):
        acc = jnp.zeros((SB, D_MODEL), _F32) + bcs_ref[i:i + 1, :]
        for j in range(i + 1):
            acc = acc + wcs_ref[i * nt + j:i * nt + j + 1, :] * v[j * SB:(j + 1) * SB, :]
        s_rows.append(acc)
    s = jnp.concatenate(s_rows, axis=0)
    y_a = _dot((u * s).astype(_BF16), wa_ref[...])

    glu = proj(2) * _sigmoid(proj(3))
    glu_ref[...] = glu.reshape(nt, SB, D_MODEL)
    hist = CONV_WIDTH - 1
    conv_rows = []
    for i in range(nt):
        acc = jnp.zeros((SB, D_MODEL), _F32) + cb_ref[...]
        for k in range(CONV_WIDTH):
            p = i + k
            src = cache_ref[p] if p < hist else glu[(p - hist) * SB:(p - hist + 1) * SB, :]
            acc = acc + cw_ref[k:k + 1, :] * src
        conv_rows.append(acc)
    cn = _ln(jnp.concatenate(conv_rows, axis=0), lncg_ref[...], lncb_ref[...])
    y_b = _dot((cn * _sigmoid(cn)).astype(_BF16), wb_ref[...]) + bb_ref[...]

    merged = (_sigmoid(proj(4) + bga_ref[...]) * y_a + _sigmoid(proj(5) + bgb_ref[...]) * y_b)
    x1 = x + _dot(merged.astype(_BF16), wout_ref[...])
    x1_ref[...] = x1.reshape(nt, SB, D_MODEL)

    h2 = _rms(x1, gffn_ref[...])
    for i in range(nt):
        _store_row_tiles(h2_ref, (i,), SB, h2[i * SB:(i + 1) * SB, :])
    meta_i, meta_f = _route(h2, wr_ref, br_ref, carry)
    mi_ref[...] = meta_i.reshape(nt, SB, LANES)
    mf_ref[...] = meta_f.reshape(nt, SB, LANES)
    cnt_ref[...] = carry[...]


def _mixer_sample(x_t, cache_t, cnt0, w):
    nt, nseq, d = x_t.shape
    hist = cache_t.shape[0]
    weights = (w["gmix"], w["win"], w["lnvg"], w["lnvb"], w["wcs"], w["bcs"], w["wa"], w["cw"],
               w["cb"], w["lncg"], w["lncb"], w["wb"], w["bb"], w["bga"], w["bgb"], w["wout"],
               w["gffn"], w["wr"], w["br"])
    seqs = lambda i: (0, i, 0)
    return pl.pallas_call(
        _mixer_sample_kernel,
        grid=(nseq // SB,),
        in_specs=[pl.BlockSpec((nt, SB, d), seqs), pl.BlockSpec((hist, SB, d), seqs),
                  _full(cnt0.shape)] + [_full(a.shape) for a in weights],
        out_specs=[
            pl.BlockSpec((nt, SB, d), seqs),
            pl.BlockSpec((nt, SB * ROW_TILES, LANES), seqs),
            pl.BlockSpec((nt, SB, LANES), seqs),
            pl.BlockSpec((nt, SB, LANES), seqs),
            _full((SUBLANES, LANES)),
            pl.BlockSpec((nt, SB, d), seqs),
            pl.BlockSpec((nt, SB, d), seqs),
        ],
        out_shape=[
            jax.ShapeDtypeStruct((nt, nseq, d), _F32),
            jax.ShapeDtypeStruct((nt, nseq * ROW_TILES, LANES), _F32),
            jax.ShapeDtypeStruct((nt, nseq, LANES), jnp.int32),
            jax.ShapeDtypeStruct((nt, nseq, LANES), _F32),
            jax.ShapeDtypeStruct((SUBLANES, LANES), _F32),
            jax.ShapeDtypeStruct((nt, nseq, d), _F32),
            jax.ShapeDtypeStruct((nt, nseq, d), _F32),
        ],
        scratch_shapes=[pltpu.VMEM((SUBLANES, LANES), _F32)],
        compiler_params=pltpu.CompilerParams(
            dimension_semantics=("arbitrary",), vmem_limit_bytes=VMEM_LIMIT),
        name="mixer_sample",
    )(x_t, cache_t, cnt0, *weights)


def _row_tile(ref, row):
    return ref.at[pl.ds(pl.multiple_of(row * ROW_TILES, ROW_TILES), ROW_TILES), :]


def _dispatch_kernel(dest_ref, h2_ref, xs_in_hbm, xs_hbm, sem):
    del xs_in_hbm

    def issue(t, carry):
        src = _row_tile(h2_ref, t)
        for k in range(TOP_K):
            pltpu.make_async_copy(src, _row_tile(xs_hbm, dest_ref[0, 0, t * TOP_K + k]),
                                  sem).start(priority=k % 2)
        return carry

    lax.fori_loop(0, TT, issue, 0)
    for _ in range(TOP_K):
        pltpu.make_async_copy(h2_ref, xs_hbm.at[pl.ds(0, TT * ROW_TILES), :], sem).wait()


def _dispatch(dest, h2, xs):
    n_tiles = dest.shape[0]
    return pl.pallas_call(
        _dispatch_kernel,
        grid=(n_tiles,),
        in_specs=[
            pl.BlockSpec((1, 1, TT * TOP_K), lambda i: (i, 0, 0), memory_space=pltpu.SMEM),
            pl.BlockSpec((TT * ROW_TILES, LANES), lambda i: (i, 0)),
            pl.BlockSpec(memory_space=pl.ANY),
        ],
        out_specs=pl.BlockSpec(memory_space=pl.ANY),
        out_shape=jax.ShapeDtypeStruct(xs.shape, xs.dtype),
        scratch_shapes=[pltpu.SemaphoreType.DMA],
        input_output_aliases={2: 0},
        compiler_params=pltpu.CompilerParams(
            dimension_semantics=("arbitrary",), has_side_effects=True),
        name="dispatch",
    )(dest, h2, xs)


def _experts_kernel(be_ref, nused_ref, xs_ref, wup_ref, bup_ref, wdn_ref, bdn_ref, out_ref,
                    wup_bf, wdn_bf):
    i = pl.program_id(0)
    prev = be_ref[jnp.maximum(i - 1, 0)]

    @pl.when((i == 0) | (be_ref[i] != prev))
    def _():
        rows = 128

        def cast(r, carry):
            sl = pl.ds(pl.multiple_of(r * rows, rows), rows)
            wup_bf[sl, :] = wup_ref[sl, :].astype(_BF16)
            wdn_bf[sl, :] = wdn_ref[sl, :].astype(_BF16)
            return carry

        lax.fori_loop(0, D_MODEL // rows, cast, 0)

    @pl.when(i < nused_ref[0])
    def _():
        xb = _load_row_tiles(xs_ref, 0, BLK).astype(_BF16)
        gate = _dot(xb, wup_bf[:, :D_EXPERT]) + bup_ref[:, :D_EXPERT]
        up = _dot(xb, wup_bf[:, D_EXPERT:]) + bup_ref[:, D_EXPERT:]
        gate = jnp.minimum(gate, SWIGLU_LIMIT)
        up = jnp.clip(up, -SWIGLU_LIMIT, SWIGLU_LIMIT)
        act = (up + 1.0) * (gate * _sigmoid(SWIGLU_ALPHA * gate))
        o = _dot(act.astype(_BF16), wdn_bf[...]) + bdn_ref[...]
        _store_row_tiles(out_ref, (), BLK, o)

    @pl.when(i >= nused_ref[0])
    def _():
        out_ref[...] = jnp.zeros_like(out_ref)


def _experts(block_expert, nused, xs, w_up, b_up, w_down, b_down):
    n_blk = block_expert.shape[0]
    rows = lambda i, be, nu: (jnp.minimum(i, nu[0] - 1), 0)
    expert = lambda i, be, nu: (be[i], 0, 0)
    d, f2 = w_up.shape[1], w_up.shape[2]
    grid_spec = pltpu.PrefetchScalarGridSpec(
        num_scalar_prefetch=2,
        grid=(n_blk,),
        in_specs=[
            pl.BlockSpec((BLK * ROW_TILES, LANES), rows),
            pl.BlockSpec((None, d, f2), expert),
            pl.BlockSpec((None, 1, f2), expert),
            pl.BlockSpec((None, w_down.shape[1], d), expert),
            pl.BlockSpec((None, 1, d), expert),
        ],
        out_specs=pl.BlockSpec((BLK * ROW_TILES, LANES), lambda i, be, nu: (i, 0)),
        scratch_shapes=[pltpu.VMEM((d, f2), _BF16), pltpu.VMEM((w_down.shape[1], d), _BF16)],
    )
    return pl.pallas_call(
        _experts_kernel,
        grid_spec=grid_spec,
        out_shape=jax.ShapeDtypeStruct(xs.shape, _F32),
        compiler_params=pltpu.CompilerParams(
            dimension_semantics=("arbitrary",), vmem_limit_bytes=VMEM_LIMIT),
        name="experts",
    )(block_expert, nused, xs, w_up, b_up, w_down, b_down)


def _combine_kernel(n, dest_ref, dest_next_ref, x1_ref, mf_ref, gfin_ref, outs_hbm, y_ref, buf,
                    sems):
    i = pl.program_id(0)

    def gather(d_ref, slot):
        def issue(t, carry):
            for k in range(TOP_K):
                dst = buf.at[slot, pl.ds(pl.multiple_of((k * TT + t) * ROW_TILES, ROW_TILES),
                                         ROW_TILES), :]
                pltpu.make_async_copy(_row_tile(outs_hbm, d_ref[0, 0, t * TOP_K + k]), dst,
                                      sems.at[slot]).start(priority=k % 2)
            return carry

        lax.fori_loop(0, TT, issue, 0)

    slot = i % 2

    @pl.when(i == 0)
    def _():
        gather(dest_ref, 0)

    @pl.when(i + 1 < n)
    def _():
        gather(dest_next_ref, 1 - slot)

    pltpu.make_async_copy(outs_hbm.at[pl.ds(0, buf.shape[1]), :], buf.at[slot], sems.at[slot]).wait()

    y = x1_ref[...]
    gates = mf_ref[...]
    rows = buf.at[slot]
    for k in range(TOP_K):
        y = y + gates[:, k:k + 1] * _load_row_tiles(rows, k * TT * ROW_TILES, TT)
    y_ref[...] = _rms(y, gfin_ref[...])


def _combine(dest, x1, meta_f, g_final, outs):
    n_tok, d = x1.shape
    n_tiles = n_tok // TT
    tok = lambda i: (i, 0)
    return pl.pallas_call(
        functools.partial(_combine_kernel, n_tiles),
        grid=(n_tiles,),
        in_specs=[
            pl.BlockSpec((1, 1, TT * TOP_K), lambda i: (i, 0, 0), memory_space=pltpu.SMEM),
            pl.BlockSpec((1, 1, TT * TOP_K), lambda i: (jnp.minimum(i + 1, n_tiles - 1), 0, 0),
                         memory_space=pltpu.SMEM),
            pl.BlockSpec((TT, d), tok),
            pl.BlockSpec((TT, LANES), tok),
            _full(g_final.shape),
            pl.BlockSpec(memory_space=pl.ANY),
        ],
        out_specs=pl.BlockSpec((TT, d), tok),
        out_shape=jax.ShapeDtypeStruct((n_tok, d), _F32),
        scratch_shapes=[pltpu.VMEM((2, TOP_K * TT * ROW_TILES, LANES), _F32),
                        pltpu.SemaphoreType.DMA((2,))],
        compiler_params=pltpu.CompilerParams(
            dimension_semantics=("arbitrary",), vmem_limit_bytes=VMEM_LIMIT),
        name="combine",
    )(dest, dest, x1, meta_f, g_final, outs)


def _prep_weights(norm_mix_g, w_in, ln_v_g, ln_v_b, w_spatial, b_spatial, w_branch_a, conv_w,
                  conv_b, ln_conv_g, ln_conv_b, w_branch_b, b_branch_b, b_gate_a, b_gate_b, w_out,
                  norm_ffn_g, w_router, b_router, n_new):
    row = lambda a: a.reshape(1, -1)
    pad = LANES - N_EXPERTS
    wr = jnp.pad(w_router, ((0, 0), (0, pad)))
    wr_hi = wr.astype(_BF16)
    wr_lo = (wr - wr_hi.astype(_F32)).astype(_BF16)
    wcs = jnp.repeat(w_spatial[:, :n_new, :n_new].transpose(1, 2, 0).reshape(n_new * n_new, A_GROUPS),
                     LANES, axis=1)
    return dict(
        gmix=row(norm_mix_g), win=w_in.astype(_BF16), lnvg=row(ln_v_g), lnvb=row(ln_v_b),
        wsp=w_spatial, bsp=jnp.repeat(b_spatial.T, LANES, axis=1), wcs=wcs,
        bcs=jnp.repeat(b_spatial[:, :n_new].T, LANES, axis=1),
        wa=w_branch_a.astype(_BF16), cw=conv_w, cb=row(conv_b), lncg=row(ln_conv_g),
        lncb=row(ln_conv_b), wb=w_branch_b.astype(_BF16), bb=row(b_branch_b), bga=row(b_gate_a),
        bgb=row(b_gate_b), wout=w_out.astype(_BF16), gffn=row(norm_ffn_g),
        wr=jnp.concatenate([wr_hi, wr_lo], axis=1), br=jnp.pad(row(b_router), ((0, 0), (0, pad))))


def kernel(x_prompt, x_sample, cache_conv, norm_mix_g, w_in, ln_v_g, ln_v_b, w_spatial, b_spatial, w_branch_a, conv_w, conv_b, ln_conv_g, ln_conv_b, w_branch_b, b_branch_b, b_gate_a, b_gate_b, w_out, norm_ffn_g, w_router, b_router, w_up, b_up, w_down, b_down, norm_final_g):
    assert w_in.shape[0] == 1, "single trunk layer"
    bsz, seq, d = x_prompt.shape
    nseq, n_new, _ = x_sample.shape
    assert d == D_MODEL and seq % TT == 0 and TT % CHUNK == 0 and nseq % SB == 0
    assert n_new <= CHUNK and (nseq * n_new) % TT == 0
    w = _prep_weights(norm_mix_g[0], w_in[0], ln_v_g[0], ln_v_b[0], w_spatial[0], b_spatial[0],
                      w_branch_a[0], conv_w[0], conv_b[0], ln_conv_g[0], ln_conv_b[0],
                      w_branch_b[0], b_branch_b[0], b_gate_a[0], b_gate_b[0], w_out[0],
                      norm_ffn_g[0], w_router[0], b_router[0], n_new)

    x1_p, h2_p, mi_p, mf_p, cnt_p, cst_p = _mixer_prompt(x_prompt, w)
    x_t = x_sample.transpose(1, 0, 2)
    cache_t = cache_conv[0].transpose(1, 0, 2)
    x1_s, h2_s, mi_s, mf_s, cnt, glu_s, v_s = _mixer_sample(x_t, cache_t, cnt_p, w)
    n_p, n_s = bsz * seq, nseq * n_new
    x1_s = x1_s.reshape(n_s, d)
    h2_s = h2_s.reshape(n_s * ROW_TILES, LANES)
    mi_s = mi_s.reshape(n_s, LANES)
    mf_s = mf_s.reshape(n_s, LANES)

    counts = cnt[0, :N_EXPERTS].astype(jnp.int32)
    padded = ((counts + BLK - 1) // BLK) * BLK
    pend = jnp.cumsum(padded)
    pstart = pend - padded
    n_assign = (n_p + n_s) * TOP_K
    n_blk = -(-(n_assign + N_EXPERTS * (BLK - 1)) // BLK)
    nused = (pend[-1] // BLK).astype(jnp.int32)
    blk_ids = jnp.arange(n_blk, dtype=jnp.int32)
    first_row = jnp.minimum(blk_ids, nused - 1) * BLK
    be = jnp.sum((pend[None, :] <= first_row[:, None]).astype(jnp.int32), axis=1)
    be = jnp.minimum(be, N_EXPERTS - 1)
    experts_row = jnp.arange(N_EXPERTS, dtype=jnp.int32)

    def dest_of(mi):
        sel = mi[:, :TOP_K, None] == experts_row[None, None, :]
        dest = jnp.sum(jnp.where(sel, pstart[None, None, :], 0), axis=-1) + mi[:, TOP_K:2 * TOP_K]
        return dest.reshape(-1, 1, TT * TOP_K)

    dest_p, dest_s = dest_of(mi_p), dest_of(mi_s)
    xs = jnp.zeros((n_blk * BLK * ROW_TILES, LANES), _F32)
    xs = _dispatch(dest_p, h2_p, xs)
    xs = _dispatch(dest_s, h2_s, xs)
    outs = _experts(be, nused.reshape(1), xs, w_up[0], b_up[0][:, None, :], w_down[0],
                    b_down[0][:, None, :])
    g_final = norm_final_g.reshape(1, d)
    y_p = _combine(dest_p, x1_p, mf_p, g_final, outs)
    y_s = _combine(dest_s, x1_s, mf_s, g_final, outs)

    y_prompt = y_p.reshape(bsz, seq, d)
    y_sample = y_s.reshape(n_new, nseq, d).transpose(1, 0, 2)
    conv_state_prompt = cst_p[None]
    glu_new = glu_s.transpose(1, 0, 2)
    conv_state_sample = jnp.concatenate([cache_conv[0][:, n_new:], glu_new], axis=1)[None]
    chunk_v_sample = v_s.transpose(1, 0, 2)[None]
    return (y_prompt, y_sample, conv_state_prompt, conv_state_sample, chunk_v_sample)
```

```python
import functools

import jax
import jax.numpy as jnp
from jax import lax
from jax.experimental import pallas as pl
from jax.experimental.pallas import tpu as pltpu

D_MODEL = 1024
CHUNK = 128
A_GROUPS = 8
CONV_WIDTH = 31
HIST = 32
N_EXPERTS = 32
TOP_K = 4
D_EXPERT = 1024
SWIGLU_LIMIT = 7.0
SWIGLU_ALPHA = 1.702
EPS = 1e-5
LANES = 128
SUBLANES = 8
ROW_TILES = D_MODEL // LANES
TT = 256
CONV_ROWS = 64
SB = 32
BLK = 512
VMEM_LIMIT = 56 * 1024 * 1024

_F32 = jnp.float32
_BF16 = jnp.bfloat16
_INV_SQRT2 = 0.7071067811865476


def _rms(x, g):
    return x * lax.rsqrt(jnp.mean(x * x, axis=-1, keepdims=True) + EPS) * g


def _ln(x, g, b):
    mu = jnp.mean(x, axis=-1, keepdims=True)
    xc = x - mu
    var = jnp.mean(xc * xc, axis=-1, keepdims=True)
    return xc * lax.rsqrt(var + EPS) * g + b


def _gelu(x):
    return 0.5 * x * (1.0 + lax.erf(x * _INV_SQRT2))


def _sigmoid(x):
    return 1.0 / (1.0 + jnp.exp(-x))


def _dot(a, b):
    return jnp.dot(a, b, preferred_element_type=_F32)


def _store_row_tiles(ref, lead, n, val):
    for j in range(ROW_TILES):
        ref[lead + (pl.ds(j, n, stride=ROW_TILES), slice(None))] = val[:, j * LANES:(j + 1) * LANES]


def _load_row_tiles(ref, start, n):
    return jnp.concatenate(
        [ref[pl.ds(start + j, n, stride=ROW_TILES), :] for j in range(ROW_TILES)], axis=1)


def _route(h2, wr_ref, br_ref, carry_ref):
    n = h2.shape[0]
    h_hi = h2.astype(_BF16)
    h_lo = (h2 - h_hi.astype(_F32)).astype(_BF16)
    by_hi = _dot(h_hi, wr_ref[...])
    logits = (by_hi[:, :LANES] + by_hi[:, LANES:] + _dot(h_lo, wr_ref[:, :LANES])) + br_ref[...]
    lane = lax.broadcasted_iota(jnp.int32, (n, LANES), 1)
    l = jnp.where(lane < N_EXPERTS, logits, -jnp.inf)
    onehots, vals, idxs = [], [], []
    for _ in range(TOP_K):
        m = jnp.max(l, axis=-1, keepdims=True)
        idx = jnp.min(jnp.where(l == m, lane, LANES), axis=-1, keepdims=True)
        oh = lane == idx
        onehots.append(oh)
        vals.append(m)
        idxs.append(idx)
        l = jnp.where(oh, -jnp.inf, l)
    exps = [jnp.exp(v - vals[0]) for v in vals]
    denom = exps[0] + exps[1] + exps[2] + exps[3]
    gates = [e / denom for e in exps]

    oh_all = jnp.zeros((n, LANES), _F32)
    for oh in onehots:
        oh_all = oh_all + jnp.where(oh, 1.0, 0.0)
    ri = lax.broadcasted_iota(jnp.int32, (n, n), 0)
    ci = lax.broadcasted_iota(jnp.int32, (n, n), 1)
    tri = jnp.where(ci < ri, 1.0, 0.0).astype(_BF16)
    base = _dot(tri, oh_all.astype(_BF16)) + carry_ref[0:1, :]
    ranks = [jnp.sum(jnp.where(oh, base, 0.0), axis=-1, keepdims=True) for oh in onehots]
    carry_ref[...] = carry_ref[...] + jnp.sum(oh_all, axis=0, keepdims=True)

    meta_i = jnp.zeros((n, LANES), jnp.int32)
    meta_f = jnp.zeros((n, LANES), _F32)
    for k in range(TOP_K):
        meta_i = jnp.where(lane == k, idxs[k], meta_i)
        meta_i = jnp.where(lane == TOP_K + k, ranks[k].astype(jnp.int32), meta_i)
        meta_f = jnp.where(lane == k, gates[k], meta_f)
    return meta_i, meta_f


def _mixer_prompt_kernel(x_ref, gmix_ref, win_ref, lnvg_ref, lnvb_ref, wsp_ref, bsp_ref, wa_ref,
                         cw_ref, cb_ref, lncg_ref, lncb_ref, wb_ref, bb_ref, bga_ref, bgb_ref,
                         wout_ref, gffn_ref, wr_ref, br_ref,
                         x1_ref, h2_ref, mi_ref, mf_ref, cnt_ref, cst_ref,
                         cbuf, carry):
    b = pl.program_id(0)
    t = pl.program_id(1)

    @pl.when((b == 0) & (t == 0))
    def _():
        carry[...] = jnp.zeros_like(carry)

    @pl.when(t == 0)
    def _():
        cbuf[0:HIST, :] = jnp.zeros((HIST, D_MODEL), _F32)
        cbuf[HIST + TT:, :] = jnp.zeros((SUBLANES, D_MODEL), _F32)

    x = x_ref[...]
    hb = _rms(x, gmix_ref[...]).astype(_BF16)

    def proj(s):
        return _dot(hb, win_ref[:, s * D_MODEL:(s + 1) * D_MODEL])

    glu = proj(2) * _sigmoid(proj(3))
    cbuf[HIST:HIST + TT, :] = glu
    off = HIST - (CONV_WIDTH - 1)
    span = HIST + CONV_ROWS + SUBLANES
    conv_cols = []
    other, half, pieces = (0, 1, 4, 5), D_MODEL // 2, []
    for c in range(D_MODEL // LANES):
        cols = slice(c * LANES, (c + 1) * LANES)
        accs = []
        for r in range(TT // CONV_ROWS):
            acc = jnp.zeros((CONV_ROWS, LANES), _F32) + cb_ref[:, cols]
            window = cbuf[r * CONV_ROWS:r * CONV_ROWS + span, cols]
            for sh in range(SUBLANES):
                shifted = window if sh == 0 else pltpu.roll(window, span - sh, axis=0)
                for a in range(HIST // SUBLANES + 1):
                    k = a * SUBLANES + sh - off
                    if 0 <= k < CONV_WIDTH:
                        acc = acc + (cw_ref[k:k + 1, cols]
                                     * shifted[a * SUBLANES:a * SUBLANES + CONV_ROWS, :])
            accs.append(acc)
        conv_cols.append(jnp.concatenate(accs, axis=0))
        lo = other[c // 2] * D_MODEL + (c % 2) * half
        pieces.append(_dot(hb, win_ref[:, lo:lo + half]))
    conv = jnp.concatenate(conv_cols, axis=1)
    p_u, p_v, p_ga, p_gb = [jnp.concatenate(pieces[2 * i:2 * i + 2], axis=1) for i in range(4)]
    cbuf[0:HIST, :] = cbuf[TT:TT + HIST, :]
    cn = _ln(conv, lncg_ref[...], lncb_ref[...])
    y_b = _dot((cn * _sigmoid(cn)).astype(_BF16), wb_ref[...]) + bb_ref[...]

    u = _gelu(p_u)
    v = _ln(_gelu(p_v), lnvg_ref[...], lnvb_ref[...])
    ri = lax.broadcasted_iota(jnp.int32, (CHUNK, CHUNK), 0)
    ci = lax.broadcasted_iota(jnp.int32, (CHUNK, CHUNK), 1)
    s_rows = []
    for c in range(TT // CHUNK):
        s_cols = []
        for g in range(A_GROUPS):
            wm = jnp.where(ci <= ri, wsp_ref[g], 0.0).astype(_BF16)
            vb = v[c * CHUNK:(c + 1) * CHUNK, g * LANES:(g + 1) * LANES].astype(_BF16)
            s_cols.append(_dot(wm, vb))
        s_rows.append(jnp.concatenate(s_cols, axis=1) + bsp_ref[...])
    s = jnp.concatenate(s_rows, axis=0)
    y_a = _dot((u * s).astype(_BF16), wa_ref[...])

    merged = _sigmoid(p_ga + bga_ref[...]) * y_a + _sigmoid(p_gb + bgb_ref[...]) * y_b
    x1 = x + _dot(merged.astype(_BF16), wout_ref[...])
    x1_ref[...] = x1

    h2 = _rms(x1, gffn_ref[...])
    _store_row_tiles(h2_ref, (), TT, h2)
    meta_i, meta_f = _route(h2, wr_ref, br_ref, carry)
    mi_ref[...] = meta_i
    mf_ref[...] = meta_f
    cnt_ref[...] = carry[...]

    @pl.when(t == pl.num_programs(1) - 1)
    def _():
        cst_ref[...] = cbuf[HIST - (CONV_WIDTH - 1):HIST, :]


def _full(shape):
    return pl.BlockSpec(shape, lambda *_: (0,) * len(shape))


def _mixer_prompt(x, w):
    bsz, seq, d = x.shape
    n_tok = bsz * seq
    nt = seq // TT
    tok = lambda b, t: (b * nt + t, 0)
    weights = (w["gmix"], w["win"], w["lnvg"], w["lnvb"], w["wsp"], w["bsp"], w["wa"], w["cw"],
               w["cb"], w["lncg"], w["lncb"], w["wb"], w["bb"], w["bga"], w["bgb"], w["wout"],
               w["gffn"], w["wr"], w["br"])
    return pl.pallas_call(
        _mixer_prompt_kernel,
        grid=(bsz, nt),
        in_specs=[pl.BlockSpec((None, TT, d), lambda b, t: (b, t, 0))]
                 + [_full(a.shape) for a in weights],
        out_specs=[
            pl.BlockSpec((TT, d), tok),
            pl.BlockSpec((TT * ROW_TILES, LANES), tok),
            pl.BlockSpec((TT, LANES), tok),
            pl.BlockSpec((TT, LANES), tok),
            _full((SUBLANES, LANES)),
            pl.BlockSpec((None, CONV_WIDTH - 1, d), lambda b, t: (b, 0, 0)),
        ],
        out_shape=[
            jax.ShapeDtypeStruct((n_tok, d), _F32),
            jax.ShapeDtypeStruct((n_tok * ROW_TILES, LANES), _F32),
            jax.ShapeDtypeStruct((n_tok, LANES), jnp.int32),
            jax.ShapeDtypeStruct((n_tok, LANES), _F32),
            jax.ShapeDtypeStruct((SUBLANES, LANES), _F32),
            jax.ShapeDtypeStruct((bsz, CONV_WIDTH - 1, d), _F32),
        ],
        scratch_shapes=[pltpu.VMEM((HIST + TT + SUBLANES, d), _F32),
                        pltpu.VMEM((SUBLANES, LANES), _F32)],
        compiler_params=pltpu.CompilerParams(
            dimension_semantics=("arbitrary", "arbitrary"), vmem_limit_bytes=VMEM_LIMIT),
        name="mixer_prompt",
    )(x, *weights)


def _mixer_sample_kernel(x_ref, cache_ref, cnt0_ref, gmix_ref, win_ref, lnvg_ref, lnvb_ref,
                         wcs_ref, bcs_ref, wa_ref, cw_ref, cb_ref, lncg_ref, lncb_ref, wb_ref,
                         bb_ref, bga_ref, bgb_ref, wout_ref, gffn_ref, wr_ref, br_ref,
                         x1_ref, h2_ref, mi_ref, mf_ref, cnt_ref, glu_ref, v_ref,
                         carry):
    nt = x_ref.shape[0]
    n = nt * SB

    @pl.when(pl.program_id(0) == 0)
    def _():
        carry[...] = cnt0_ref[...]

    x = x_ref[...].reshape(n, D_MODEL)
    hb = _rms(x, gmix_ref[...]).astype(_BF16)

    def proj(s):
        return _dot(hb, win_ref[:, s * D_MODEL:(s + 1) * D_MODEL])

    u = _gelu(proj(0))
    v = _ln(_gelu(proj(1)), lnvg_ref[...], lnvb_ref[...])
    v_ref[...] = v.reshape(nt, SB, D_MODEL)
    s_rows = []
    for i in range(nt):
        acc = jnp.zeros((SB, D_MODEL), _F32) + bcs_ref[i:i + 1, :]
        for j in range(i + 1):
            acc = acc + wcs_ref[i * nt + j:i * nt + j + 1, :] * v[j * SB:(j + 1) * SB, :]
        s_rows.append(acc)
    s = jnp.concatenate(s_rows, axis=0)
    y_a = _dot((u * s).astype(_BF16), wa_ref[...])

    glu = proj(2) * _sigmoid(proj(3))
    glu_ref[...] = glu.reshape(nt, SB, D_MODEL)
    hist = CONV_WIDTH - 1
    conv_rows = []
    for i in range(nt):
        acc = jnp.zeros((SB, D_MODEL), _F32) + cb_ref[...]
        for k in range(CONV_WIDTH):
            p = i + k
            src = cache_ref[p] if p < hist else glu[(p - hist) * SB:(p - hist + 1) * SB, :]
            acc = acc + cw_ref[k:k + 1, :] * src
        conv_rows.append(acc)
    cn = _ln(jnp.concatenate(conv_rows, axis=0), lncg_ref[...], lncb_ref[...])
    y_b = _dot((cn * _sigmoid(cn)).astype(_BF16), wb_ref[...]) + bb_ref[...]

    merged = (_sigmoid(proj(4) + bga_ref[...]) * y_a + _sigmoid(proj(5) + bgb_ref[...]) * y_b)
    x1 = x + _dot(merged.astype(_BF16), wout_ref[...])
    x1_ref[...] = x1.reshape(nt, SB, D_MODEL)

    h2 = _rms(x1, gffn_ref[...])
    for i in range(nt):
        _store_row_tiles(h2_ref, (i,), SB, h2[i * SB:(i + 1) * SB, :])
    meta_i, meta_f = _route(h2, wr_ref, br_ref, carry)
    mi_ref[...] = meta_i.reshape(nt, SB, LANES)
    mf_ref[...] = meta_f.reshape(nt, SB, LANES)
    cnt_ref[...] = carry[...]


def _mixer_sample(x_t, cache_t, cnt0, w):
    nt, nseq, d = x_t.shape
    hist = cache_t.shape[0]
    weights = (w["gmix"], w["win"], w["lnvg"], w["lnvb"], w["wcs"], w["bcs"], w["wa"], w["cw"],
               w["cb"], w["lncg"], w["lncb"], w["wb"], w["bb"], w["bga"], w["bgb"], w["wout"],
               w["gffn"], w["wr"], w["br"])
    seqs = lambda i: (0, i, 0)
    return pl.pallas_call(
        _mixer_sample_kernel,
        grid=(nseq // SB,),
        in_specs=[pl.BlockSpec((nt, SB, d), seqs), pl.BlockSpec((hist, SB, d), seqs),
                  _full(cnt0.shape)] + [_full(a.shape) for a in weights],
        out_specs=[
            pl.BlockSpec((nt, SB, d), seqs),
            pl.BlockSpec((nt, SB * ROW_TILES, LANES), seqs),
            pl.BlockSpec((nt, SB, LANES), seqs),
            pl.BlockSpec((nt, SB, LANES), seqs),
            _full((SUBLANES, LANES)),
            pl.BlockSpec((nt, SB, d), seqs),
            pl.BlockSpec((nt, SB, d), seqs),
        ],
        out_shape=[
            jax.ShapeDtypeStruct((nt, nseq, d), _F32),
            jax.ShapeDtypeStruct((nt, nseq * ROW_TILES, LANES), _F32),
            jax.ShapeDtypeStruct((nt, nseq, LANES), jnp.int32),
            jax.ShapeDtypeStruct((nt, nseq, LANES), _F32),
            jax.ShapeDtypeStruct((SUBLANES, LANES), _F32),
            jax.ShapeDtypeStruct((nt, nseq, d), _F32),
            jax.ShapeDtypeStruct((nt, nseq, d), _F32),
        ],
        scratch_shapes=[pltpu.VMEM((SUBLANES, LANES), _F32)],
        compiler_params=pltpu.CompilerParams(
            dimension_semantics=("arbitrary",), vmem_limit_bytes=VMEM_LIMIT),
        name="mixer_sample",
    )(x_t, cache_t, cnt0, *weights)


def _row_tile(ref, row):
    return ref.at[pl.ds(pl.multiple_of(row * ROW_TILES, ROW_TILES), ROW_TILES), :]


def _dispatch_kernel(n_first, n_blk, pad_start_ref, pad_len_ref, nused_ref, dest_ref, h2a_ref,
                     h2b_ref, xs_hbm, zbuf, sem, zsem):
    i = pl.program_id(0)

    def fill(start):
        def go(copy):
            copy.start() if start else copy.wait()

        def per_expert(e, carry):
            first, length = pad_start_ref[e], pad_len_ref[e]
            for s in range(1, BLK.bit_length()):
                p = BLK >> s

                @pl.when((length & p) != 0)
                def _():
                    row = first + (length & (-2 * p))
                    go(pltpu.make_async_copy(
                        zbuf.at[pl.ds(0, p * ROW_TILES), :],
                        xs_hbm.at[pl.ds(pl.multiple_of(row * ROW_TILES, ROW_TILES), p * ROW_TILES), :],
                        zsem))
            return carry

        def per_tail_block(b, carry):
            go(pltpu.make_async_copy(
                zbuf, xs_hbm.at[pl.ds(pl.multiple_of(b * BLK * ROW_TILES, ROW_TILES),
                                      BLK * ROW_TILES), :], zsem))
            return carry

        lax.fori_loop(0, N_EXPERTS, per_expert, 0)
        lax.fori_loop(nused_ref[0], n_blk, per_tail_block, 0)

    @pl.when(i == 0)
    def _():
        zbuf[...] = jnp.zeros_like(zbuf)
        fill(True)

    def scatter(h2_ref):
        def issue(t, carry):
            src = _row_tile(h2_ref, t)
            for k in range(TOP_K):
                pltpu.make_async_copy(src, _row_tile(xs_hbm, dest_ref[0, 0, t * TOP_K + k]),
                                      sem).start(priority=k % 2)
            return carry

        lax.fori_loop(0, TT, issue, 0)
        for _ in range(TOP_K):
            pltpu.make_async_copy(h2_ref, xs_hbm.at[pl.ds(0, TT * ROW_TILES), :], sem).wait()

    @pl.when(i < n_first)
    def _():
        scatter(h2a_ref)

    @pl.when(i >= n_first)
    def _():
        scatter(h2b_ref)

    @pl.when(i == 0)
    def _():
        fill(False)


def _dispatch(pad_start, pad_len, nused, dest, h2_a, h2_b, n_blk):
    n_tiles = dest.shape[0]
    n_first = h2_a.shape[0] // (TT * ROW_TILES)
    tile = (TT * ROW_TILES, LANES)
    grid_spec = pltpu.PrefetchScalarGridSpec(
        num_scalar_prefetch=3,
        grid=(n_tiles,),
        in_specs=[
            pl.BlockSpec((1, 1, TT * TOP_K), lambda i, *_: (i, 0, 0), memory_space=pltpu.SMEM),
            pl.BlockSpec(tile, lambda i, *_: (jnp.minimum(i, n_first - 1), 0)),
            pl.BlockSpec(tile, lambda i, *_: (jnp.maximum(i - n_first, 0), 0)),
        ],
        out_specs=pl.BlockSpec(memory_space=pl.ANY),
        scratch_shapes=[pltpu.VMEM((BLK * ROW_TILES, LANES), _F32), pltpu.SemaphoreType.DMA,
                        pltpu.SemaphoreType.DMA],
    )
    return pl.pallas_call(
        functools.partial(_dispatch_kernel, n_first, n_blk),
        grid_spec=grid_spec,
        out_shape=jax.ShapeDtypeStruct((n_blk * BLK * ROW_TILES, LANES), _F32),
        compiler_params=pltpu.CompilerParams(
            dimension_semantics=("arbitrary",), has_side_effects=True),
        name="dispatch",
    )(pad_start, pad_len, nused, dest, h2_a, h2_b)


def _experts_kernel(be_ref, nused_ref, xs_ref, wup_ref, bup_ref, wdn_ref, bdn_ref, out_ref,
                    wup_bf, wdn_bf):
    i = pl.program_id(0)
    prev = be_ref[jnp.maximum(i - 1, 0)]

    @pl.when((i == 0) | (be_ref[i] != prev))
    def _():
        rows = 128

        def cast(r, carry):
            sl = pl.ds(pl.multiple_of(r * rows, rows), rows)
            wup_bf[sl, :] = wup_ref[sl, :].astype(_BF16)
            wdn_bf[sl, :] = wdn_ref[sl, :].astype(_BF16)
            return carry

        lax.fori_loop(0, D_MODEL // rows, cast, 0)

    @pl.when(i < nused_ref[0])
    def _():
        xb = _load_row_tiles(xs_ref, 0, BLK).astype(_BF16)
        gate = _dot(xb, wup_bf[:, :D_EXPERT]) + bup_ref[:, :D_EXPERT]
        up = _dot(xb, wup_bf[:, D_EXPERT:]) + bup_ref[:, D_EXPERT:]
        gate = jnp.minimum(gate, SWIGLU_LIMIT)
        up = jnp.clip(up, -SWIGLU_LIMIT, SWIGLU_LIMIT)
        act = (up + 1.0) * (gate * _sigmoid(SWIGLU_ALPHA * gate))
        o = _dot(act.astype(_BF16), wdn_bf[...]) + bdn_ref[...]
        _store_row_tiles(out_ref, (), BLK, o)

    @pl.when(i >= nused_ref[0])
    def _():
        out_ref[...] = jnp.zeros_like(out_ref)


def _experts(block_expert, nused, xs, w_up, b_up, w_down, b_down):
    n_blk = block_expert.shape[0]
    rows = lambda i, be, nu: (jnp.minimum(i, nu[0] - 1), 0)
    expert = lambda i, be, nu: (be[i], 0, 0)
    d, f2 = w_up.shape[1], w_up.shape[2]
    grid_spec = pltpu.PrefetchScalarGridSpec(
        num_scalar_prefetch=2,
        grid=(n_blk,),
        in_specs=[
            pl.BlockSpec((BLK * ROW_TILES, LANES), rows),
            pl.BlockSpec((None, d, f2), expert),
            pl.BlockSpec((None, 1, f2), expert),
            pl.BlockSpec((None, w_down.shape[1], d), expert),
            pl.BlockSpec((None, 1, d), expert),
        ],
        out_specs=pl.BlockSpec((BLK * ROW_TILES, LANES), lambda i, be, nu: (i, 0)),
        scratch_shapes=[pltpu.VMEM((d, f2), _BF16), pltpu.VMEM((w_down.shape[1], d), _BF16)],
    )
    return pl.pallas_call(
        _experts_kernel,
        grid_spec=grid_spec,
        out_shape=jax.ShapeDtypeStruct(xs.shape, _F32),
        compiler_params=pltpu.CompilerParams(
            dimension_semantics=("arbitrary",), vmem_limit_bytes=VMEM_LIMIT),
        name="experts",
    )(block_expert, nused, xs, w_up, b_up, w_down, b_down)


def _combine_kernel(n, dest_ref, dest_next_ref, x1_ref, mf_ref, gfin_ref, outs_hbm, y_ref, buf,
                    sems):
    i = pl.program_id(0)

    def gather(d_ref, slot):
        def issue(t, carry):
            for k in range(TOP_K):
                dst = buf.at[slot, pl.ds(pl.multiple_of((k * TT + t) * ROW_TILES, ROW_TILES),
                                         ROW_TILES), :]
                pltpu.make_async_copy(_row_tile(outs_hbm, d_ref[0, 0, t * TOP_K + k]), dst,
                                      sems.at[slot]).start(priority=k % 2)
            return carry

        lax.fori_loop(0, TT, issue, 0)

    slot = i % 2

    @pl.when(i == 0)
    def _():
        gather(dest_ref, 0)

    @pl.when(i + 1 < n)
    def _():
        gather(dest_next_ref, 1 - slot)

    pltpu.make_async_copy(outs_hbm.at[pl.ds(0, buf.shape[1]), :], buf.at[slot], sems.at[slot]).wait()

    y = x1_ref[...]
    gates = mf_ref[...]
    rows = buf.at[slot]
    for k in range(TOP_K):
        y = y + gates[:, k:k + 1] * _load_row_tiles(rows, k * TT * ROW_TILES, TT)
    y_ref[...] = _rms(y, gfin_ref[...])


def _combine(dest, x1, meta_f, g_final, outs):
    n_tok, d = x1.shape
    n_tiles = n_tok // TT
    tok = lambda i: (i, 0)
    return pl.pallas_call(
        functools.partial(_combine_kernel, n_tiles),
        grid=(n_tiles,),
        in_specs=[
            pl.BlockSpec((1, 1, TT * TOP_K), lambda i: (i, 0, 0), memory_space=pltpu.SMEM),
            pl.BlockSpec((1, 1, TT * TOP_K), lambda i: (jnp.minimum(i + 1, n_tiles - 1), 0, 0),
                         memory_space=pltpu.SMEM),
            pl.BlockSpec((TT, d), tok),
            pl.BlockSpec((TT, LANES), tok),
            _full(g_final.shape),
            pl.BlockSpec(memory_space=pl.ANY),
        ],
        out_specs=pl.BlockSpec((TT, d), tok),
        out_shape=jax.ShapeDtypeStruct((n_tok, d), _F32),
        scratch_shapes=[pltpu.VMEM((2, TOP_K * TT * ROW_TILES, LANES), _F32),
                        pltpu.SemaphoreType.DMA((2,))],
        compiler_params=pltpu.CompilerParams(
            dimension_semantics=("arbitrary",), vmem_limit_bytes=VMEM_LIMIT),
        name="combine",
    )(dest, dest, x1, meta_f, g_final, outs)


def _prep_weights(norm_mix_g, w_in, ln_v_g, ln_v_b, w_spatial, b_spatial, w_branch_a, conv_w,
                  conv_b, ln_conv_g, ln_conv_b, w_branch_b, b_branch_b, b_gate_a, b_gate_b, w_out,
                  norm_ffn_g, w_router, b_router, n_new):
    row = lambda a: a.reshape(1, -1)
    pad = LANES - N_EXPERTS
    wr = jnp.pad(w_router, ((0, 0), (0, pad)))
    wr_hi = wr.astype(_BF16)
    wr_lo = (wr - wr_hi.astype(_F32)).astype(_BF16)
    wcs = jnp.repeat(w_spatial[:, :n_new, :n_new].transpose(1, 2, 0).reshape(n_new * n_new, A_GROUPS),
                     LANES, axis=1)
    return dict(
        gmix=row(norm_mix_g), win=w_in.astype(_BF16), lnvg=row(ln_v_g), lnvb=row(ln_v_b),
        wsp=w_spatial, bsp=jnp.repeat(b_spatial.T, LANES, axis=1), wcs=wcs,
        bcs=jnp.repeat(b_spatial[:, :n_new].T, LANES, axis=1),
        wa=w_branch_a.astype(_BF16), cw=conv_w, cb=row(conv_b), lncg=row(ln_conv_g),
        lncb=row(ln_conv_b), wb=w_branch_b.astype(_BF16), bb=row(b_branch_b), bga=row(b_gate_a),
        bgb=row(b_gate_b), wout=w_out.astype(_BF16), gffn=row(norm_ffn_g),
        wr=jnp.concatenate([wr_hi, wr_lo], axis=1), br=jnp.pad(row(b_router), ((0, 0), (0, pad))))


def kernel(x_prompt, x_sample, cache_conv, norm_mix_g, w_in, ln_v_g, ln_v_b, w_spatial, b_spatial, w_branch_a, conv_w, conv_b, ln_conv_g, ln_conv_b, w_branch_b, b_branch_b, b_gate_a, b_gate_b, w_out, norm_ffn_g, w_router, b_router, w_up, b_up, w_down, b_down, norm_final_g):
    assert w_in.shape[0] == 1, "single trunk layer"
    bsz, seq, d = x_prompt.shape
    nseq, n_new, _ = x_sample.shape
    assert d == D_MODEL and seq % TT == 0 and TT % CHUNK == 0 and nseq % SB == 0
    assert n_new <= CHUNK and (nseq * n_new) % TT == 0
    w = _prep_weights(norm_mix_g[0], w_in[0], ln_v_g[0], ln_v_b[0], w_spatial[0], b_spatial[0],
                      w_branch_a[0], conv_w[0], conv_b[0], ln_conv_g[0], ln_conv_b[0],
                      w_branch_b[0], b_branch_b[0], b_gate_a[0], b_gate_b[0], w_out[0],
                      norm_ffn_g[0], w_router[0], b_router[0], n_new)

    x1_p, h2_p, mi_p, mf_p, cnt_p, cst_p = _mixer_prompt(x_prompt, w)
    x_t = x_sample.transpose(1, 0, 2)
    cache_t = cache_conv[0].transpose(1, 0, 2)
    x1_s, h2_s, mi_s, mf_s, cnt, glu_s, v_s = _mixer_sample(x_t, cache_t, cnt_p, w)
    n_p, n_s = bsz * seq, nseq * n_new
    x1_s = x1_s.reshape(n_s, d)
    h2_s = h2_s.reshape(n_s * ROW_TILES, LANES)
    mi_s = mi_s.reshape(n_s, LANES)
    mf_s = mf_s.reshape(n_s, LANES)

    counts = cnt[0, :N_EXPERTS].astype(jnp.int32)
    padded = ((counts + BLK - 1) // BLK) * BLK
    pend = jnp.cumsum(padded)
    pstart = pend - padded
    n_assign = (n_p + n_s) * TOP_K
    n_blk = -(-(n_assign + N_EXPERTS * (BLK - 1)) // BLK)
    nused = (pend[-1] // BLK).astype(jnp.int32)
    blk_ids = jnp.arange(n_blk, dtype=jnp.int32)
    first_row = jnp.minimum(blk_ids, nused - 1) * BLK
    be = jnp.sum((pend[None, :] <= first_row[:, None]).astype(jnp.int32), axis=1)
    be = jnp.minimum(be, N_EXPERTS - 1)
    experts_row = jnp.arange(N_EXPERTS, dtype=jnp.int32)

    def dest_of(mi):
        sel = mi[:, :TOP_K, None] == experts_row[None, None, :]
        dest = jnp.sum(jnp.where(sel, pstart[None, None, :], 0), axis=-1) + mi[:, TOP_K:2 * TOP_K]
        return dest.reshape(-1, 1, TT * TOP_K)

    dest_p, dest_s = dest_of(mi_p), dest_of(mi_s)
    xs = _dispatch(pstart + counts, padded - counts, nused.reshape(1),
                   jnp.concatenate([dest_p, dest_s], axis=0), h2_p, h2_s, n_blk)
    outs = _experts(be, nused.reshape(1), xs, w_up[0], b_up[0][:, None, :], w_down[0],
                    b_down[0][:, None, :])
    g_final = norm_final_g.reshape(1, d)
    y_p = _combine(dest_p, x1_p, mf_p, g_final, outs)
    y_s = _combine(dest_s, x1_s, mf_s, g_final, outs)

    y_prompt = y_p.reshape(bsz, seq, d)
    y_sample = y_s.reshape(n_new, nseq, d).transpose(1, 0, 2)
    conv_state_prompt = cst_p[None]
    glu_new = glu_s.transpose(1, 0, 2)
    conv_state_sample = jnp.concatenate([cache_conv[0][:, n_new:], glu_new], axis=1)[None]
    chunk_v_sample = v_s.transpose(1, 0, 2)[None]
    return (y_prompt, y_sample, conv_state_prompt, conv_state_sample, chunk_v_sample)
```

```python
import functools

import jax
import jax.numpy as jnp
from jax import lax
from jax.experimental import pallas as pl
from jax.experimental.pallas import tpu as pltpu

D_MODEL = 1024
CHUNK = 128
A_GROUPS = 8
CONV_WIDTH = 31
HIST = 32
N_EXPERTS = 32
TOP_K = 4
D_EXPERT = 1024
SWIGLU_LIMIT = 7.0
SWIGLU_ALPHA = 1.702
EPS = 1e-5
LANES = 128
SUBLANES = 8
ROW_TILES = D_MODEL // LANES
N_COL = D_MODEL // LANES
TT = 256
CONV_ROWS = 64
SB = 32
BLK = 512
VMEM_LIMIT = 56 * 1024 * 1024

_F32 = jnp.float32
_BF16 = jnp.bfloat16
_INV_SQRT2 = 0.7071067811865476


def _rms(x, g):
    return x * lax.rsqrt(jnp.mean(x * x, axis=-1, keepdims=True) + EPS) * g


def _ln(x, g, b):
    mu = jnp.mean(x, axis=-1, keepdims=True)
    xc = x - mu
    var = jnp.mean(xc * xc, axis=-1, keepdims=True)
    return xc * lax.rsqrt(var + EPS) * g + b


def _gelu(x):
    return 0.5 * x * (1.0 + lax.erf(x * _INV_SQRT2))


def _sigmoid(x):
    return 1.0 / (1.0 + jnp.exp(-x))


def _dot(a, b):
    return jnp.dot(a, b, preferred_element_type=_F32)


def _store_row_tiles(ref, lead, n, val):
    for j in range(ROW_TILES):
        ref[lead + (pl.ds(j, n, stride=ROW_TILES), slice(None))] = val[:, j * LANES:(j + 1) * LANES]


def _load_row_tiles(ref, start, n):
    return jnp.concatenate(
        [ref[pl.ds(start + j, n, stride=ROW_TILES), :] for j in range(ROW_TILES)], axis=1)


def _route(h2, wr_ref, br_ref, carry_ref):
    n = h2.shape[0]
    h_hi = h2.astype(_BF16)
    h_lo = (h2 - h_hi.astype(_F32)).astype(_BF16)
    by_hi = _dot(h_hi, wr_ref[...])
    logits = (by_hi[:, :LANES] + by_hi[:, LANES:] + _dot(h_lo, wr_ref[:, :LANES])) + br_ref[...]
    lane = lax.broadcasted_iota(jnp.int32, (n, LANES), 1)
    l = jnp.where(lane < N_EXPERTS, logits, -jnp.inf)
    onehots, vals, idxs = [], [], []
    for _ in range(TOP_K):
        m = jnp.max(l, axis=-1, keepdims=True)
        idx = jnp.min(jnp.where(l == m, lane, LANES), axis=-1, keepdims=True)
        oh = lane == idx
        onehots.append(oh)
        vals.append(m)
        idxs.append(idx)
        l = jnp.where(oh, -jnp.inf, l)
    exps = [jnp.exp(v - vals[0]) for v in vals]
    denom = exps[0] + exps[1] + exps[2] + exps[3]
    gates = [e / denom for e in exps]

    oh_all = jnp.zeros((n, LANES), _F32)
    for oh in onehots:
        oh_all = oh_all + jnp.where(oh, 1.0, 0.0)
    ri = lax.broadcasted_iota(jnp.int32, (n, n), 0)
    ci = lax.broadcasted_iota(jnp.int32, (n, n), 1)
    tri = jnp.where(ci < ri, 1.0, 0.0).astype(_BF16)
    base = _dot(tri, oh_all.astype(_BF16)) + carry_ref[0:1, :]
    ranks = [jnp.sum(jnp.where(oh, base, 0.0), axis=-1, keepdims=True) for oh in onehots]
    carry_ref[...] = carry_ref[...] + jnp.sum(oh_all, axis=0, keepdims=True)

    meta_i = jnp.zeros((n, LANES), jnp.int32)
    meta_f = jnp.zeros((n, LANES), _F32)
    for k in range(TOP_K):
        meta_i = jnp.where(lane == k, idxs[k], meta_i)
        meta_i = jnp.where(lane == TOP_K + k, ranks[k].astype(jnp.int32), meta_i)
        meta_f = jnp.where(lane == k, gates[k], meta_f)
    return meta_i, meta_f


def _mixer_prompt_kernel(x_ref, gmix_ref, wglu_ref, wrest_ref, lnvg_ref, lnvb_ref, wsp_ref, bsp_ref,
                         wa_ref, cw_ref, cb_ref, lncg_ref, lncb_ref, wb_ref, bb_ref, bga_ref,
                         bgb_ref, wout_ref, gffn_ref, wr_ref, br_ref,
                         x1_ref, h2_ref, mi_ref, mf_ref, cnt_ref, cst_ref,
                         cbuf, convbuf, pbuf, hbuf, carry):
    b = pl.program_id(0)
    t = pl.program_id(1)

    @pl.when((b == 0) & (t == 0))
    def _():
        carry[...] = jnp.zeros_like(carry)

    @pl.when(t == 0)
    def _():
        cbuf[:, 0:HIST, :] = jnp.zeros((N_COL, HIST, LANES), _F32)
        cbuf[:, HIST + TT:, :] = jnp.zeros((N_COL, SUBLANES, LANES), _F32)

    x = x_ref[...]
    hb = _rms(x, gmix_ref[...]).astype(_BF16)
    hbuf[...] = hb

    glu = _dot(hb, wglu_ref[:, :D_MODEL]) * _sigmoid(_dot(hb, wglu_ref[:, D_MODEL:]))
    for c in range(N_COL):
        cbuf[c, HIST:HIST + TT, :] = glu[:, c * LANES:(c + 1) * LANES]
    off = HIST - (CONV_WIDTH - 1)
    span = HIST + CONV_ROWS + SUBLANES

    def col_block(c, carry):
        for r in range(TT // CONV_ROWS):
            acc = jnp.zeros((CONV_ROWS, LANES), _F32) + cb_ref[c]
            window = cbuf[c, r * CONV_ROWS:r * CONV_ROWS + span, :]
            for sh in range(SUBLANES):
                shifted = window if sh == 0 else pltpu.roll(window, span - sh, axis=0)
                for a in range(HIST // SUBLANES + 1):
                    k = a * SUBLANES + sh - off
                    if 0 <= k < CONV_WIDTH:
                        acc = acc + (cw_ref[c, k:k + 1, :]
                                     * shifted[a * SUBLANES:a * SUBLANES + CONV_ROWS, :])
            convbuf[c, r * CONV_ROWS:(r + 1) * CONV_ROWS, :] = acc
        pbuf[c] = _dot(hbuf[...], wrest_ref[c])
        return carry

    lax.fori_loop(0, N_COL, col_block, 0)
    conv = jnp.concatenate([convbuf[c] for c in range(N_COL)], axis=1)
    p_u, p_v, p_ga, p_gb = [jnp.concatenate([pbuf[2 * i], pbuf[2 * i + 1]], axis=1)
                            for i in range(4)]
    for c in range(N_COL):
        cbuf[c, 0:HIST, :] = cbuf[c, TT:TT + HIST, :]
    cn = _ln(conv, lncg_ref[...], lncb_ref[...])
    y_b = _dot((cn * _sigmoid(cn)).astype(_BF16), wb_ref[...]) + bb_ref[...]

    u = _gelu(p_u)
    v = _ln(_gelu(p_v), lnvg_ref[...], lnvb_ref[...])
    ri = lax.broadcasted_iota(jnp.int32, (CHUNK, CHUNK), 0)
    ci = lax.broadcasted_iota(jnp.int32, (CHUNK, CHUNK), 1)
    s_rows = []
    for c in range(TT // CHUNK):
        s_cols = []
        for g in range(A_GROUPS):
            wm = jnp.where(ci <= ri, wsp_ref[g], 0.0).astype(_BF16)
            vb = v[c * CHUNK:(c + 1) * CHUNK, g * LANES:(g + 1) * LANES].astype(_BF16)
            s_cols.append(_dot(wm, vb))
        s_rows.append(jnp.concatenate(s_cols, axis=1) + bsp_ref[...])
    s = jnp.concatenate(s_rows, axis=0)
    y_a = _dot((u * s).astype(_BF16), wa_ref[...])

    merged = _sigmoid(p_ga + bga_ref[...]) * y_a + _sigmoid(p_gb + bgb_ref[...]) * y_b
    x1 = x + _dot(merged.astype(_BF16), wout_ref[...])
    x1_ref[...] = x1

    h2 = _rms(x1, gffn_ref[...])
    _store_row_tiles(h2_ref, (), TT, h2)
    meta_i, meta_f = _route(h2, wr_ref, br_ref, carry)
    mi_ref[...] = meta_i
    mf_ref[...] = meta_f
    cnt_ref[...] = carry[...]

    @pl.when(t == pl.num_programs(1) - 1)
    def _():
        cst_ref[...] = jnp.concatenate(
            [cbuf[c, HIST - (CONV_WIDTH - 1):HIST, :] for c in range(N_COL)], axis=1)


def _full(shape):
    return pl.BlockSpec(shape, lambda *_: (0,) * len(shape))


def _mixer_prompt(x, w):
    bsz, seq, d = x.shape
    n_tok = bsz * seq
    nt = seq // TT
    tok = lambda b, t: (b * nt + t, 0)
    weights = (w["gmix"], w["wglu"], w["wrest"], w["lnvg"], w["lnvb"], w["wsp"], w["bsp"], w["wa"],
               w["cw3"], w["cb3"], w["lncg"], w["lncb"], w["wb"], w["bb"], w["bga"], w["bgb"],
               w["wout"], w["gffn"], w["wr"], w["br"])
    return pl.pallas_call(
        _mixer_prompt_kernel,
        grid=(bsz, nt),
        in_specs=[pl.BlockSpec((None, TT, d), lambda b, t: (b, t, 0))]
                 + [_full(a.shape) for a in weights],
        out_specs=[
            pl.BlockSpec((TT, d), tok),
            pl.BlockSpec((TT * ROW_TILES, LANES), tok),
            pl.BlockSpec((TT, LANES), tok),
            pl.BlockSpec((TT, LANES), tok),
            _full((SUBLANES, LANES)),
            pl.BlockSpec((None, CONV_WIDTH - 1, d), lambda b, t: (b, 0, 0)),
        ],
        out_shape=[
            jax.ShapeDtypeStruct((n_tok, d), _F32),
            jax.ShapeDtypeStruct((n_tok * ROW_TILES, LANES), _F32),
            jax.ShapeDtypeStruct((n_tok, LANES), jnp.int32),
            jax.ShapeDtypeStruct((n_tok, LANES), _F32),
            jax.ShapeDtypeStruct((SUBLANES, LANES), _F32),
            jax.ShapeDtypeStruct((bsz, CONV_WIDTH - 1, d), _F32),
        ],
        scratch_shapes=[pltpu.VMEM((N_COL, HIST + TT + SUBLANES, LANES), _F32),
                        pltpu.VMEM((N_COL, TT, LANES), _F32),
                        pltpu.VMEM((N_COL, TT, 4 * d // N_COL), _F32),
                        pltpu.VMEM((TT, d), _BF16),
                        pltpu.VMEM((SUBLANES, LANES), _F32)],
        compiler_params=pltpu.CompilerParams(
            dimension_semantics=("arbitrary", "arbitrary"), vmem_limit_bytes=VMEM_LIMIT),
        name="mixer_prompt",
    )(x, *weights)


def _mixer_sample_kernel(x_ref, cache_ref, cnt0_ref, gmix_ref, win_ref, lnvg_ref, lnvb_ref,
                         wcs_ref, bcs_ref, wa_ref, cw_ref, cb_ref, lncg_ref, lncb_ref, wb_ref,
                         bb_ref, bga_ref, bgb_ref, wout_ref, gffn_ref, wr_ref, br_ref,
                         x1_ref, h2_ref, mi_ref, mf_ref, cnt_ref, glu_ref, v_ref,
                         carry):
    nt = x_ref.shape[0]
    n = nt * SB

    @pl.when(pl.program_id(0) == 0)
    def _():
        carry[...] = cnt0_ref[...]

    x = x_ref[...].reshape(n, D_MODEL)
    hb = _rms(x, gmix_ref[...]).astype(_BF16)

    def proj(s):
        return _dot(hb, win_ref[:, s * D_MODEL:(s + 1) * D_MODEL])

    u = _gelu(proj(0))
    v = _ln(_gelu(proj(1)), lnvg_ref[...], lnvb_ref[...])
    v_ref[...] = v.reshape(nt, SB, D_MODEL)
    s_rows = []
    for i in range(nt):
        acc = jnp.zeros((SB, D_MODEL), _F32) + bcs_ref[i:i + 1, :]
        for j in range(i + 1):
            acc = acc + wcs_ref[i * nt + j:i * nt + j + 1, :] * v[j * SB:(j + 1) * SB, :]
        s_rows.append(acc)
    s = jnp.concatenate(s_rows, axis=0)
    y_a = _dot((u * s).astype(_BF16), wa_ref[...])

    glu = proj(2) * _sigmoid(proj(3))
    glu_ref[...] = glu.reshape(nt, SB, D_MODEL)
    hist = CONV_WIDTH - 1
    conv_rows = []
    for i in range(nt):
        acc = jnp.zeros((SB, D_MODEL), _F32) + cb_ref[...]
        for k in range(CONV_WIDTH):
            p = i + k
            src = cache_ref[p] if p < hist else glu[(p - hist) * SB:(p - hist + 1) * SB, :]
            acc = acc + cw_ref[k:k + 1, :] * src
        conv_rows.append(acc)
    cn = _ln(jnp.concatenate(conv_rows, axis=0), lncg_ref[...], lncb_ref[...])
    y_b = _dot((cn * _sigmoid(cn)).astype(_BF16), wb_ref[...]) + bb_ref[...]

    merged = (_sigmoid(proj(4) + bga_ref[...]) * y_a + _sigmoid(proj(5) + bgb_ref[...]) * y_b)
    x1 = x + _dot(merged.astype(_BF16), wout_ref[...])
    x1_ref[...] = x1.reshape(nt, SB, D_MODEL)

    h2 = _rms(x1, gffn_ref[...])
    for i in range(nt):
        _store_row_tiles(h2_ref, (i,), SB, h2[i * SB:(i + 1) * SB, :])
    meta_i, meta_f = _route(h2, wr_ref, br_ref, carry)
    mi_ref[...] = meta_i.reshape(nt, SB, LANES)
    mf_ref[...] = meta_f.reshape(nt, SB, LANES)
    cnt_ref[...] = carry[...]


def _mixer_sample(x_t, cache_t, cnt0, w):
    nt, nseq, d = x_t.shape
    hist = cache_t.shape[0]
    weights = (w["gmix"], w["win"], w["lnvg"], w["lnvb"], w["wcs"], w["bcs"], w["wa"], w["cw"],
               w["cb"], w["lncg"], w["lncb"], w["wb"], w["bb"], w["bga"], w["bgb"], w["wout"],
               w["gffn"], w["wr"], w["br"])
    seqs = lambda i: (0, i, 0)
    return pl.pallas_call(
        _mixer_sample_kernel,
        grid=(nseq // SB,),
        in_specs=[pl.BlockSpec((nt, SB, d), seqs), pl.BlockSpec((hist, SB, d), seqs),
                  _full(cnt0.shape)] + [_full(a.shape) for a in weights],
        out_specs=[
            pl.BlockSpec((nt, SB, d), seqs),
            pl.BlockSpec((nt, SB * ROW_TILES, LANES), seqs),
            pl.BlockSpec((nt, SB, LANES), seqs),
            pl.BlockSpec((nt, SB, LANES), seqs),
            _full((SUBLANES, LANES)),
            pl.BlockSpec((nt, SB, d), seqs),
            pl.BlockSpec((nt, SB, d), seqs),
        ],
        out_shape=[
            jax.ShapeDtypeStruct((nt, nseq, d), _F32),
            jax.ShapeDtypeStruct((nt, nseq * ROW_TILES, LANES), _F32),
            jax.ShapeDtypeStruct((nt, nseq, LANES), jnp.int32),
            jax.ShapeDtypeStruct((nt, nseq, LANES), _F32),
            jax.ShapeDtypeStruct((SUBLANES, LANES), _F32),
            jax.ShapeDtypeStruct((nt, nseq, d), _F32),
            jax.ShapeDtypeStruct((nt, nseq, d), _F32),
        ],
        scratch_shapes=[pltpu.VMEM((SUBLANES, LANES), _F32)],
        compiler_params=pltpu.CompilerParams(
            dimension_semantics=("arbitrary",), vmem_limit_bytes=VMEM_LIMIT),
        name="mixer_sample",
    )(x_t, cache_t, cnt0, *weights)


def _row_tile(ref, row):
    return ref.at[pl.ds(pl.multiple_of(row * ROW_TILES, ROW_TILES), ROW_TILES), :]


def _dispatch_kernel(n_first, n_blk, pad_start_ref, pad_len_ref, nused_ref, dest_ref, h2a_ref,
                     h2b_ref, xs_hbm, zbuf, sem, zsem):
    i = pl.program_id(0)

    def fill(start):
        def go(copy):
            copy.start() if start else copy.wait()

        def per_expert(e, carry):
            first, length = pad_start_ref[e], pad_len_ref[e]
            for s in range(1, BLK.bit_length()):
                p = BLK >> s

                @pl.when((length & p) != 0)
                def _():
                    row = first + (length & (-2 * p))
                    go(pltpu.make_async_copy(
                        zbuf.at[pl.ds(0, p * ROW_TILES), :],
                        xs_hbm.at[pl.ds(pl.multiple_of(row * ROW_TILES, ROW_TILES), p * ROW_TILES), :],
                        zsem))
            return carry

        def per_tail_block(b, carry):
            go(pltpu.make_async_copy(
                zbuf, xs_hbm.at[pl.ds(pl.multiple_of(b * BLK * ROW_TILES, ROW_TILES),
                                      BLK * ROW_TILES), :], zsem))
            return carry

        lax.fori_loop(0, N_EXPERTS, per_expert, 0)
        lax.fori_loop(nused_ref[0], n_blk, per_tail_block, 0)

    @pl.when(i == 0)
    def _():
        zbuf[...] = jnp.zeros_like(zbuf)
        fill(True)

    def scatter(h2_ref):
        def issue(t, carry):
            src = _row_tile(h2_ref, t)
            for k in range(TOP_K):
                pltpu.make_async_copy(src, _row_tile(xs_hbm, dest_ref[0, 0, t * TOP_K + k]),
                                      sem).start(priority=k % 2)
            return carry

        lax.fori_loop(0, TT, issue, 0)
        for _ in range(TOP_K):
            pltpu.make_async_copy(h2_ref, xs_hbm.at[pl.ds(0, TT * ROW_TILES), :], sem).wait()

    @pl.when(i < n_first)
    def _():
        scatter(h2a_ref)

    @pl.when(i >= n_first)
    def _():
        scatter(h2b_ref)

    @pl.when(i == 0)
    def _():
        fill(False)


def _dispatch(pad_start, pad_len, nused, dest, h2_a, h2_b, n_blk):
    n_tiles = dest.shape[0]
    n_first = h2_a.shape[0] // (TT * ROW_TILES)
    tile = (TT * ROW_TILES, LANES)
    grid_spec = pltpu.PrefetchScalarGridSpec(
        num_scalar_prefetch=3,
        grid=(n_tiles,),
        in_specs=[
            pl.BlockSpec((1, 1, TT * TOP_K), lambda i, *_: (i, 0, 0), memory_space=pltpu.SMEM),
            pl.BlockSpec(tile, lambda i, *_: (jnp.minimum(i, n_first - 1), 0)),
            pl.BlockSpec(tile, lambda i, *_: (jnp.maximum(i - n_first, 0), 0)),
        ],
        out_specs=pl.BlockSpec(memory_space=pl.ANY),
        scratch_shapes=[pltpu.VMEM((BLK * ROW_TILES, LANES), _F32), pltpu.SemaphoreType.DMA,
                        pltpu.SemaphoreType.DMA],
    )
    return pl.pallas_call(
        functools.partial(_dispatch_kernel, n_first, n_blk),
        grid_spec=grid_spec,
        out_shape=jax.ShapeDtypeStruct((n_blk * BLK * ROW_TILES, LANES), _F32),
        compiler_params=pltpu.CompilerParams(
            dimension_semantics=("arbitrary",), has_side_effects=True),
        name="dispatch",
    )(pad_start, pad_len, nused, dest, h2_a, h2_b)


def _experts_kernel(be_ref, nused_ref, xs_ref, wup_ref, bup_ref, wdn_ref, bdn_ref, out_ref,
                    wup_bf, wdn_bf):
    i = pl.program_id(0)
    prev = be_ref[jnp.maximum(i - 1, 0)]

    @pl.when((i == 0) | (be_ref[i] != prev))
    def _():
        rows = 128

        def cast(r, carry):
            sl = pl.ds(pl.multiple_of(r * rows, rows), rows)
            wup_bf[sl, :] = wup_ref[sl, :].astype(_BF16)
            wdn_bf[sl, :] = wdn_ref[sl, :].astype(_BF16)
            return carry

        lax.fori_loop(0, D_MODEL // rows, cast, 0)

    @pl.when(i < nused_ref[0])
    def _():
        xb = _load_row_tiles(xs_ref, 0, BLK).astype(_BF16)
        gate = _dot(xb, wup_bf[:, :D_EXPERT]) + bup_ref[:, :D_EXPERT]
        up = _dot(xb, wup_bf[:, D_EXPERT:]) + bup_ref[:, D_EXPERT:]
        gate = jnp.minimum(gate, SWIGLU_LIMIT)
        up = jnp.clip(up, -SWIGLU_LIMIT, SWIGLU_LIMIT)
        act = (up + 1.0) * (gate * _sigmoid(SWIGLU_ALPHA * gate))
        o = _dot(act.astype(_BF16), wdn_bf[...]) + bdn_ref[...]
        _store_row_tiles(out_ref, (), BLK, o)

    @pl.when(i >= nused_ref[0])
    def _():
        out_ref[...] = jnp.zeros_like(out_ref)


def _experts(block_expert, nused, xs, w_up, b_up, w_down, b_down):
    n_blk = block_expert.shape[0]
    rows = lambda i, be, nu: (jnp.minimum(i, nu[0] - 1), 0)
    expert = lambda i, be, nu: (be[i], 0, 0)
    d, f2 = w_up.shape[1], w_up.shape[2]
    grid_spec = pltpu.PrefetchScalarGridSpec(
        num_scalar_prefetch=2,
        grid=(n_blk,),
        in_specs=[
            pl.BlockSpec((BLK * ROW_TILES, LANES), rows),
            pl.BlockSpec((None, d, f2), expert),
            pl.BlockSpec((None, 1, f2), expert),
            pl.BlockSpec((None, w_down.shape[1], d), expert),
            pl.BlockSpec((None, 1, d), expert),
        ],
        out_specs=pl.BlockSpec((BLK * ROW_TILES, LANES), lambda i, be, nu: (i, 0)),
        scratch_shapes=[pltpu.VMEM((d, f2), _BF16), pltpu.VMEM((w_down.shape[1], d), _BF16)],
    )
    return pl.pallas_call(
        _experts_kernel,
        grid_spec=grid_spec,
        out_shape=jax.ShapeDtypeStruct(xs.shape, _F32),
        compiler_params=pltpu.CompilerParams(
            dimension_semantics=("arbitrary",), vmem_limit_bytes=VMEM_LIMIT),
        name="experts",
    )(block_expert, nused, xs, w_up, b_up, w_down, b_down)


def _combine_kernel(n, dest_ref, dest_next_ref, x1_ref, mf_ref, gfin_ref, outs_hbm, y_ref, buf,
                    sems):
    i = pl.program_id(0)

    def gather(d_ref, slot):
        def issue(t, carry):
            for k in range(TOP_K):
                dst = buf.at[slot, pl.ds(pl.multiple_of((k * TT + t) * ROW_TILES, ROW_TILES),
                                         ROW_TILES), :]
                pltpu.make_async_copy(_row_tile(outs_hbm, d_ref[0, 0, t * TOP_K + k]), dst,
                                      sems.at[slot]).start(priority=k % 2)
            return carry

        lax.fori_loop(0, TT, issue, 0)

    slot = i % 2

    @pl.when(i == 0)
    def _():
        gather(dest_ref, 0)

    @pl.when(i + 1 < n)
    def _():
        gather(dest_next_ref, 1 - slot)

    pltpu.make_async_copy(outs_hbm.at[pl.ds(0, buf.shape[1]), :], buf.at[slot], sems.at[slot]).wait()

    y = x1_ref[...]
    gates = mf_ref[...]
    rows = buf.at[slot]
    for k in range(TOP_K):
        y = y + gates[:, k:k + 1] * _load_row_tiles(rows, k * TT * ROW_TILES, TT)
    y_ref[...] = _rms(y, gfin_ref[...])


def _combine(dest, x1, meta_f, g_final, outs):
    n_tok, d = x1.shape
    n_tiles = n_tok // TT
    tok = lambda i: (i, 0)
    return pl.pallas_call(
        functools.partial(_combine_kernel, n_tiles),
        grid=(n_tiles,),
        in_specs=[
            pl.BlockSpec((1, 1, TT * TOP_K), lambda i: (i, 0, 0), memory_space=pltpu.SMEM),
            pl.BlockSpec((1, 1, TT * TOP_K), lambda i: (jnp.minimum(i + 1, n_tiles - 1), 0, 0),
                         memory_space=pltpu.SMEM),
            pl.BlockSpec((TT, d), tok),
            pl.BlockSpec((TT, LANES), tok),
            _full(g_final.shape),
            pl.BlockSpec(memory_space=pl.ANY),
        ],
        out_specs=pl.BlockSpec((TT, d), tok),
        out_shape=jax.ShapeDtypeStruct((n_tok, d), _F32),
        scratch_shapes=[pltpu.VMEM((2, TOP_K * TT * ROW_TILES, LANES), _F32),
                        pltpu.SemaphoreType.DMA((2,))],
        compiler_params=pltpu.CompilerParams(
            dimension_semantics=("arbitrary",), vmem_limit_bytes=VMEM_LIMIT),
        name="combine",
    )(dest, dest, x1, meta_f, g_final, outs)


def _prep_weights(norm_mix_g, w_in, ln_v_g, ln_v_b, w_spatial, b_spatial, w_branch_a, conv_w,
                  conv_b, ln_conv_g, ln_conv_b, w_branch_b, b_branch_b, b_gate_a, b_gate_b, w_out,
                  norm_ffn_g, w_router, b_router, n_new):
    row = lambda a: a.reshape(1, -1)
    pad = LANES - N_EXPERTS
    wr = jnp.pad(w_router, ((0, 0), (0, pad)))
    wr_hi = wr.astype(_BF16)
    wr_lo = (wr - wr_hi.astype(_F32)).astype(_BF16)
    wcs = jnp.repeat(w_spatial[:, :n_new, :n_new].transpose(1, 2, 0).reshape(n_new * n_new, A_GROUPS),
                     LANES, axis=1)
    win = w_in.astype(_BF16)
    d = D_MODEL
    wrest = jnp.concatenate([win[:, :2 * d], win[:, 4 * d:]], axis=1)
    wrest = wrest.reshape(d, N_COL, 4 * d // N_COL).transpose(1, 0, 2)
    return dict(
        wglu=win[:, 2 * d:4 * d], wrest=wrest,
        cw3=conv_w.reshape(CONV_WIDTH, N_COL, LANES).transpose(1, 0, 2),
        cb3=conv_b.reshape(N_COL, 1, LANES),
        gmix=row(norm_mix_g), win=win, lnvg=row(ln_v_g), lnvb=row(ln_v_b),
        wsp=w_spatial, bsp=jnp.repeat(b_spatial.T, LANES, axis=1), wcs=wcs,
        bcs=jnp.repeat(b_spatial[:, :n_new].T, LANES, axis=1),
        wa=w_branch_a.astype(_BF16), cw=conv_w, cb=row(conv_b), lncg=row(ln_conv_g),
        lncb=row(ln_conv_b), wb=w_branch_b.astype(_BF16), bb=row(b_branch_b), bga=row(b_gate_a),
        bgb=row(b_gate_b), wout=w_out.astype(_BF16), gffn=row(norm_ffn_g),
        wr=jnp.concatenate([wr_hi, wr_lo], axis=1), br=jnp.pad(row(b_router), ((0, 0), (0, pad))))


def kernel(x_prompt, x_sample, cache_conv, norm_mix_g, w_in, ln_v_g, ln_v_b, w_spatial, b_spatial, w_branch_a, conv_w, conv_b, ln_conv_g, ln_conv_b, w_branch_b, b_branch_b, b_gate_a, b_gate_b, w_out, norm_ffn_g, w_router, b_router, w_up, b_up, w_down, b_down, norm_final_g):
    assert w_in.shape[0] == 1, "single trunk layer"
    bsz, seq, d = x_prompt.shape
    nseq, n_new, _ = x_sample.shape
    assert d == D_MODEL and seq % TT == 0 and TT % CHUNK == 0 and nseq % SB == 0
    assert n_new <= CHUNK and (nseq * n_new) % TT == 0
    w = _prep_weights(norm_mix_g[0], w_in[0], ln_v_g[0], ln_v_b[0], w_spatial[0], b_spatial[0],
                      w_branch_a[0], conv_w[0], conv_b[0], ln_conv_g[0], ln_conv_b[0],
                      w_branch_b[0], b_branch_b[0], b_gate_a[0], b_gate_b[0], w_out[0],
                      norm_ffn_g[0], w_router[0], b_router[0], n_new)

    x1_p, h2_p, mi_p, mf_p, cnt_p, cst_p = _mixer_prompt(x_prompt, w)
    x_t = x_sample.transpose(1, 0, 2)
    cache_t = cache_conv[0].transpose(1, 0, 2)
    x1_s, h2_s, mi_s, mf_s, cnt, glu_s, v_s = _mixer_sample(x_t, cache_t, cnt_p, w)
    n_p, n_s = bsz * seq, nseq * n_new
    x1_s = x1_s.reshape(n_s, d)
    h2_s = h2_s.reshape(n_s * ROW_TILES, LANES)
    mi_s = mi_s.reshape(n_s, LANES)
    mf_s = mf_s.reshape(n_s, LANES)

    counts = cnt[0, :N_EXPERTS].astype(jnp.int32)
    padded = ((counts + BLK - 1) // BLK) * BLK
    pend = jnp.cumsum(padded)
    pstart = pend - padded
    n_assign = (n_p + n_s) * TOP_K
    n_blk = -(-(n_assign + N_EXPERTS * (BLK - 1)) // BLK)
    nused = (pend[-1] // BLK).astype(jnp.int32)
    blk_ids = jnp.arange(n_blk, dtype=jnp.int32)
    first_row = jnp.minimum(blk_ids, nused - 1) * BLK
    be = jnp.sum((pend[None, :] <= first_row[:, None]).astype(jnp.int32), axis=1)
    be = jnp.minimum(be, N_EXPERTS - 1)
    experts_row = jnp.arange(N_EXPERTS, dtype=jnp.int32)

    def dest_of(mi):
        sel = mi[:, :TOP_K, None] == experts_row[None, None, :]
        dest = jnp.sum(jnp.where(sel, pstart[None, None, :], 0), axis=-1) + mi[:, TOP_K:2 * TOP_K]
        return dest.reshape(-1, 1, TT * TOP_K)

    dest_p, dest_s = dest_of(mi_p), dest_of(mi_s)
    xs = _dispatch(pstart + counts, padded - counts, nused.reshape(1),
                   jnp.concatenate([dest_p, dest_s], axis=0), h2_p, h2_s, n_blk)
    outs = _experts(be, nused.reshape(1), xs, w_up[0], b_up[0][:, None, :], w_down[0],
                    b_down[0][:, None, :])
    g_final = norm_final_g.reshape(1, d)
    y_p = _combine(dest_p, x1_p, mf_p, g_final, outs)
    y_s = _combine(dest_s, x1_s, mf_s, g_final, outs)

    y_prompt = y_p.reshape(bsz, seq, d)
    y_sample = y_s.reshape(n_new, nseq, d).transpose(1, 0, 2)
    conv_state_prompt = cst_p[None]
    glu_new = glu_s.transpose(1, 0, 2)
    conv_state_sample = jnp.concatenate([cache_conv[0][:, n_new:], glu_new], axis=1)[None]
    chunk_v_sample = v_s.transpose(1, 0, 2)[None]
    return (y_prompt, y_sample, conv_state_prompt, conv_state_sample, chunk_v_sample)
```

```python
import functools

import jax
import jax.numpy as jnp
from jax import lax
from jax.experimental import pallas as pl
from jax.experimental.pallas import tpu as pltpu

D_MODEL = 1024
CHUNK = 128
A_GROUPS = 8
CONV_WIDTH = 31
HIST = 32
N_EXPERTS = 32
TOP_K = 4
D_EXPERT = 1024
SWIGLU_LIMIT = 7.0
SWIGLU_ALPHA = 1.702
EPS = 1e-5
LANES = 128
SUBLANES = 8
ROW_TILES = D_MODEL // LANES
TT = 256
CONV_ROWS = 64
SB = 32
BLK = 512
SUB = 256
VMEM_LIMIT = 56 * 1024 * 1024

_F32 = jnp.float32
_BF16 = jnp.bfloat16
_INV_SQRT2 = 0.7071067811865476


def _rms(x, g):
    return x * lax.rsqrt(jnp.mean(x * x, axis=-1, keepdims=True) + EPS) * g


def _ln(x, g, b):
    mu = jnp.mean(x, axis=-1, keepdims=True)
    xc = x - mu
    var = jnp.mean(xc * xc, axis=-1, keepdims=True)
    return xc * lax.rsqrt(var + EPS) * g + b


def _gelu(x):
    return 0.5 * x * (1.0 + lax.erf(x * _INV_SQRT2))


def _sigmoid(x):
    return 1.0 / (1.0 + jnp.exp(-x))


def _dot(a, b):
    return jnp.dot(a, b, preferred_element_type=_F32)


def _store_row_tiles(ref, lead, n, val, start=0):
    for j in range(ROW_TILES):
        ref[lead + (pl.ds(start + j, n, stride=ROW_TILES), slice(None))] = (
            val[:, j * LANES:(j + 1) * LANES])


def _load_row_tiles(ref, start, n):
    return jnp.concatenate(
        [ref[pl.ds(start + j, n, stride=ROW_TILES), :] for j in range(ROW_TILES)], axis=1)


def _route(h2, wr_ref, br_ref, carry_ref):
    n = h2.shape[0]
    h_hi = h2.astype(_BF16)
    h_lo = (h2 - h_hi.astype(_F32)).astype(_BF16)
    by_hi = _dot(h_hi, wr_ref[...])
    logits = (by_hi[:, :LANES] + by_hi[:, LANES:] + _dot(h_lo, wr_ref[:, :LANES])) + br_ref[...]
    lane = lax.broadcasted_iota(jnp.int32, (n, LANES), 1)
    l = jnp.where(lane < N_EXPERTS, logits, -jnp.inf)
    onehots, vals, idxs = [], [], []
    for _ in range(TOP_K):
        m = jnp.max(l, axis=-1, keepdims=True)
        idx = jnp.min(jnp.where(l == m, lane, LANES), axis=-1, keepdims=True)
        oh = lane == idx
        onehots.append(oh)
        vals.append(m)
        idxs.append(idx)
        l = jnp.where(oh, -jnp.inf, l)
    exps = [jnp.exp(v - vals[0]) for v in vals]
    denom = exps[0] + exps[1] + exps[2] + exps[3]
    gates = [e / denom for e in exps]

    oh_all = jnp.zeros((n, LANES), _F32)
    for oh in onehots:
        oh_all = oh_all + jnp.where(oh, 1.0, 0.0)
    ri = lax.broadcasted_iota(jnp.int32, (n, n), 0)
    ci = lax.broadcasted_iota(jnp.int32, (n, n), 1)
    tri = jnp.where(ci < ri, 1.0, 0.0).astype(_BF16)
    base = _dot(tri, oh_all.astype(_BF16)) + carry_ref[0:1, :]
    ranks = [jnp.sum(jnp.where(oh, base, 0.0), axis=-1, keepdims=True) for oh in onehots]
    carry_ref[...] = carry_ref[...] + jnp.sum(oh_all, axis=0, keepdims=True)

    meta_i = jnp.zeros((n, LANES), jnp.int32)
    meta_f = jnp.zeros((n, LANES), _F32)
    for k in range(TOP_K):
        meta_i = jnp.where(lane == k, idxs[k], meta_i)
        meta_i = jnp.where(lane == TOP_K + k, ranks[k].astype(jnp.int32), meta_i)
        meta_f = jnp.where(lane == k, gates[k], meta_f)
    return meta_i, meta_f


def _mixer_prompt_kernel(x_ref, gmix_ref, win_ref, lnvg_ref, lnvb_ref, wsp_ref, bsp_ref, wa_ref,
                         cw_ref, cb_ref, lncg_ref, lncb_ref, wb_ref, bb_ref, bga_ref, bgb_ref,
                         wout_ref, gffn_ref, wr_ref, br_ref,
                         x1_ref, h2_ref, mi_ref, mf_ref, cnt_ref, cst_ref,
                         cbuf, carry):
    b = pl.program_id(0)
    t = pl.program_id(1)

    @pl.when((b == 0) & (t == 0))
    def _():
        carry[...] = jnp.zeros_like(carry)

    @pl.when(t == 0)
    def _():
        cbuf[0:HIST, :] = jnp.zeros((HIST, D_MODEL), _F32)
        cbuf[HIST + TT:, :] = jnp.zeros((SUBLANES, D_MODEL), _F32)

    x = x_ref[...]
    hb = _rms(x, gmix_ref[...]).astype(_BF16)

    def proj(s):
        return _dot(hb, win_ref[:, s * D_MODEL:(s + 1) * D_MODEL])

    glu = proj(2) * _sigmoid(proj(3))
    cbuf[HIST:HIST + TT, :] = glu
    off = HIST - (CONV_WIDTH - 1)
    span = HIST + CONV_ROWS + SUBLANES
    conv_cols = []
    other, half, pieces = (0, 1, 4, 5), D_MODEL // 2, []
    for c in range(D_MODEL // LANES):
        cols = slice(c * LANES, (c + 1) * LANES)
        accs = []
        for r in range(TT // CONV_ROWS):
            acc = jnp.zeros((CONV_ROWS, LANES), _F32) + cb_ref[:, cols]
            window = cbuf[r * CONV_ROWS:r * CONV_ROWS + span, cols]
            for sh in range(SUBLANES):
                shifted = window if sh == 0 else pltpu.roll(window, span - sh, axis=0)
                for a in range(HIST // SUBLANES + 1):
                    k = a * SUBLANES + sh - off
                    if 0 <= k < CONV_WIDTH:
                        acc = acc + (cw_ref[k:k + 1, cols]
                                     * shifted[a * SUBLANES:a * SUBLANES + CONV_ROWS, :])
            accs.append(acc)
        conv_cols.append(jnp.concatenate(accs, axis=0))
        lo = other[c // 2] * D_MODEL + (c % 2) * half
        pieces.append(_dot(hb, win_ref[:, lo:lo + half]))
    conv = jnp.concatenate(conv_cols, axis=1)
    p_u, p_v, p_ga, p_gb = [jnp.concatenate(pieces[2 * i:2 * i + 2], axis=1) for i in range(4)]
    cbuf[0:HIST, :] = cbuf[TT:TT + HIST, :]
    cn = _ln(conv, lncg_ref[...], lncb_ref[...])
    y_b = _dot((cn * _sigmoid(cn)).astype(_BF16), wb_ref[...]) + bb_ref[...]

    u = _gelu(p_u)
    v = _ln(_gelu(p_v), lnvg_ref[...], lnvb_ref[...])
    ri = lax.broadcasted_iota(jnp.int32, (CHUNK, CHUNK), 0)
    ci = lax.broadcasted_iota(jnp.int32, (CHUNK, CHUNK), 1)
    s_rows = []
    for c in range(TT // CHUNK):
        s_cols = []
        for g in range(A_GROUPS):
            wm = jnp.where(ci <= ri, wsp_ref[g], 0.0).astype(_BF16)
            vb = v[c * CHUNK:(c + 1) * CHUNK, g * LANES:(g + 1) * LANES].astype(_BF16)
            s_cols.append(_dot(wm, vb))
        s_rows.append(jnp.concatenate(s_cols, axis=1) + bsp_ref[...])
    s = jnp.concatenate(s_rows, axis=0)
    y_a = _dot((u * s).astype(_BF16), wa_ref[...])

    merged = _sigmoid(p_ga + bga_ref[...]) * y_a + _sigmoid(p_gb + bgb_ref[...]) * y_b
    x1 = x + _dot(merged.astype(_BF16), wout_ref[...])
    x1_ref[...] = x1

    h2 = _rms(x1, gffn_ref[...])
    _store_row_tiles(h2_ref, (), TT, h2)
    meta_i, meta_f = _route(h2, wr_ref, br_ref, carry)
    mi_ref[...] = meta_i
    mf_ref[...] = meta_f
    cnt_ref[...] = carry[...]

    @pl.when(t == pl.num_programs(1) - 1)
    def _():
        cst_ref[...] = cbuf[HIST - (CONV_WIDTH - 1):HIST, :]


def _full(shape):
    return pl.BlockSpec(shape, lambda *_: (0,) * len(shape))


def _mixer_prompt(x, w):
    bsz, seq, d = x.shape
    n_tok = bsz * seq
    nt = seq // TT
    tok = lambda b, t: (b * nt + t, 0)
    weights = (w["gmix"], w["win"], w["lnvg"], w["lnvb"], w["wsp"], w["bsp"], w["wa"], w["cw"],
               w["cb"], w["lncg"], w["lncb"], w["wb"], w["bb"], w["bga"], w["bgb"], w["wout"],
               w["gffn"], w["wr"], w["br"])
    return pl.pallas_call(
        _mixer_prompt_kernel,
        grid=(bsz, nt),
        in_specs=[pl.BlockSpec((None, TT, d), lambda b, t: (b, t, 0))]
                 + [_full(a.shape) for a in weights],
        out_specs=[
            pl.BlockSpec((TT, d), tok),
            pl.BlockSpec((TT * ROW_TILES, LANES), tok),
            pl.BlockSpec((TT, LANES), tok),
            pl.BlockSpec((TT, LANES), tok),
            _full((SUBLANES, LANES)),
            pl.BlockSpec((None, CONV_WIDTH - 1, d), lambda b, t: (b, 0, 0)),
        ],
        out_shape=[
            jax.ShapeDtypeStruct((n_tok, d), _F32),
            jax.ShapeDtypeStruct((n_tok * ROW_TILES, LANES), _F32),
            jax.ShapeDtypeStruct((n_tok, LANES), jnp.int32),
            jax.ShapeDtypeStruct((n_tok, LANES), _F32),
            jax.ShapeDtypeStruct((SUBLANES, LANES), _F32),
            jax.ShapeDtypeStruct((bsz, CONV_WIDTH - 1, d), _F32),
        ],
        scratch_shapes=[pltpu.VMEM((HIST + TT + SUBLANES, d), _F32),
                        pltpu.VMEM((SUBLANES, LANES), _F32)],
        compiler_params=pltpu.CompilerParams(
            dimension_semantics=("arbitrary", "arbitrary"), vmem_limit_bytes=VMEM_LIMIT),
        name="mixer_prompt",
    )(x, *weights)


def _mixer_sample_kernel(x_ref, cache_ref, cnt0_ref, gmix_ref, win_ref, lnvg_ref, lnvb_ref,
                         wcs_ref, bcs_ref, wa_ref, cw_ref, cb_ref, lncg_ref, lncb_ref, wb_ref,
                         bb_ref, bga_ref, bgb_ref, wout_ref, gffn_ref, wr_ref, br_ref,
                         x1_ref, h2_ref, mi_ref, mf_ref, cnt_ref, glu_ref, v_ref,
                         carry):
    nt = x_ref.shape[0]
    n = nt * SB

    @pl.when(pl.program_id(0) == 0)
    def _():
        carry[...] = cnt0_ref[...]

    x = x_ref[...].reshape(n, D_MODEL)
    hb = _rms(x, gmix_ref[...]).astype(_BF16)

    def proj(s):
        return _dot(hb, win_ref[:, s * D_MODEL:(s + 1) * D_MODEL])

    u = _gelu(proj(0))
    v = _ln(_gelu(proj(1)), lnvg_ref[...], lnvb_ref[...])
    v_ref[...] = v.reshape(nt, SB, D_MODEL)
    s_rows = []
    for i in range(nt):
        acc = jnp.zeros((SB, D_MODEL), _F32) + bcs_ref[i:i + 1, :]
        for j in range(i + 1):
            acc = acc + wcs_ref[i * nt + j:i * nt + j + 1, :] * v[j * SB:(j + 1) * SB, :]
        s_rows.append(acc)
    s = jnp.concatenate(s_rows, axis=0)
    y_a = _dot((u * s).astype(_BF16), wa_ref[...])

    glu = proj(2) * _sigmoid(proj(3))
    glu_ref[...] = glu.reshape(nt, SB, D_MODEL)
    hist = CONV_WIDTH - 1
    conv_rows = []
    for i in range(nt):
        acc = jnp.zeros((SB, D_MODEL), _F32) + cb_ref[...]
        for k in range(CONV_WIDTH):
            p = i + k
            src = cache_ref[p] if p < hist else glu[(p - hist) * SB:(p - hist + 1) * SB, :]
            acc = acc + cw_ref[k:k + 1, :] * src
        conv_rows.append(acc)
    cn = _ln(jnp.concatenate(conv_rows, axis=0), lncg_ref[...], lncb_ref[...])
    y_b = _dot((cn * _sigmoid(cn)).astype(_BF16), wb_ref[...]) + bb_ref[...]

    merged = (_sigmoid(proj(4) + bga_ref[...]) * y_a + _sigmoid(proj(5) + bgb_ref[...]) * y_b)
    x1 = x + _dot(merged.astype(_BF16), wout_ref[...])
    x1_ref[...] = x1.reshape(nt, SB, D_MODEL)

    h2 = _rms(x1, gffn_ref[...])
    for i in range(nt):
        _store_row_tiles(h2_ref, (i,), SB, h2[i * SB:(i + 1) * SB, :])
    meta_i, meta_f = _route(h2, wr_ref, br_ref, carry)
    mi_ref[...] = meta_i.reshape(nt, SB, LANES)
    mf_ref[...] = meta_f.reshape(nt, SB, LANES)
    cnt_ref[...] = carry[...]


def _mixer_sample(x_t, cache_t, cnt0, w):
    nt, nseq, d = x_t.shape
    hist = cache_t.shape[0]
    weights = (w["gmix"], w["win"], w["lnvg"], w["lnvb"], w["wcs"], w["bcs"], w["wa"], w["cw"],
               w["cb"], w["lncg"], w["lncb"], w["wb"], w["bb"], w["bga"], w["bgb"], w["wout"],
               w["gffn"], w["wr"], w["br"])
    seqs = lambda i: (0, i, 0)
    return pl.pallas_call(
        _mixer_sample_kernel,
        grid=(nseq // SB,),
        in_specs=[pl.BlockSpec((nt, SB, d), seqs), pl.BlockSpec((hist, SB, d), seqs),
                  _full(cnt0.shape)] + [_full(a.shape) for a in weights],
        out_specs=[
            pl.BlockSpec((nt, SB, d), seqs),
            pl.BlockSpec((nt, SB * ROW_TILES, LANES), seqs),
            pl.BlockSpec((nt, SB, LANES), seqs),
            pl.BlockSpec((nt, SB, LANES), seqs),
            _full((SUBLANES, LANES)),
            pl.BlockSpec((nt, SB, d), seqs),
            pl.BlockSpec((nt, SB, d), seqs),
        ],
        out_shape=[
            jax.ShapeDtypeStruct((nt, nseq, d), _F32),
            jax.ShapeDtypeStruct((nt, nseq * ROW_TILES, LANES), _F32),
            jax.ShapeDtypeStruct((nt, nseq, LANES), jnp.int32),
            jax.ShapeDtypeStruct((nt, nseq, LANES), _F32),
            jax.ShapeDtypeStruct((SUBLANES, LANES), _F32),
            jax.ShapeDtypeStruct((nt, nseq, d), _F32),
            jax.ShapeDtypeStruct((nt, nseq, d), _F32),
        ],
        scratch_shapes=[pltpu.VMEM((SUBLANES, LANES), _F32)],
        compiler_params=pltpu.CompilerParams(
            dimension_semantics=("arbitrary",), vmem_limit_bytes=VMEM_LIMIT),
        name="mixer_sample",
    )(x_t, cache_t, cnt0, *weights)


def _row_tile(ref, row):
    return ref.at[pl.ds(pl.multiple_of(row * ROW_TILES, ROW_TILES), ROW_TILES), :]


def _dispatch_kernel(n_first, n_blk, pad_start_ref, pad_len_ref, nused_ref, dest_ref, h2a_ref,
                     h2b_ref, xs_hbm, zbuf, sem, zsem):
    i = pl.program_id(0)

    def fill(start):
        def go(copy):
            copy.start() if start else copy.wait()

        def per_expert(e, carry):
            first, length = pad_start_ref[e], pad_len_ref[e]
            for s in range(1, BLK.bit_length()):
                p = BLK >> s

                @pl.when((length & p) != 0)
                def _():
                    row = first + (length & (-2 * p))
                    go(pltpu.make_async_copy(
                        zbuf.at[pl.ds(0, p * ROW_TILES), :],
                        xs_hbm.at[pl.ds(pl.multiple_of(row * ROW_TILES, ROW_TILES), p * ROW_TILES), :],
                        zsem))
            return carry

        def per_tail_block(b, carry):
            go(pltpu.make_async_copy(
                zbuf, xs_hbm.at[pl.ds(pl.multiple_of(b * BLK * ROW_TILES, ROW_TILES),
                                      BLK * ROW_TILES), :], zsem))
            return carry

        lax.fori_loop(0, N_EXPERTS, per_expert, 0)
        lax.fori_loop(nused_ref[0], n_blk, per_tail_block, 0)

    @pl.when(i == 0)
    def _():
        zbuf[...] = jnp.zeros_like(zbuf)
        fill(True)

    def scatter(h2_ref):
        def issue(t, carry):
            src = _row_tile(h2_ref, t)
            for k in range(TOP_K):
                pltpu.make_async_copy(src, _row_tile(xs_hbm, dest_ref[0, 0, t * TOP_K + k]),
                                      sem).start(priority=k % 2)
            return carry

        lax.fori_loop(0, TT, issue, 0)
        for _ in range(TOP_K):
            pltpu.make_async_copy(h2_ref, xs_hbm.at[pl.ds(0, TT * ROW_TILES), :], sem).wait()

    @pl.when(i < n_first)
    def _():
        scatter(h2a_ref)

    @pl.when(i >= n_first)
    def _():
        scatter(h2b_ref)

    @pl.when(i == 0)
    def _():
        fill(False)


def _dispatch(pad_start, pad_len, nused, dest, h2_a, h2_b, n_blk):
    n_tiles = dest.shape[0]
    n_first = h2_a.shape[0] // (TT * ROW_TILES)
    tile = (TT * ROW_TILES, LANES)
    grid_spec = pltpu.PrefetchScalarGridSpec(
        num_scalar_prefetch=3,
        grid=(n_tiles,),
        in_specs=[
            pl.BlockSpec((1, 1, TT * TOP_K), lambda i, *_: (i, 0, 0), memory_space=pltpu.SMEM),
            pl.BlockSpec(tile, lambda i, *_: (jnp.minimum(i, n_first - 1), 0)),
            pl.BlockSpec(tile, lambda i, *_: (jnp.maximum(i - n_first, 0), 0)),
        ],
        out_specs=pl.BlockSpec(memory_space=pl.ANY),
        scratch_shapes=[pltpu.VMEM((BLK * ROW_TILES, LANES), _F32), pltpu.SemaphoreType.DMA,
                        pltpu.SemaphoreType.DMA],
    )
    return pl.pallas_call(
        functools.partial(_dispatch_kernel, n_first, n_blk),
        grid_spec=grid_spec,
        out_shape=jax.ShapeDtypeStruct((n_blk * BLK * ROW_TILES, LANES), _F32),
        compiler_params=pltpu.CompilerParams(
            dimension_semantics=("arbitrary",), has_side_effects=True),
        name="dispatch",
    )(pad_start, pad_len, nused, dest, h2_a, h2_b)


def _experts_kernel(be_ref, nused_ref, nvalid_ref, xs_ref, wup_ref, bup_ref, wdn_ref, bdn_ref,
                    out_ref, wup_bf, wdn_bf):
    del nused_ref
    i = pl.program_id(0)
    prev = be_ref[jnp.maximum(i - 1, 0)]

    @pl.when((i == 0) | (be_ref[i] != prev))
    def _():
        rows = 128

        def cast(r, carry):
            sl = pl.ds(pl.multiple_of(r * rows, rows), rows)
            wup_bf[sl, :] = wup_ref[sl, :].astype(_BF16)
            wdn_bf[sl, :] = wdn_ref[sl, :].astype(_BF16)
            return carry

        lax.fori_loop(0, D_MODEL // rows, cast, 0)

    def rows(h):
        start = h * SUB * ROW_TILES
        xb = _load_row_tiles(xs_ref, start, SUB).astype(_BF16)
        gate = _dot(xb, wup_bf[:, :D_EXPERT]) + bup_ref[:, :D_EXPERT]
        up = _dot(xb, wup_bf[:, D_EXPERT:]) + bup_ref[:, D_EXPERT:]
        gate = jnp.minimum(gate, SWIGLU_LIMIT)
        up = jnp.clip(up, -SWIGLU_LIMIT, SWIGLU_LIMIT)
        act = (up + 1.0) * (gate * _sigmoid(SWIGLU_ALPHA * gate))
        o = _dot(act.astype(_BF16), wdn_bf[...]) + bdn_ref[...]
        _store_row_tiles(out_ref, (), SUB, o, start)

    n_valid = nvalid_ref[i]
    n_sub = BLK // SUB
    for live in range(n_sub + 1):
        lo, hi = (live - 1) * SUB, live * SUB

        @pl.when((n_valid > lo) & (n_valid <= hi) if live else n_valid <= 0)
        def _():
            for h in range(live):
                rows(h)
            if live < n_sub:
                out_ref[live * SUB * ROW_TILES:, :] = jnp.zeros(
                    ((n_sub - live) * SUB * ROW_TILES, LANES), _F32)


def _experts(block_expert, nused, nvalid, xs, w_up, b_up, w_down, b_down):
    n_blk = block_expert.shape[0]
    rows = lambda i, be, nu, nv: (jnp.minimum(i, nu[0] - 1), 0)
    expert = lambda i, be, nu, nv: (be[i], 0, 0)
    d, f2 = w_up.shape[1], w_up.shape[2]
    grid_spec = pltpu.PrefetchScalarGridSpec(
        num_scalar_prefetch=3,
        grid=(n_blk,),
        in_specs=[
            pl.BlockSpec((BLK * ROW_TILES, LANES), rows),
            pl.BlockSpec((None, d, f2), expert),
            pl.BlockSpec((None, 1, f2), expert),
            pl.BlockSpec((None, w_down.shape[1], d), expert),
            pl.BlockSpec((None, 1, d), expert),
        ],
        out_specs=pl.BlockSpec((BLK * ROW_TILES, LANES), lambda i, be, nu, nv: (i, 0)),
        scratch_shapes=[pltpu.VMEM((d, f2), _BF16), pltpu.VMEM((w_down.shape[1], d), _BF16)],
    )
    return pl.pallas_call(
        _experts_kernel,
        grid_spec=grid_spec,
        out_shape=jax.ShapeDtypeStruct(xs.shape, _F32),
        compiler_params=pltpu.CompilerParams(
            dimension_semantics=("arbitrary",), vmem_limit_bytes=VMEM_LIMIT),
        name="experts",
    )(block_expert, nused, nvalid, xs, w_up, b_up, w_down, b_down)


def _combine_kernel(n, dest_ref, dest_next_ref, x1_ref, mf_ref, gfin_ref, outs_hbm, y_ref, buf,
                    sems):
    i = pl.program_id(0)

    def gather(d_ref, slot):
        def issue(t, carry):
            for k in range(TOP_K):
                dst = buf.at[slot, pl.ds(pl.multiple_of((k * TT + t) * ROW_TILES, ROW_TILES),
                                         ROW_TILES), :]
                pltpu.make_async_copy(_row_tile(outs_hbm, d_ref[0, 0, t * TOP_K + k]), dst,
                                      sems.at[slot]).start(priority=k % 2)
            return carry

        lax.fori_loop(0, TT, issue, 0)

    slot = i % 2

    @pl.when(i == 0)
    def _():
        gather(dest_ref, 0)

    @pl.when(i + 1 < n)
    def _():
        gather(dest_next_ref, 1 - slot)

    pltpu.make_async_copy(outs_hbm.at[pl.ds(0, buf.shape[1]), :], buf.at[slot], sems.at[slot]).wait()

    y = x1_ref[...]
    gates = mf_ref[...]
    rows = buf.at[slot]
    for k in range(TOP_K):
        y = y + gates[:, k:k + 1] * _load_row_tiles(rows, k * TT * ROW_TILES, TT)
    y_ref[...] = _rms(y, gfin_ref[...])


def _combine(dest, x1, meta_f, g_final, outs):
    n_tok, d = x1.shape
    n_tiles = n_tok // TT
    tok = lambda i: (i, 0)
    return pl.pallas_call(
        functools.partial(_combine_kernel, n_tiles),
        grid=(n_tiles,),
        in_specs=[
            pl.BlockSpec((1, 1, TT * TOP_K), lambda i: (i, 0, 0), memory_space=pltpu.SMEM),
            pl.BlockSpec((1, 1, TT * TOP_K), lambda i: (jnp.minimum(i + 1, n_tiles - 1), 0, 0),
                         memory_space=pltpu.SMEM),
            pl.BlockSpec((TT, d), tok),
            pl.BlockSpec((TT, LANES), tok),
            _full(g_final.shape),
            pl.BlockSpec(memory_space=pl.ANY),
        ],
        out_specs=pl.BlockSpec((TT, d), tok),
        out_shape=jax.ShapeDtypeStruct((n_tok, d), _F32),
        scratch_shapes=[pltpu.VMEM((2, TOP_K * TT * ROW_TILES, LANES), _F32),
                        pltpu.SemaphoreType.DMA((2,))],
        compiler_params=pltpu.CompilerParams(
            dimension_semantics=("arbitrary",), vmem_limit_bytes=VMEM_LIMIT),
        name="combine",
    )(dest, dest, x1, meta_f, g_final, outs)


def _prep_weights(norm_mix_g, w_in, ln_v_g, ln_v_b, w_spatial, b_spatial, w_branch_a, conv_w,
                  conv_b, ln_conv_g, ln_conv_b, w_branch_b, b_branch_b, b_gate_a, b_gate_b, w_out,
                  norm_ffn_g, w_router, b_router, n_new):
    row = lambda a: a.reshape(1, -1)
    pad = LANES - N_EXPERTS
    wr = jnp.pad(w_router, ((0, 0), (0, pad)))
    wr_hi = wr.astype(_BF16)
    wr_lo = (wr - wr_hi.astype(_F32)).astype(_BF16)
    wcs = jnp.repeat(w_spatial[:, :n_new, :n_new].transpose(1, 2, 0).reshape(n_new * n_new, A_GROUPS),
                     LANES, axis=1)
    return dict(
        gmix=row(norm_mix_g), win=w_in.astype(_BF16), lnvg=row(ln_v_g), lnvb=row(ln_v_b),
        wsp=w_spatial, bsp=jnp.repeat(b_spatial.T, LANES, axis=1), wcs=wcs,
        bcs=jnp.repeat(b_spatial[:, :n_new].T, LANES, axis=1),
        wa=w_branch_a.astype(_BF16), cw=conv_w, cb=row(conv_b), lncg=row(ln_conv_g),
        lncb=row(ln_conv_b), wb=w_branch_b.astype(_BF16), bb=row(b_branch_b), bga=row(b_gate_a),
        bgb=row(b_gate_b), wout=w_out.astype(_BF16), gffn=row(norm_ffn_g),
        wr=jnp.concatenate([wr_hi, wr_lo], axis=1), br=jnp.pad(row(b_router), ((0, 0), (0, pad))))


def kernel(x_prompt, x_sample, cache_conv, norm_mix_g, w_in, ln_v_g, ln_v_b, w_spatial, b_spatial, w_branch_a, conv_w, conv_b, ln_conv_g, ln_conv_b, w_branch_b, b_branch_b, b_gate_a, b_gate_b, w_out, norm_ffn_g, w_router, b_router, w_up, b_up, w_down, b_down, norm_final_g):
    assert w_in.shape[0] == 1, "single trunk layer"
    bsz, seq, d = x_prompt.shape
    nseq, n_new, _ = x_sample.shape
    assert d == D_MODEL and seq % TT == 0 and TT % CHUNK == 0 and nseq % SB == 0
    assert n_new <= CHUNK and (nseq * n_new) % TT == 0
    w = _prep_weights(norm_mix_g[0], w_in[0], ln_v_g[0], ln_v_b[0], w_spatial[0], b_spatial[0],
                      w_branch_a[0], conv_w[0], conv_b[0], ln_conv_g[0], ln_conv_b[0],
                      w_branch_b[0], b_branch_b[0], b_gate_a[0], b_gate_b[0], w_out[0],
                      norm_ffn_g[0], w_router[0], b_router[0], n_new)

    x1_p, h2_p, mi_p, mf_p, cnt_p, cst_p = _mixer_prompt(x_prompt, w)
    x_t = x_sample.transpose(1, 0, 2)
    cache_t = cache_conv[0].transpose(1, 0, 2)
    x1_s, h2_s, mi_s, mf_s, cnt, glu_s, v_s = _mixer_sample(x_t, cache_t, cnt_p, w)
    n_p, n_s = bsz * seq, nseq * n_new
    x1_s = x1_s.reshape(n_s, d)
    h2_s = h2_s.reshape(n_s * ROW_TILES, LANES)
    mi_s = mi_s.reshape(n_s, LANES)
    mf_s = mf_s.reshape(n_s, LANES)

    counts = cnt[0, :N_EXPERTS].astype(jnp.int32)
    padded = ((counts + BLK - 1) // BLK) * BLK
    pend = jnp.cumsum(padded)
    pstart = pend - padded
    n_assign = (n_p + n_s) * TOP_K
    n_blk = -(-(n_assign + N_EXPERTS * (BLK - 1)) // BLK)
    nused = (pend[-1] // BLK).astype(jnp.int32)
    blk_ids = jnp.arange(n_blk, dtype=jnp.int32)
    first_row = jnp.minimum(blk_ids, nused - 1) * BLK
    be = jnp.sum((pend[None, :] <= first_row[:, None]).astype(jnp.int32), axis=1)
    be = jnp.minimum(be, N_EXPERTS - 1)
    experts_row = jnp.arange(N_EXPERTS, dtype=jnp.int32)

    def dest_of(mi):
        sel = mi[:, :TOP_K, None] == experts_row[None, None, :]
        dest = jnp.sum(jnp.where(sel, pstart[None, None, :], 0), axis=-1) + mi[:, TOP_K:2 * TOP_K]
        return dest.reshape(-1, 1, TT * TOP_K)

    dest_p, dest_s = dest_of(mi_p), dest_of(mi_s)
    xs = _dispatch(pstart + counts, padded - counts, nused.reshape(1),
                   jnp.concatenate([dest_p, dest_s], axis=0), h2_p, h2_s, n_blk)
    of_block = be[:, None] == experts_row[None, :]
    row_in_expert = blk_ids * BLK - jnp.sum(jnp.where(of_block, pstart[None, :], 0), axis=1)
    nvalid = jnp.clip(jnp.sum(jnp.where(of_block, counts[None, :], 0), axis=1) - row_in_expert,
                      0, BLK)
    nvalid = jnp.where(blk_ids < nused, nvalid, 0).astype(jnp.int32)
    outs = _experts(be, nused.reshape(1), nvalid, xs, w_up[0], b_up[0][:, None, :], w_down[0],
                    b_down[0][:, None, :])
    g_final = norm_final_g.reshape(1, d)
    y_p = _combine(dest_p, x1_p, mf_p, g_final, outs)
    y_s = _combine(dest_s, x1_s, mf_s, g_final, outs)

    y_prompt = y_p.reshape(bsz, seq, d)
    y_sample = y_s.reshape(n_new, nseq, d).transpose(1, 0, 2)
    conv_state_prompt = cst_p[None]
    glu_new = glu_s.transpose(1, 0, 2)
    conv_state_sample = jnp.concatenate([cache_conv[0][:, n_new:], glu_new], axis=1)[None]
    chunk_v_sample = v_s.transpose(1, 0, 2)[None]
    return (y_prompt, y_sample, conv_state_prompt, conv_state_sample, chunk_v_sample)
```

```python
import functools

import jax
import jax.numpy as jnp
from jax import lax
from jax.experimental import pallas as pl
from jax.experimental.pallas import tpu as pltpu

D_MODEL = 1024
CHUNK = 128
A_GROUPS = 8
CONV_WIDTH = 31
HIST = 32
N_EXPERTS = 32
TOP_K = 4
D_EXPERT = 1024
SWIGLU_LIMIT = 7.0
SWIGLU_ALPHA = 1.702
EPS = 1e-5
LANES = 128
SUBLANES = 8
ROW_TILES = D_MODEL // LANES
TT = 256
CONV_ROWS = 64
SB = 32
BLK = 512
SUB = 256
VMEM_LIMIT = 56 * 1024 * 1024

_F32 = jnp.float32
_BF16 = jnp.bfloat16
_INV_SQRT2 = 0.7071067811865476


def _rms(x, g):
    return x * lax.rsqrt(jnp.mean(x * x, axis=-1, keepdims=True) + EPS) * g


def _ln(x, g, b):
    mu = jnp.mean(x, axis=-1, keepdims=True)
    xc = x - mu
    var = jnp.mean(xc * xc, axis=-1, keepdims=True)
    return xc * lax.rsqrt(var + EPS) * g + b


def _gelu(x):
    return 0.5 * x * (1.0 + lax.erf(x * _INV_SQRT2))


def _sigmoid(x):
    return 1.0 / (1.0 + jnp.exp(-x))


def _dot(a, b):
    return jnp.dot(a, b, preferred_element_type=_F32)


def _store_row_tiles(ref, lead, n, val, start=0):
    for j in range(ROW_TILES):
        ref[lead + (pl.ds(start + j, n, stride=ROW_TILES), slice(None))] = (
            val[:, j * LANES:(j + 1) * LANES])


def _load_row_tiles(ref, start, n):
    return jnp.concatenate(
        [ref[pl.ds(start + j, n, stride=ROW_TILES), :] for j in range(ROW_TILES)], axis=1)


def _route(h2, wr_ref, br_ref, carry_ref):
    n = h2.shape[0]
    h_hi = h2.astype(_BF16)
    h_lo = (h2 - h_hi.astype(_F32)).astype(_BF16)
    by_hi = _dot(h_hi, wr_ref[...])
    logits = (by_hi[:, :LANES] + by_hi[:, LANES:] + _dot(h_lo, wr_ref[:, :LANES])) + br_ref[...]
    lane = lax.broadcasted_iota(jnp.int32, (n, LANES), 1)
    l = jnp.where(lane < N_EXPERTS, logits, -jnp.inf)
    onehots, vals, idxs = [], [], []
    for _ in range(TOP_K):
        m = jnp.max(l, axis=-1, keepdims=True)
        idx = jnp.min(jnp.where(l == m, lane, LANES), axis=-1, keepdims=True)
        oh = lane == idx
        onehots.append(oh)
        vals.append(m)
        idxs.append(idx)
        l = jnp.where(oh, -jnp.inf, l)
    exps = [jnp.exp(v - vals[0]) for v in vals]
    denom = exps[0] + exps[1] + exps[2] + exps[3]
    gates = [e / denom for e in exps]

    oh_all = jnp.zeros((n, LANES), _F32)
    for oh in onehots:
        oh_all = oh_all + jnp.where(oh, 1.0, 0.0)
    ri = lax.broadcasted_iota(jnp.int32, (n, n), 0)
    ci = lax.broadcasted_iota(jnp.int32, (n, n), 1)
    tri = jnp.where(ci < ri, 1.0, 0.0).astype(_BF16)
    base = _dot(tri, oh_all.astype(_BF16)) + carry_ref[0:1, :]
    ranks = [jnp.sum(jnp.where(oh, base, 0.0), axis=-1, keepdims=True) for oh in onehots]
    carry_ref[...] = carry_ref[...] + jnp.sum(oh_all, axis=0, keepdims=True)

    meta_i = jnp.zeros((n, LANES), jnp.int32)
    meta_f = jnp.zeros((n, LANES), _F32)
    for k in range(TOP_K):
        meta_i = jnp.where(lane == k, idxs[k], meta_i)
        meta_i = jnp.where(lane == TOP_K + k, ranks[k].astype(jnp.int32), meta_i)
        meta_f = jnp.where(lane == k, gates[k], meta_f)
    return meta_i, meta_f


def _mixer_prompt_kernel(x_ref, gmix_ref, win_ref, lnvg_ref, lnvb_ref, wsp_ref, bsp_ref, wa_ref,
                         cw_ref, cb_ref, lncg_ref, lncb_ref, wb_ref, bb_ref, bga_ref, bgb_ref,
                         wout_ref, gffn_ref, wr_ref, br_ref,
                         x1_ref, h2_ref, mi_ref, mf_ref, cnt_ref, cst_ref,
                         cbuf, carry):
    b = pl.program_id(0)
    t = pl.program_id(1)

    @pl.when((b == 0) & (t == 0))
    def _():
        carry[...] = jnp.zeros_like(carry)

    @pl.when(t == 0)
    def _():
        cbuf[0:HIST, :] = jnp.zeros((HIST, D_MODEL), _F32)
        cbuf[HIST + TT:, :] = jnp.zeros((SUBLANES, D_MODEL), _F32)

    x = x_ref[...]
    hb = _rms(x, gmix_ref[...]).astype(_BF16)

    def proj(s):
        return _dot(hb, win_ref[:, s * D_MODEL:(s + 1) * D_MODEL])

    glu = proj(2) * _sigmoid(proj(3))
    cbuf[HIST:HIST + TT, :] = glu
    off = HIST - (CONV_WIDTH - 1)
    span = HIST + CONV_ROWS + SUBLANES
    conv_cols = []
    other, half, pieces = (0, 1, 4, 5), D_MODEL // 2, []
    for c in range(D_MODEL // LANES):
        cols = slice(c * LANES, (c + 1) * LANES)
        accs = []
        for r in range(TT // CONV_ROWS):
            acc = jnp.zeros((CONV_ROWS, LANES), _F32) + cb_ref[:, cols]
            window = cbuf[r * CONV_ROWS:r * CONV_ROWS + span, cols]
            for sh in range(SUBLANES):
                shifted = window if sh == 0 else pltpu.roll(window, span - sh, axis=0)
                for a in range(HIST // SUBLANES + 1):
                    k = a * SUBLANES + sh - off
                    if 0 <= k < CONV_WIDTH:
                        acc = acc + (cw_ref[k:k + 1, cols]
                                     * shifted[a * SUBLANES:a * SUBLANES + CONV_ROWS, :])
            accs.append(acc)
        conv_cols.append(jnp.concatenate(accs, axis=0))
        lo = other[c // 2] * D_MODEL + (c % 2) * half
        pieces.append(_dot(hb, win_ref[:, lo:lo + half]))
    conv = jnp.concatenate(conv_cols, axis=1)
    p_u, p_v, p_ga, p_gb = [jnp.concatenate(pieces[2 * i:2 * i + 2], axis=1) for i in range(4)]
    cbuf[0:HIST, :] = cbuf[TT:TT + HIST, :]
    cn = _ln(conv, lncg_ref[...], lncb_ref[...])
    y_b = _dot((cn * _sigmoid(cn)).astype(_BF16), wb_ref[...]) + bb_ref[...]

    u = _gelu(p_u)
    v = _ln(_gelu(p_v), lnvg_ref[...], lnvb_ref[...])
    ri = lax.broadcasted_iota(jnp.int32, (CHUNK, CHUNK), 0)
    ci = lax.broadcasted_iota(jnp.int32, (CHUNK, CHUNK), 1)
    s_rows = []
    for c in range(TT // CHUNK):
        s_cols = []
        for g in range(A_GROUPS):
            wm = jnp.where(ci <= ri, wsp_ref[g], 0.0).astype(_BF16)
            vb = v[c * CHUNK:(c + 1) * CHUNK, g * LANES:(g + 1) * LANES].astype(_BF16)
            s_cols.append(_dot(wm, vb))
        s_rows.append(jnp.concatenate(s_cols, axis=1) + bsp_ref[...])
    s = jnp.concatenate(s_rows, axis=0)
    y_a = _dot((u * s).astype(_BF16), wa_ref[...])

    merged = _sigmoid(p_ga + bga_ref[...]) * y_a + _sigmoid(p_gb + bgb_ref[...]) * y_b
    x1 = x + _dot(merged.astype(_BF16), wout_ref[...])
    x1_ref[...] = x1

    h2 = _rms(x1, gffn_ref[...])
    _store_row_tiles(h2_ref, (), TT, h2)
    meta_i, meta_f = _route(h2, wr_ref, br_ref, carry)
    mi_ref[...] = meta_i
    mf_ref[...] = meta_f
    cnt_ref[...] = carry[...]

    @pl.when(t == pl.num_programs(1) - 1)
    def _():
        cst_ref[...] = cbuf[HIST - (CONV_WIDTH - 1):HIST, :]


def _full(shape):
    return pl.BlockSpec(shape, lambda *_: (0,) * len(shape))


def _mixer_prompt(x, w):
    bsz, seq, d = x.shape
    n_tok = bsz * seq
    nt = seq // TT
    tok = lambda b, t: (b * nt + t, 0)
    weights = (w["gmix"], w["win"], w["lnvg"], w["lnvb"], w["wsp"], w["bsp"], w["wa"], w["cw"],
               w["cb"], w["lncg"], w["lncb"], w["wb"], w["bb"], w["bga"], w["bgb"], w["wout"],
               w["gffn"], w["wr"], w["br"])
    return pl.pallas_call(
        _mixer_prompt_kernel,
        grid=(bsz, nt),
        in_specs=[pl.BlockSpec((None, TT, d), lambda b, t: (b, t, 0))]
                 + [_full(a.shape) for a in weights],
        out_specs=[
            pl.BlockSpec((TT, d), tok),
            pl.BlockSpec((TT * ROW_TILES, LANES), tok),
            pl.BlockSpec((TT, LANES), tok),
            pl.BlockSpec((TT, LANES), tok),
            _full((SUBLANES, LANES)),
            pl.BlockSpec((None, CONV_WIDTH - 1, d), lambda b, t: (b, 0, 0)),
        ],
        out_shape=[
            jax.ShapeDtypeStruct((n_tok, d), _F32),
            jax.ShapeDtypeStruct((n_tok * ROW_TILES, LANES), _F32),
            jax.ShapeDtypeStruct((n_tok, LANES), jnp.int32),
            jax.ShapeDtypeStruct((n_tok, LANES), _F32),
            jax.ShapeDtypeStruct((SUBLANES, LANES), _F32),
            jax.ShapeDtypeStruct((bsz, CONV_WIDTH - 1, d), _F32),
        ],
        scratch_shapes=[pltpu.VMEM((HIST + TT + SUBLANES, d), _F32),
                        pltpu.VMEM((SUBLANES, LANES), _F32)],
        compiler_params=pltpu.CompilerParams(
            dimension_semantics=("arbitrary", "arbitrary"), vmem_limit_bytes=VMEM_LIMIT),
        name="mixer_prompt",
    )(x, *weights)


def _mixer_sample_kernel(x_ref, cache_ref, cnt0_ref, gmix_ref, win_ref, lnvg_ref, lnvb_ref,
                         wcs_ref, bcs_ref, wa_ref, cw_ref, cb_ref, lncg_ref, lncb_ref, wb_ref,
                         bb_ref, bga_ref, bgb_ref, wout_ref, gffn_ref, wr_ref, br_ref,
                         x1_ref, h2_ref, mi_ref, mf_ref, cnt_ref, glu_ref, v_ref,
                         carry):
    nt = x_ref.shape[0]
    n = nt * SB

    @pl.when(pl.program_id(0) == 0)
    def _():
        carry[...] = cnt0_ref[...]

    x = x_ref[...].reshape(n, D_MODEL)
    hb = _rms(x, gmix_ref[...]).astype(_BF16)

    def proj(s):
        return _dot(hb, win_ref[:, s * D_MODEL:(s + 1) * D_MODEL])

    u = _gelu(proj(0))
    v = _ln(_gelu(proj(1)), lnvg_ref[...], lnvb_ref[...])
    v_ref[...] = v.reshape(nt, SB, D_MODEL)
    s_rows = []
    for i in range(nt):
        acc = jnp.zeros((SB, D_MODEL), _F32) + bcs_ref[i:i + 1, :]
        for j in range(i + 1):
            acc = acc + wcs_ref[i * nt + j:i * nt + j + 1, :] * v[j * SB:(j + 1) * SB, :]
        s_rows.append(acc)
    s = jnp.concatenate(s_rows, axis=0)
    y_a = _dot((u * s).astype(_BF16), wa_ref[...])

    glu = proj(2) * _sigmoid(proj(3))
    glu_ref[...] = glu.reshape(nt, SB, D_MODEL)
    hist = CONV_WIDTH - 1
    conv_rows = []
    for i in range(nt):
        acc = jnp.zeros((SB, D_MODEL), _F32) + cb_ref[...]
        for k in range(CONV_WIDTH):
            p = i + k
            src = cache_ref[p] if p < hist else glu[(p - hist) * SB:(p - hist + 1) * SB, :]
            acc = acc + cw_ref[k:k + 1, :] * src
        conv_rows.append(acc)
    cn = _ln(jnp.concatenate(conv_rows, axis=0), lncg_ref[...], lncb_ref[...])
    y_b = _dot((cn * _sigmoid(cn)).astype(_BF16), wb_ref[...]) + bb_ref[...]

    merged = (_sigmoid(proj(4) + bga_ref[...]) * y_a + _sigmoid(proj(5) + bgb_ref[...]) * y_b)
    x1 = x + _dot(merged.astype(_BF16), wout_ref[...])
    x1_ref[...] = x1.reshape(nt, SB, D_MODEL)

    h2 = _rms(x1, gffn_ref[...])
    for i in range(nt):
        _store_row_tiles(h2_ref, (i,), SB, h2[i * SB:(i + 1) * SB, :])
    meta_i, meta_f = _route(h2, wr_ref, br_ref, carry)
    mi_ref[...] = meta_i.reshape(nt, SB, LANES)
    mf_ref[...] = meta_f.reshape(nt, SB, LANES)
    cnt_ref[...] = carry[...]


def _mixer_sample(x_t, cache_t, cnt0, w):
    nt, nseq, d = x_t.shape
    hist = cache_t.shape[0]
    weights = (w["gmix"], w["win"], w["lnvg"], w["lnvb"], w["wcs"], w["bcs"], w["wa"], w["cw"],
               w["cb"], w["lncg"], w["lncb"], w["wb"], w["bb"], w["bga"], w["bgb"], w["wout"],
               w["gffn"], w["wr"], w["br"])
    seqs = lambda i: (0, i, 0)
    return pl.pallas_call(
        _mixer_sample_kernel,
        grid=(nseq // SB,),
        in_specs=[pl.BlockSpec((nt, SB, d), seqs), pl.BlockSpec((hist, SB, d), seqs),
                  _full(cnt0.shape)] + [_full(a.shape) for a in weights],
        out_specs=[
            pl.BlockSpec((nt, SB, d), seqs),
            pl.BlockSpec((nt, SB * ROW_TILES, LANES), seqs),
            pl.BlockSpec((nt, SB, LANES), seqs),
            pl.BlockSpec((nt, SB, LANES), seqs),
            _full((SUBLANES, LANES)),
            pl.BlockSpec((nt, SB, d), seqs),
            pl.BlockSpec((nt, SB, d), seqs),
        ],
        out_shape=[
            jax.ShapeDtypeStruct((nt, nseq, d), _F32),
            jax.ShapeDtypeStruct((nt, nseq * ROW_TILES, LANES), _F32),
            jax.ShapeDtypeStruct((nt, nseq, LANES), jnp.int32),
            jax.ShapeDtypeStruct((nt, nseq, LANES), _F32),
            jax.ShapeDtypeStruct((SUBLANES, LANES), _F32),
            jax.ShapeDtypeStruct((nt, nseq, d), _F32),
            jax.ShapeDtypeStruct((nt, nseq, d), _F32),
        ],
        scratch_shapes=[pltpu.VMEM((SUBLANES, LANES), _F32)],
        compiler_params=pltpu.CompilerParams(
            dimension_semantics=("arbitrary",), vmem_limit_bytes=VMEM_LIMIT),
        name="mixer_sample",
    )(x_t, cache_t, cnt0, *weights)


def _row_tile(ref, row):
    return ref.at[pl.ds(pl.multiple_of(row * ROW_TILES, ROW_TILES), ROW_TILES), :]


def _dispatch_kernel(n_first, n_blk, pad_start_ref, pad_len_ref, nused_ref, dest_ref, h2a_ref,
                     h2b_ref, xs_hbm, zbuf, sem, zsem):
    i = pl.program_id(0)

    def fill(start):
        def go(copy):
            copy.start() if start else copy.wait()

        def per_expert(e, carry):
            first, length = pad_start_ref[e], pad_len_ref[e]
            for s in range(1, BLK.bit_length()):
                p = BLK >> s

                @pl.when((length & p) != 0)
                def _():
                    row = first + (length & (-2 * p))
                    go(pltpu.make_async_copy(
                        zbuf.at[pl.ds(0, p * ROW_TILES), :],
                        xs_hbm.at[pl.ds(pl.multiple_of(row * ROW_TILES, ROW_TILES), p * ROW_TILES), :],
                        zsem))
            return carry

        def per_tail_block(b, carry):
            go(pltpu.make_async_copy(
                zbuf, xs_hbm.at[pl.ds(pl.multiple_of(b * BLK * ROW_TILES, ROW_TILES),
                                      BLK * ROW_TILES), :], zsem))
            return carry

        lax.fori_loop(0, N_EXPERTS, per_expert, 0)
        lax.fori_loop(nused_ref[0], n_blk, per_tail_block, 0)

    @pl.when(i == 0)
    def _():
        zbuf[...] = jnp.zeros_like(zbuf)
        fill(True)

    def scatter(h2_ref):
        def issue(t, carry):
            src = _row_tile(h2_ref, t)
            for k in range(TOP_K):
                pltpu.make_async_copy(src, _row_tile(xs_hbm, dest_ref[0, 0, t * TOP_K + k]),
                                      sem).start(priority=k % 2)
            return carry

        lax.fori_loop(0, TT, issue, 0)
        for _ in range(TOP_K):
            pltpu.make_async_copy(h2_ref, xs_hbm.at[pl.ds(0, TT * ROW_TILES), :], sem).wait()

    @pl.when(i < n_first)
    def _():
        scatter(h2a_ref)

    @pl.when(i >= n_first)
    def _():
        scatter(h2b_ref)

    @pl.when(i == 0)
    def _():
        fill(False)


def _dispatch(pad_start, pad_len, nused, dest, h2_a, h2_b, n_blk):
    n_tiles = dest.shape[0]
    n_first = h2_a.shape[0] // (TT * ROW_TILES)
    tile = (TT * ROW_TILES, LANES)
    grid_spec = pltpu.PrefetchScalarGridSpec(
        num_scalar_prefetch=3,
        grid=(n_tiles,),
        in_specs=[
            pl.BlockSpec((1, 1, TT * TOP_K), lambda i, *_: (i, 0, 0), memory_space=pltpu.SMEM),
            pl.BlockSpec(tile, lambda i, *_: (jnp.minimum(i, n_first - 1), 0)),
            pl.BlockSpec(tile, lambda i, *_: (jnp.maximum(i - n_first, 0), 0)),
        ],
        out_specs=pl.BlockSpec(memory_space=pl.ANY),
        scratch_shapes=[pltpu.VMEM((BLK * ROW_TILES, LANES), _F32), pltpu.SemaphoreType.DMA,
                        pltpu.SemaphoreType.DMA],
    )
    return pl.pallas_call(
        functools.partial(_dispatch_kernel, n_first, n_blk),
        grid_spec=grid_spec,
        out_shape=jax.ShapeDtypeStruct((n_blk * BLK * ROW_TILES, LANES), _F32),
        compiler_params=pltpu.CompilerParams(
            dimension_semantics=("arbitrary",), has_side_effects=True),
        name="dispatch",
    )(pad_start, pad_len, nused, dest, h2_a, h2_b)


def _experts_kernel(be_ref, nused_ref, nvalid_ref, xs_ref, wup_ref, bup_ref, wdn_ref, bdn_ref,
                    out_ref, wup_bf, wdn_bf):
    del nused_ref
    i = pl.program_id(0)
    prev = be_ref[jnp.maximum(i - 1, 0)]

    @pl.when((i == 0) | (be_ref[i] != prev))
    def _():
        rows = 128

        def cast(r, carry):
            sl = pl.ds(pl.multiple_of(r * rows, rows), rows)
            wup_bf[sl, :] = wup_ref[sl, :].astype(_BF16)
            wdn_bf[sl, :] = wdn_ref[sl, :].astype(_BF16)
            return carry

        lax.fori_loop(0, D_MODEL // rows, cast, 0)

    def first_rows(n):
        if n:
            xb = _load_row_tiles(xs_ref, 0, n).astype(_BF16)
            gate = _dot(xb, wup_bf[:, :D_EXPERT]) + bup_ref[:, :D_EXPERT]
            up = _dot(xb, wup_bf[:, D_EXPERT:]) + bup_ref[:, D_EXPERT:]
            gate = jnp.minimum(gate, SWIGLU_LIMIT)
            up = jnp.clip(up, -SWIGLU_LIMIT, SWIGLU_LIMIT)
            act = (up + 1.0) * (gate * _sigmoid(SWIGLU_ALPHA * gate))
            o = _dot(act.astype(_BF16), wdn_bf[...]) + bdn_ref[...]
            _store_row_tiles(out_ref, (), n, o)
        if n < BLK:
            out_ref[n * ROW_TILES:, :] = jnp.zeros(((BLK - n) * ROW_TILES, LANES), _F32)

    n_valid = nvalid_ref[i]
    for live in range(BLK // SUB + 1):
        lo, hi = (live - 1) * SUB, live * SUB

        @pl.when((n_valid > lo) & (n_valid <= hi) if live else n_valid <= 0)
        def _():
            first_rows(live * SUB)


def _experts(block_expert, nused, nvalid, xs, w_up, b_up, w_down, b_down):
    n_blk = block_expert.shape[0]
    rows = lambda i, be, nu, nv: (jnp.minimum(i, nu[0] - 1), 0)
    expert = lambda i, be, nu, nv: (be[i], 0, 0)
    d, f2 = w_up.shape[1], w_up.shape[2]
    grid_spec = pltpu.PrefetchScalarGridSpec(
        num_scalar_prefetch=3,
        grid=(n_blk,),
        in_specs=[
            pl.BlockSpec((BLK * ROW_TILES, LANES), rows),
            pl.BlockSpec((None, d, f2), expert),
            pl.BlockSpec((None, 1, f2), expert),
            pl.BlockSpec((None, w_down.shape[1], d), expert),
            pl.BlockSpec((None, 1, d), expert),
        ],
        out_specs=pl.BlockSpec((BLK * ROW_TILES, LANES), lambda i, be, nu, nv: (i, 0)),
        scratch_shapes=[pltpu.VMEM((d, f2), _BF16), pltpu.VMEM((w_down.shape[1], d), _BF16)],
    )
    return pl.pallas_call(
        _experts_kernel,
        grid_spec=grid_spec,
        out_shape=jax.ShapeDtypeStruct(xs.shape, _F32),
        compiler_params=pltpu.CompilerParams(
            dimension_semantics=("arbitrary",), vmem_limit_bytes=VMEM_LIMIT),
        name="experts",
    )(block_expert, nused, nvalid, xs, w_up, b_up, w_down, b_down)


def _combine_kernel(n, dest_ref, dest_next_ref, x1_ref, mf_ref, gfin_ref, outs_hbm, y_ref, buf,
                    sems):
    i = pl.program_id(0)

    def gather(d_ref, slot):
        def issue(t, carry):
            for k in range(TOP_K):
                dst = buf.at[slot, pl.ds(pl.multiple_of((k * TT + t) * ROW_TILES, ROW_TILES),
                                         ROW_TILES), :]
                pltpu.make_async_copy(_row_tile(outs_hbm, d_ref[0, 0, t * TOP_K + k]), dst,
                                      sems.at[slot]).start(priority=k % 2)
            return carry

        lax.fori_loop(0, TT, issue, 0)

    slot = i % 2

    @pl.when(i == 0)
    def _():
        gather(dest_ref, 0)

    @pl.when(i + 1 < n)
    def _():
        gather(dest_next_ref, 1 - slot)

    pltpu.make_async_copy(outs_hbm.at[pl.ds(0, buf.shape[1]), :], buf.at[slot], sems.at[slot]).wait()

    y = x1_ref[...]
    gates = mf_ref[...]
    rows = buf.at[slot]
    for k in range(TOP_K):
        y = y + gates[:, k:k + 1] * _load_row_tiles(rows, k * TT * ROW_TILES, TT)
    y_ref[...] = _rms(y, gfin_ref[...])


def _combine(dest, x1, meta_f, g_final, outs):
    n_tok, d = x1.shape
    n_tiles = n_tok // TT
    tok = lambda i: (i, 0)
    return pl.pallas_call(
        functools.partial(_combine_kernel, n_tiles),
        grid=(n_tiles,),
        in_specs=[
            pl.BlockSpec((1, 1, TT * TOP_K), lambda i: (i, 0, 0), memory_space=pltpu.SMEM),
            pl.BlockSpec((1, 1, TT * TOP_K), lambda i: (jnp.minimum(i + 1, n_tiles - 1), 0, 0),
                         memory_space=pltpu.SMEM),
            pl.BlockSpec((TT, d), tok),
            pl.BlockSpec((TT, LANES), tok),
            _full(g_final.shape),
            pl.BlockSpec(memory_space=pl.ANY),
        ],
        out_specs=pl.BlockSpec((TT, d), tok),
        out_shape=jax.ShapeDtypeStruct((n_tok, d), _F32),
        scratch_shapes=[pltpu.VMEM((2, TOP_K * TT * ROW_TILES, LANES), _F32),
                        pltpu.SemaphoreType.DMA((2,))],
        compiler_params=pltpu.CompilerParams(
            dimension_semantics=("arbitrary",), vmem_limit_bytes=VMEM_LIMIT),
        name="combine",
    )(dest, dest, x1, meta_f, g_final, outs)


def _prep_weights(norm_mix_g, w_in, ln_v_g, ln_v_b, w_spatial, b_spatial, w_branch_a, conv_w,
                  conv_b, ln_conv_g, ln_conv_b, w_branch_b, b_branch_b, b_gate_a, b_gate_b, w_out,
                  norm_ffn_g, w_router, b_router, n_new):
    row = lambda a: a.reshape(1, -1)
    pad = LANES - N_EXPERTS
    wr = jnp.pad(w_router, ((0, 0), (0, pad)))
    wr_hi = wr.astype(_BF16)
    wr_lo = (wr - wr_hi.astype(_F32)).astype(_BF16)
    wcs = jnp.repeat(w_spatial[:, :n_new, :n_new].transpose(1, 2, 0).reshape(n_new * n_new, A_GROUPS),
                     LANES, axis=1)
    return dict(
        gmix=row(norm_mix_g), win=w_in.astype(_BF16), lnvg=row(ln_v_g), lnvb=row(ln_v_b),
        wsp=w_spatial, bsp=jnp.repeat(b_spatial.T, LANES, axis=1), wcs=wcs,
        bcs=jnp.repeat(b_spatial[:, :n_new].T, LANES, axis=1),
        wa=w_branch_a.astype(_BF16), cw=conv_w, cb=row(conv_b), lncg=row(ln_conv_g),
        lncb=row(ln_conv_b), wb=w_branch_b.astype(_BF16), bb=row(b_branch_b), bga=row(b_gate_a),
        bgb=row(b_gate_b), wout=w_out.astype(_BF16), gffn=row(norm_ffn_g),
        wr=jnp.concatenate([wr_hi, wr_lo], axis=1), br=jnp.pad(row(b_router), ((0, 0), (0, pad))))


def kernel(x_prompt, x_sample, cache_conv, norm_mix_g, w_in, ln_v_g, ln_v_b, w_spatial, b_spatial, w_branch_a, conv_w, conv_b, ln_conv_g, ln_conv_b, w_branch_b, b_branch_b, b_gate_a, b_gate_b, w_out, norm_ffn_g, w_router, b_router, w_up, b_up, w_down, b_down, norm_final_g):
    assert w_in.shape[0] == 1, "single trunk layer"
    bsz, seq, d = x_prompt.shape
    nseq, n_new, _ = x_sample.shape
    assert d == D_MODEL and seq % TT == 0 and TT % CHUNK == 0 and nseq % SB == 0
    assert n_new <= CHUNK and (nseq * n_new) % TT == 0
    w = _prep_weights(norm_mix_g[0], w_in[0], ln_v_g[0], ln_v_b[0], w_spatial[0], b_spatial[0],
                      w_branch_a[0], conv_w[0], conv_b[0], ln_conv_g[0], ln_conv_b[0],
                      w_branch_b[0], b_branch_b[0], b_gate_a[0], b_gate_b[0], w_out[0],
                      norm_ffn_g[0], w_router[0], b_router[0], n_new)

    x1_p, h2_p, mi_p, mf_p, cnt_p, cst_p = _mixer_prompt(x_prompt, w)
    x_t = x_sample.transpose(1, 0, 2)
    cache_t = cache_conv[0].transpose(1, 0, 2)
    x1_s, h2_s, mi_s, mf_s, cnt, glu_s, v_s = _mixer_sample(x_t, cache_t, cnt_p, w)
    n_p, n_s = bsz * seq, nseq * n_new
    x1_s = x1_s.reshape(n_s, d)
    h2_s = h2_s.reshape(n_s * ROW_TILES, LANES)
    mi_s = mi_s.reshape(n_s, LANES)
    mf_s = mf_s.reshape(n_s, LANES)

    counts = cnt[0, :N_EXPERTS].astype(jnp.int32)
    padded = ((counts + BLK - 1) // BLK) * BLK
    pend = jnp.cumsum(padded)
    pstart = pend - padded
    n_assign = (n_p + n_s) * TOP_K
    n_blk = -(-(n_assign + N_EXPERTS * (BLK - 1)) // BLK)
    nused = (pend[-1] // BLK).astype(jnp.int32)
    blk_ids = jnp.arange(n_blk, dtype=jnp.int32)
    first_row = jnp.minimum(blk_ids, nused - 1) * BLK
    be = jnp.sum((pend[None, :] <= first_row[:, None]).astype(jnp.int32), axis=1)
    be = jnp.minimum(be, N_EXPERTS - 1)
    experts_row = jnp.arange(N_EXPERTS, dtype=jnp.int32)

    def dest_of(mi):
        sel = mi[:, :TOP_K, None] == experts_row[None, None, :]
        dest = jnp.sum(jnp.where(sel, pstart[None, None, :], 0), axis=-1) + mi[:, TOP_K:2 * TOP_K]
        return dest.reshape(-1, 1, TT * TOP_K)

    dest_p, dest_s = dest_of(mi_p), dest_of(mi_s)
    xs = _dispatch(pstart + counts, padded - counts, nused.reshape(1),
                   jnp.concatenate([dest_p, dest_s], axis=0), h2_p, h2_s, n_blk)
    of_block = be[:, None] == experts_row[None, :]
    row_in_expert = blk_ids * BLK - jnp.sum(jnp.where(of_block, pstart[None, :], 0), axis=1)
    nvalid = jnp.clip(jnp.sum(jnp.where(of_block, counts[None, :], 0), axis=1) - row_in_expert,
                      0, BLK)
    nvalid = jnp.where(blk_ids < nused, nvalid, 0).astype(jnp.int32)
    outs = _experts(be, nused.reshape(1), nvalid, xs, w_up[0], b_up[0][:, None, :], w_down[0],
                    b_down[0][:, None, :])
    g_final = norm_final_g.reshape(1, d)
    y_p = _combine(dest_p, x1_p, mf_p, g_final, outs)
    y_s = _combine(dest_s, x1_s, mf_s, g_final, outs)

    y_prompt = y_p.reshape(bsz, seq, d)
    y_sample = y_s.reshape(n_new, nseq, d).transpose(1, 0, 2)
    conv_state_prompt = cst_p[None]
    glu_new = glu_s.transpose(1, 0, 2)
    conv_state_sample = jnp.concatenate([cache_conv[0][:, n_new:], glu_new], axis=1)[None]
    chunk_v_sample = v_s.transpose(1, 0, 2)[None]
    return (y_prompt, y_sample, conv_state_prompt, conv_state_sample, chunk_v_sample)
```

```python
import functools

import jax
import jax.numpy as jnp
from jax import lax
from jax.experimental import pallas as pl
from jax.experimental.pallas import tpu as pltpu

D_MODEL = 1024
CHUNK = 128
A_GROUPS = 8
CONV_WIDTH = 31
HIST = 32
N_EXPERTS = 32
TOP_K = 4
D_EXPERT = 1024
SWIGLU_LIMIT = 7.0
SWIGLU_ALPHA = 1.702
EPS = 1e-5
LANES = 128
SUBLANES = 8
ROW_TILES = D_MODEL // LANES
TT = 256
CONV_ROWS = 64
SB = 32
BLK = 512
SUB = 256
VMEM_LIMIT = 56 * 1024 * 1024

_F32 = jnp.float32
_BF16 = jnp.bfloat16
_INV_SQRT2 = 0.7071067811865476


def _rms(x, g):
    return x * lax.rsqrt(jnp.mean(x * x, axis=-1, keepdims=True) + EPS) * g


def _ln(x, g, b):
    mu = jnp.mean(x, axis=-1, keepdims=True)
    xc = x - mu
    var = jnp.mean(xc * xc, axis=-1, keepdims=True)
    return xc * lax.rsqrt(var + EPS) * g + b


def _gelu(x):
    return 0.5 * x * (1.0 + lax.erf(x * _INV_SQRT2))


def _sigmoid(x):
    return 1.0 / (1.0 + jnp.exp(-x))


def _dot(a, b):
    return jnp.dot(a, b, preferred_element_type=_F32)


def _store_row_tiles(ref, lead, n, val, start=0):
    for j in range(ROW_TILES):
        ref[lead + (pl.ds(start + j, n, stride=ROW_TILES), slice(None))] = (
            val[:, j * LANES:(j + 1) * LANES])


def _load_row_tiles(ref, start, n):
    return jnp.concatenate(
        [ref[pl.ds(start + j, n, stride=ROW_TILES), :] for j in range(ROW_TILES)], axis=1)


def _route(h2, wr_ref, br_ref, carry_ref):
    n = h2.shape[0]
    h_hi = h2.astype(_BF16)
    h_lo = (h2 - h_hi.astype(_F32)).astype(_BF16)
    by_hi = _dot(h_hi, wr_ref[...])
    logits = (by_hi[:, :LANES] + by_hi[:, LANES:] + _dot(h_lo, wr_ref[:, :LANES])) + br_ref[...]
    lane = lax.broadcasted_iota(jnp.int32, (n, LANES), 1)
    l = jnp.where(lane < N_EXPERTS, logits, -jnp.inf)
    onehots, vals, idxs = [], [], []
    for _ in range(TOP_K):
        m = jnp.max(l, axis=-1, keepdims=True)
        idx = jnp.min(jnp.where(l == m, lane, LANES), axis=-1, keepdims=True)
        oh = lane == idx
        onehots.append(oh)
        vals.append(m)
        idxs.append(idx)
        l = jnp.where(oh, -jnp.inf, l)
    exps = [jnp.exp(v - vals[0]) for v in vals]
    denom = exps[0] + exps[1] + exps[2] + exps[3]
    gates = [e / denom for e in exps]

    oh_all = jnp.zeros((n, LANES), _F32)
    for oh in onehots:
        oh_all = oh_all + jnp.where(oh, 1.0, 0.0)
    ri = lax.broadcasted_iota(jnp.int32, (n, n), 0)
    ci = lax.broadcasted_iota(jnp.int32, (n, n), 1)
    tri = jnp.where(ci < ri, 1.0, 0.0).astype(_BF16)
    base = _dot(tri, oh_all.astype(_BF16)) + carry_ref[0:1, :]
    ranks = [jnp.sum(jnp.where(oh, base, 0.0), axis=-1, keepdims=True) for oh in onehots]
    carry_ref[...] = carry_ref[...] + jnp.sum(oh_all, axis=0, keepdims=True)

    meta_i = jnp.zeros((n, LANES), jnp.int32)
    meta_f = jnp.zeros((n, LANES), _F32)
    for k in range(TOP_K):
        meta_i = jnp.where(lane == k, idxs[k], meta_i)
        meta_i = jnp.where(lane == TOP_K + k, ranks[k].astype(jnp.int32), meta_i)
        meta_f = jnp.where(lane == k, gates[k], meta_f)
    return meta_i, meta_f


def _mixer_prompt_kernel(x_ref, gmix_ref, win_ref, lnvg_ref, lnvb_ref, wsp_ref, bsp_ref, wa_ref,
                         cw_ref, cb_ref, lncg_ref, lncb_ref, wb_ref, bb_ref, bga_ref, bgb_ref,
                         wout_ref, gffn_ref, wr_ref, br_ref,
                         x1_ref, h2_ref, mi_ref, mf_ref, cnt_ref, cst_ref,
                         cbuf, carry):
    b = pl.program_id(0)
    t = pl.program_id(1)

    @pl.when((b == 0) & (t == 0))
    def _():
        carry[...] = jnp.zeros_like(carry)

    @pl.when(t == 0)
    def _():
        cbuf[0:HIST, :] = jnp.zeros((HIST, D_MODEL), _F32)
        cbuf[HIST + TT:, :] = jnp.zeros((SUBLANES, D_MODEL), _F32)

    x = x_ref[...]
    hb = _rms(x, gmix_ref[...]).astype(_BF16)

    def proj(s):
        return _dot(hb, win_ref[:, s * D_MODEL:(s + 1) * D_MODEL])

    glu = proj(2) * _sigmoid(proj(3))
    cbuf[HIST:HIST + TT, :] = glu
    off = HIST - (CONV_WIDTH - 1)
    span = HIST + CONV_ROWS + SUBLANES
    conv_cols = []
    other, half, pieces = (0, 1, 4, 5), D_MODEL // 2, []
    for c in range(D_MODEL // LANES):
        cols = slice(c * LANES, (c + 1) * LANES)
        accs = []
        for r in range(TT // CONV_ROWS):
            acc = jnp.zeros((CONV_ROWS, LANES), _F32) + cb_ref[:, cols]
            window = cbuf[r * CONV_ROWS:r * CONV_ROWS + span, cols]
            for sh in range(SUBLANES):
                shifted = window if sh == 0 else pltpu.roll(window, span - sh, axis=0)
                for a in range(HIST // SUBLANES + 1):
                    k = a * SUBLANES + sh - off
                    if 0 <= k < CONV_WIDTH:
                        acc = acc + (cw_ref[k:k + 1, cols]
                                     * shifted[a * SUBLANES:a * SUBLANES + CONV_ROWS, :])
            accs.append(acc)
        conv_cols.append(jnp.concatenate(accs, axis=0))
        lo = other[c // 2] * D_MODEL + (c % 2) * half
        pieces.append(_dot(hb, win_ref[:, lo:lo + half]))
    conv = jnp.concatenate(conv_cols, axis=1)
    p_u, p_v, p_ga, p_gb = [jnp.concatenate(pieces[2 * i:2 * i + 2], axis=1) for i in range(4)]
    cbuf[0:HIST, :] = cbuf[TT:TT + HIST, :]
    cn = _ln(conv, lncg_ref[...], lncb_ref[...])
    y_b = _dot((cn * _sigmoid(cn)).astype(_BF16), wb_ref[...]) + bb_ref[...]

    u = _gelu(p_u)
    v = _ln(_gelu(p_v), lnvg_ref[...], lnvb_ref[...])
    ri = lax.broadcasted_iota(jnp.int32, (CHUNK, CHUNK), 0)
    ci = lax.broadcasted_iota(jnp.int32, (CHUNK, CHUNK), 1)
    s_rows = []
    for c in range(TT // CHUNK):
        s_cols = []
        for g in range(A_GROUPS):
            wm = jnp.where(ci <= ri, wsp_ref[g], 0.0).astype(_BF16)
            vb = v[c * CHUNK:(c + 1) * CHUNK, g * LANES:(g + 1) * LANES].astype(_BF16)
            s_cols.append(_dot(wm, vb))
        s_rows.append(jnp.concatenate(s_cols, axis=1) + bsp_ref[...])
    s = jnp.concatenate(s_rows, axis=0)
    y_a = _dot((u * s).astype(_BF16), wa_ref[...])

    merged = _sigmoid(p_ga + bga_ref[...]) * y_a + _sigmoid(p_gb + bgb_ref[...]) * y_b
    x1 = x + _dot(merged.astype(_BF16), wout_ref[...])
    x1_ref[...] = x1

    h2 = _rms(x1, gffn_ref[...])
    _store_row_tiles(h2_ref, (), TT, h2)
    meta_i, meta_f = _route(h2, wr_ref, br_ref, carry)
    mi_ref[...] = meta_i
    mf_ref[...] = meta_f
    cnt_ref[...] = carry[...]

    @pl.when(t == pl.num_programs(1) - 1)
    def _():
        cst_ref[...] = cbuf[HIST - (CONV_WIDTH - 1):HIST, :]


def _full(shape):
    return pl.BlockSpec(shape, lambda *_: (0,) * len(shape))


def _mixer_prompt(x, w):
    bsz, seq, d = x.shape
    n_tok = bsz * seq
    nt = seq // TT
    tok = lambda b, t: (b * nt + t, 0)
    weights = (w["gmix"], w["win"], w["lnvg"], w["lnvb"], w["wsp"], w["bsp"], w["wa"], w["cw"],
               w["cb"], w["lncg"], w["lncb"], w["wb"], w["bb"], w["bga"], w["bgb"], w["wout"],
               w["gffn"], w["wr"], w["br"])
    return pl.pallas_call(
        _mixer_prompt_kernel,
        grid=(bsz, nt),
        in_specs=[pl.BlockSpec((None, TT, d), lambda b, t: (b, t, 0))]
                 + [_full(a.shape) for a in weights],
        out_specs=[
            pl.BlockSpec((TT, d), tok),
            pl.BlockSpec((TT * ROW_TILES, LANES), tok),
            pl.BlockSpec((TT, LANES), tok),
            pl.BlockSpec((TT, LANES), tok),
            _full((SUBLANES, LANES)),
            pl.BlockSpec((None, CONV_WIDTH - 1, d), lambda b, t: (b, 0, 0)),
        ],
        out_shape=[
            jax.ShapeDtypeStruct((n_tok, d), _F32),
            jax.ShapeDtypeStruct((n_tok * ROW_TILES, LANES), _F32),
            jax.ShapeDtypeStruct((n_tok, LANES), jnp.int32),
            jax.ShapeDtypeStruct((n_tok, LANES), _F32),
            jax.ShapeDtypeStruct((SUBLANES, LANES), _F32),
            jax.ShapeDtypeStruct((bsz, CONV_WIDTH - 1, d), _F32),
        ],
        scratch_shapes=[pltpu.VMEM((HIST + TT + SUBLANES, d), _F32),
                        pltpu.VMEM((SUBLANES, LANES), _F32)],
        compiler_params=pltpu.CompilerParams(
            dimension_semantics=("arbitrary", "arbitrary"), vmem_limit_bytes=VMEM_LIMIT),
        name="mixer_prompt",
    )(x, *weights)


def _mixer_sample_kernel(x_ref, cache_ref, cnt0_ref, gmix_ref, win_ref, lnvg_ref, lnvb_ref,
                         wcs_ref, bcs_ref, wa_ref, cw_ref, cb_ref, lncg_ref, lncb_ref, wb_ref,
                         bb_ref, bga_ref, bgb_ref, wout_ref, gffn_ref, wr_ref, br_ref,
                         x1_ref, h2_ref, mi_ref, mf_ref, cnt_ref, glu_ref, v_ref,
                         carry):
    nt = x_ref.shape[0]
    n = nt * SB

    @pl.when(pl.program_id(0) == 0)
    def _():
        carry[...] = cnt0_ref[...]

    x = x_ref[...].reshape(n, D_MODEL)
    hb = _rms(x, gmix_ref[...]).astype(_BF16)

    def proj(s):
        return _dot(hb, win_ref[:, s * D_MODEL:(s + 1) * D_MODEL])

    u = _gelu(proj(0))
    v = _ln(_gelu(proj(1)), lnvg_ref[...], lnvb_ref[...])
    v_ref[...] = v.reshape(nt, SB, D_MODEL)
    s_rows = []
    for i in range(nt):
        acc = jnp.zeros((SB, D_MODEL), _F32) + bcs_ref[i:i + 1, :]
        for j in range(i + 1):
            acc = acc + wcs_ref[i * nt + j:i * nt + j + 1, :] * v[j * SB:(j + 1) * SB, :]
        s_rows.append(acc)
    s = jnp.concatenate(s_rows, axis=0)
    y_a = _dot((u * s).astype(_BF16), wa_ref[...])

    glu = proj(2) * _sigmoid(proj(3))
    glu_ref[...] = glu.reshape(nt, SB, D_MODEL)
    hist = CONV_WIDTH - 1
    conv_rows = []
    for i in range(nt):
        acc = jnp.zeros((SB, D_MODEL), _F32) + cb_ref[...]
        for k in range(CONV_WIDTH):
            p = i + k
            src = cache_ref[p] if p < hist else glu[(p - hist) * SB:(p - hist + 1) * SB, :]
            acc = acc + cw_ref[k:k + 1, :] * src
        conv_rows.append(acc)
    cn = _ln(jnp.concatenate(conv_rows, axis=0), lncg_ref[...], lncb_ref[...])
    y_b = _dot((cn * _sigmoid(cn)).astype(_BF16), wb_ref[...]) + bb_ref[...]

    merged = (_sigmoid(proj(4) + bga_ref[...]) * y_a + _sigmoid(proj(5) + bgb_ref[...]) * y_b)
    x1 = x + _dot(merged.astype(_BF16), wout_ref[...])
    x1_ref[...] = x1.reshape(nt, SB, D_MODEL)

    h2 = _rms(x1, gffn_ref[...])
    for i in range(nt):
        _store_row_tiles(h2_ref, (i,), SB, h2[i * SB:(i + 1) * SB, :])
    meta_i, meta_f = _route(h2, wr_ref, br_ref, carry)
    mi_ref[...] = meta_i.reshape(nt, SB, LANES)
    mf_ref[...] = meta_f.reshape(nt, SB, LANES)
    cnt_ref[...] = carry[...]


def _mixer_sample(x_t, cache_t, cnt0, w):
    nt, nseq, d = x_t.shape
    hist = cache_t.shape[0]
    weights = (w["gmix"], w["win"], w["lnvg"], w["lnvb"], w["wcs"], w["bcs"], w["wa"], w["cw"],
               w["cb"], w["lncg"], w["lncb"], w["wb"], w["bb"], w["bga"], w["bgb"], w["wout"],
               w["gffn"], w["wr"], w["br"])
    seqs = lambda i: (0, i, 0)
    return pl.pallas_call(
        _mixer_sample_kernel,
        grid=(nseq // SB,),
        in_specs=[pl.BlockSpec((nt, SB, d), seqs), pl.BlockSpec((hist, SB, d), seqs),
                  _full(cnt0.shape)] + [_full(a.shape) for a in weights],
        out_specs=[
            pl.BlockSpec((nt, SB, d), seqs),
            pl.BlockSpec((nt, SB * ROW_TILES, LANES), seqs),
            pl.BlockSpec((nt, SB, LANES), seqs),
            pl.BlockSpec((nt, SB, LANES), seqs),
            _full((SUBLANES, LANES)),
            pl.BlockSpec((nt, SB, d), seqs),
            pl.BlockSpec((nt, SB, d), seqs),
        ],
        out_shape=[
            jax.ShapeDtypeStruct((nt, nseq, d), _F32),
            jax.ShapeDtypeStruct((nt, nseq * ROW_TILES, LANES), _F32),
            jax.ShapeDtypeStruct((nt, nseq, LANES), jnp.int32),
            jax.ShapeDtypeStruct((nt, nseq, LANES), _F32),
            jax.ShapeDtypeStruct((SUBLANES, LANES), _F32),
            jax.ShapeDtypeStruct((nt, nseq, d), _F32),
            jax.ShapeDtypeStruct((nt, nseq, d), _F32),
        ],
        scratch_shapes=[pltpu.VMEM((SUBLANES, LANES), _F32)],
        compiler_params=pltpu.CompilerParams(
            dimension_semantics=("arbitrary",), vmem_limit_bytes=VMEM_LIMIT),
        name="mixer_sample",
    )(x_t, cache_t, cnt0, *weights)


def _row_tile(ref, row):
    return ref.at[pl.ds(pl.multiple_of(row * ROW_TILES, ROW_TILES), ROW_TILES), :]


def _dispatch_kernel(n_first, n_blk, pad_start_ref, pad_len_ref, nused_ref, dest_ref, h2a_ref,
                     h2b_ref, xs_hbm, zbuf, sem, zsem):
    i = pl.program_id(0)

    def fill(start):
        def go(copy):
            copy.start() if start else copy.wait()

        def per_expert(e, carry):
            first, length = pad_start_ref[e], pad_len_ref[e]
            for s in range(1, BLK.bit_length()):
                p = BLK >> s

                @pl.when((length & p) != 0)
                def _():
                    row = first + (length & (-2 * p))
                    go(pltpu.make_async_copy(
                        zbuf.at[pl.ds(0, p * ROW_TILES), :],
                        xs_hbm.at[pl.ds(pl.multiple_of(row * ROW_TILES, ROW_TILES), p * ROW_TILES), :],
                        zsem))
            return carry

        def per_tail_block(b, carry):
            go(pltpu.make_async_copy(
                zbuf, xs_hbm.at[pl.ds(pl.multiple_of(b * BLK * ROW_TILES, ROW_TILES),
                                      BLK * ROW_TILES), :], zsem))
            return carry

        lax.fori_loop(0, N_EXPERTS, per_expert, 0)
        lax.fori_loop(nused_ref[0], n_blk, per_tail_block, 0)

    @pl.when(i == 0)
    def _():
        zbuf[...] = jnp.zeros_like(zbuf)
        fill(True)

    def scatter(h2_ref):
        def issue(t, carry):
            src = _row_tile(h2_ref, t)
            for k in range(TOP_K):
                pltpu.make_async_copy(src, _row_tile(xs_hbm, dest_ref[0, 0, t * TOP_K + k]),
                                      sem).start(priority=k % 2)
            return carry

        lax.fori_loop(0, TT, issue, 0)
        for _ in range(TOP_K):
            pltpu.make_async_copy(h2_ref, xs_hbm.at[pl.ds(0, TT * ROW_TILES), :], sem).wait()

    @pl.when(i < n_first)
    def _():
        scatter(h2a_ref)

    @pl.when(i >= n_first)
    def _():
        scatter(h2b_ref)

    @pl.when(i == 0)
    def _():
        fill(False)


def _dispatch(pad_start, pad_len, nused, dest, h2_a, h2_b, n_blk):
    n_tiles = dest.shape[0]
    n_first = h2_a.shape[0] // (TT * ROW_TILES)
    tile = (TT * ROW_TILES, LANES)
    grid_spec = pltpu.PrefetchScalarGridSpec(
        num_scalar_prefetch=3,
        grid=(n_tiles,),
        in_specs=[
            pl.BlockSpec((1, 1, TT * TOP_K), lambda i, *_: (i, 0, 0), memory_space=pltpu.SMEM),
            pl.BlockSpec(tile, lambda i, *_: (jnp.minimum(i, n_first - 1), 0)),
            pl.BlockSpec(tile, lambda i, *_: (jnp.maximum(i - n_first, 0), 0)),
        ],
        out_specs=pl.BlockSpec(memory_space=pl.ANY),
        scratch_shapes=[pltpu.VMEM((BLK * ROW_TILES, LANES), _F32), pltpu.SemaphoreType.DMA,
                        pltpu.SemaphoreType.DMA],
    )
    return pl.pallas_call(
        functools.partial(_dispatch_kernel, n_first, n_blk),
        grid_spec=grid_spec,
        out_shape=jax.ShapeDtypeStruct((n_blk * BLK * ROW_TILES, LANES), _F32),
        compiler_params=pltpu.CompilerParams(
            dimension_semantics=("arbitrary",), has_side_effects=True),
        name="dispatch",
    )(pad_start, pad_len, nused, dest, h2_a, h2_b)


def _experts_kernel(be_ref, nused_ref, nvalid_ref, slot_ref, next_ref, xs_ref, bup_ref, bdn_ref,
                    wup_hbm, wdn_hbm, out_ref, wup_buf, wdn_buf, sems):
    del nused_ref
    i = pl.program_id(0)
    expert = be_ref[i]
    slot = slot_ref[i]
    first_of_run = (i == 0) | (expert != be_ref[jnp.maximum(i - 1, 0)])

    def weight_copies(e, s):
        return (pltpu.make_async_copy(wup_hbm.at[e], wup_buf.at[s], sems.at[s]),
                pltpu.make_async_copy(wdn_hbm.at[e], wdn_buf.at[s], sems.at[s]))

    @pl.when(i == 0)
    def _():
        for copy in weight_copies(expert, slot):
            copy.start()

    @pl.when(first_of_run & (next_ref[i] >= 0))
    def _():
        for copy in weight_copies(next_ref[i], 1 - slot):
            copy.start()

    @pl.when(first_of_run)
    def _():
        for copy in weight_copies(expert, slot):
            copy.wait()

    wup = wup_buf.at[slot]
    wdn = wdn_buf.at[slot]

    def first_rows(n):
        if n:
            x = _load_row_tiles(xs_ref, 0, n)
            gate = _dot(x, wup[:, :D_EXPERT]) + bup_ref[:, :D_EXPERT]
            up = _dot(x, wup[:, D_EXPERT:]) + bup_ref[:, D_EXPERT:]
            gate = jnp.minimum(gate, SWIGLU_LIMIT)
            up = jnp.clip(up, -SWIGLU_LIMIT, SWIGLU_LIMIT)
            act = (up + 1.0) * (gate * _sigmoid(SWIGLU_ALPHA * gate))
            o = _dot(act, wdn[...]) + bdn_ref[...]
            _store_row_tiles(out_ref, (), n, o)
        if n < BLK:
            out_ref[n * ROW_TILES:, :] = jnp.zeros(((BLK - n) * ROW_TILES, LANES), _F32)

    n_valid = nvalid_ref[i]
    for live in range(BLK // SUB + 1):
        lo, hi = (live - 1) * SUB, live * SUB

        @pl.when((n_valid > lo) & (n_valid <= hi) if live else n_valid <= 0)
        def _():
            first_rows(live * SUB)


def _experts(block_expert, nused, nvalid, slot, next_expert, xs, w_up, b_up, w_down, b_down):
    n_blk = block_expert.shape[0]
    rows = lambda i, be, nu, *_: (jnp.minimum(i, nu[0] - 1), 0)
    expert = lambda i, be, *_: (be[i], 0, 0)
    d, f2 = w_up.shape[1], w_up.shape[2]
    grid_spec = pltpu.PrefetchScalarGridSpec(
        num_scalar_prefetch=5,
        grid=(n_blk,),
        in_specs=[
            pl.BlockSpec((BLK * ROW_TILES, LANES), rows),
            pl.BlockSpec((None, 1, f2), expert),
            pl.BlockSpec((None, 1, d), expert),
            pl.BlockSpec(memory_space=pl.ANY),
            pl.BlockSpec(memory_space=pl.ANY),
        ],
        out_specs=pl.BlockSpec((BLK * ROW_TILES, LANES), lambda i, *_: (i, 0)),
        scratch_shapes=[pltpu.VMEM((2, d, f2), _F32), pltpu.VMEM((2, w_down.shape[1], d), _F32),
                        pltpu.SemaphoreType.DMA((2,))],
    )
    return pl.pallas_call(
        _experts_kernel,
        grid_spec=grid_spec,
        out_shape=jax.ShapeDtypeStruct(xs.shape, _F32),
        compiler_params=pltpu.CompilerParams(
            dimension_semantics=("arbitrary",), vmem_limit_bytes=VMEM_LIMIT),
        name="experts",
    )(block_expert, nused, nvalid, slot, next_expert, xs, b_up, b_down, w_up, w_down)


def _combine_kernel(n, dest_ref, dest_next_ref, x1_ref, mf_ref, gfin_ref, outs_hbm, y_ref, buf,
                    sems):
    i = pl.program_id(0)

    def gather(d_ref, slot):
        def issue(t, carry):
            for k in range(TOP_K):
                dst = buf.at[slot, pl.ds(pl.multiple_of((k * TT + t) * ROW_TILES, ROW_TILES),
                                         ROW_TILES), :]
                pltpu.make_async_copy(_row_tile(outs_hbm, d_ref[0, 0, t * TOP_K + k]), dst,
                                      sems.at[slot]).start(priority=k % 2)
            return carry

        lax.fori_loop(0, TT, issue, 0)

    slot = i % 2

    @pl.when(i == 0)
    def _():
        gather(dest_ref, 0)

    @pl.when(i + 1 < n)
    def _():
        gather(dest_next_ref, 1 - slot)

    pltpu.make_async_copy(outs_hbm.at[pl.ds(0, buf.shape[1]), :], buf.at[slot], sems.at[slot]).wait()

    y = x1_ref[...]
    gates = mf_ref[...]
    rows = buf.at[slot]
    for k in range(TOP_K):
        y = y + gates[:, k:k + 1] * _load_row_tiles(rows, k * TT * ROW_TILES, TT)
    y_ref[...] = _rms(y, gfin_ref[...])


def _combine(dest, x1, meta_f, g_final, outs):
    n_tok, d = x1.shape
    n_tiles = n_tok // TT
    tok = lambda i: (i, 0)
    return pl.pallas_call(
        functools.partial(_combine_kernel, n_tiles),
        grid=(n_tiles,),
        in_specs=[
            pl.BlockSpec((1, 1, TT * TOP_K), lambda i: (i, 0, 0), memory_space=pltpu.SMEM),
            pl.BlockSpec((1, 1, TT * TOP_K), lambda i: (jnp.minimum(i + 1, n_tiles - 1), 0, 0),
                         memory_space=pltpu.SMEM),
            pl.BlockSpec((TT, d), tok),
            pl.BlockSpec((TT, LANES), tok),
            _full(g_final.shape),
            pl.BlockSpec(memory_space=pl.ANY),
        ],
        out_specs=pl.BlockSpec((TT, d), tok),
        out_shape=jax.ShapeDtypeStruct((n_tok, d), _F32),
        scratch_shapes=[pltpu.VMEM((2, TOP_K * TT * ROW_TILES, LANES), _F32),
                        pltpu.SemaphoreType.DMA((2,))],
        compiler_params=pltpu.CompilerParams(
            dimension_semantics=("arbitrary",), vmem_limit_bytes=VMEM_LIMIT),
        name="combine",
    )(dest, dest, x1, meta_f, g_final, outs)


def _prep_weights(norm_mix_g, w_in, ln_v_g, ln_v_b, w_spatial, b_spatial, w_branch_a, conv_w,
                  conv_b, ln_conv_g, ln_conv_b, w_branch_b, b_branch_b, b_gate_a, b_gate_b, w_out,
                  norm_ffn_g, w_router, b_router, n_new):
    row = lambda a: a.reshape(1, -1)
    pad = LANES - N_EXPERTS
    wr = jnp.pad(w_router, ((0, 0), (0, pad)))
    wr_hi = wr.astype(_BF16)
    wr_lo = (wr - wr_hi.astype(_F32)).astype(_BF16)
    wcs = jnp.repeat(w_spatial[:, :n_new, :n_new].transpose(1, 2, 0).reshape(n_new * n_new, A_GROUPS),
                     LANES, axis=1)
    return dict(
        gmix=row(norm_mix_g), win=w_in.astype(_BF16), lnvg=row(ln_v_g), lnvb=row(ln_v_b),
        wsp=w_spatial, bsp=jnp.repeat(b_spatial.T, LANES, axis=1), wcs=wcs,
        bcs=jnp.repeat(b_spatial[:, :n_new].T, LANES, axis=1),
        wa=w_branch_a.astype(_BF16), cw=conv_w, cb=row(conv_b), lncg=row(ln_conv_g),
        lncb=row(ln_conv_b), wb=w_branch_b.astype(_BF16), bb=row(b_branch_b), bga=row(b_gate_a),
        bgb=row(b_gate_b), wout=w_out.astype(_BF16), gffn=row(norm_ffn_g),
        wr=jnp.concatenate([wr_hi, wr_lo], axis=1), br=jnp.pad(row(b_router), ((0, 0), (0, pad))))


def kernel(x_prompt, x_sample, cache_conv, norm_mix_g, w_in, ln_v_g, ln_v_b, w_spatial, b_spatial, w_branch_a, conv_w, conv_b, ln_conv_g, ln_conv_b, w_branch_b, b_branch_b, b_gate_a, b_gate_b, w_out, norm_ffn_g, w_router, b_router, w_up, b_up, w_down, b_down, norm_final_g):
    assert w_in.shape[0] == 1, "single trunk layer"
    bsz, seq, d = x_prompt.shape
    nseq, n_new, _ = x_sample.shape
    assert d == D_MODEL and seq % TT == 0 and TT % CHUNK == 0 and nseq % SB == 0
    assert n_new <= CHUNK and (nseq * n_new) % TT == 0
    w = _prep_weights(norm_mix_g[0], w_in[0], ln_v_g[0], ln_v_b[0], w_spatial[0], b_spatial[0],
                      w_branch_a[0], conv_w[0], conv_b[0], ln_conv_g[0], ln_conv_b[0],
                      w_branch_b[0], b_branch_b[0], b_gate_a[0], b_gate_b[0], w_out[0],
                      norm_ffn_g[0], w_router[0], b_router[0], n_new)

    x1_p, h2_p, mi_p, mf_p, cnt_p, cst_p = _mixer_prompt(x_prompt, w)
    x_t = x_sample.transpose(1, 0, 2)
    cache_t = cache_conv[0].transpose(1, 0, 2)
    x1_s, h2_s, mi_s, mf_s, cnt, glu_s, v_s = _mixer_sample(x_t, cache_t, cnt_p, w)
    n_p, n_s = bsz * seq, nseq * n_new
    x1_s = x1_s.reshape(n_s, d)
    h2_s = h2_s.reshape(n_s * ROW_TILES, LANES)
    mi_s = mi_s.reshape(n_s, LANES)
    mf_s = mf_s.reshape(n_s, LANES)

    counts = cnt[0, :N_EXPERTS].astype(jnp.int32)
    padded = ((counts + BLK - 1) // BLK) * BLK
    pend = jnp.cumsum(padded)
    pstart = pend - padded
    n_assign = (n_p + n_s) * TOP_K
    n_blk = -(-(n_assign + N_EXPERTS * (BLK - 1)) // BLK)
    nused = (pend[-1] // BLK).astype(jnp.int32)
    blk_ids = jnp.arange(n_blk, dtype=jnp.int32)
    first_row = jnp.minimum(blk_ids, nused - 1) * BLK
    be = jnp.sum((pend[None, :] <= first_row[:, None]).astype(jnp.int32), axis=1)
    be = jnp.minimum(be, N_EXPERTS - 1)
    experts_row = jnp.arange(N_EXPERTS, dtype=jnp.int32)

    def dest_of(mi):
        sel = mi[:, :TOP_K, None] == experts_row[None, None, :]
        dest = jnp.sum(jnp.where(sel, pstart[None, None, :], 0), axis=-1) + mi[:, TOP_K:2 * TOP_K]
        return dest.reshape(-1, 1, TT * TOP_K)

    dest_p, dest_s = dest_of(mi_p), dest_of(mi_s)
    xs = _dispatch(pstart + counts, padded - counts, nused.reshape(1),
                   jnp.concatenate([dest_p, dest_s], axis=0), h2_p, h2_s, n_blk)
    of_block = be[:, None] == experts_row[None, :]
    row_in_expert = blk_ids * BLK - jnp.sum(jnp.where(of_block, pstart[None, :], 0), axis=1)
    nvalid = jnp.clip(jnp.sum(jnp.where(of_block, counts[None, :], 0), axis=1) - row_in_expert,
                      0, BLK)
    nvalid = jnp.where(blk_ids < nused, nvalid, 0).astype(jnp.int32)
    used = counts > 0
    slot_e = (jnp.cumsum(used.astype(jnp.int32)) - 1) % 2
    later = jnp.where(used[None, :] & (experts_row[None, :] > experts_row[:, None]),
                      experts_row[None, :], N_EXPERTS)
    next_e = jnp.min(later, axis=1)
    next_e = jnp.where(next_e < N_EXPERTS, next_e, -1)
    slot = jnp.sum(jnp.where(of_block, slot_e[None, :], 0), axis=1).astype(jnp.int32)
    next_expert = jnp.sum(jnp.where(of_block, next_e[None, :], 0), axis=1).astype(jnp.int32)
    outs = _experts(be, nused.reshape(1), nvalid, slot, next_expert, xs, w_up[0],
                    b_up[0][:, None, :], w_down[0], b_down[0][:, None, :])
    g_final = norm_final_g.reshape(1, d)
    y_p = _combine(dest_p, x1_p, mf_p, g_final, outs)
    y_s = _combine(dest_s, x1_s, mf_s, g_final, outs)

    y_prompt = y_p.reshape(bsz, seq, d)
    y_sample = y_s.reshape(n_new, nseq, d).transpose(1, 0, 2)
    conv_state_prompt = cst_p[None]
    glu_new = glu_s.transpose(1, 0, 2)
    conv_state_sample = jnp.concatenate([cache_conv[0][:, n_new:], glu_new], axis=1)[None]
    chunk_v_sample = v_s.transpose(1, 0, 2)[None]
    return (y_prompt, y_sample, conv_state_prompt, conv_state_sample, chunk_v_sample)
```

```python
import functools

import jax
import jax.numpy as jnp
from jax import lax
from jax.experimental import pallas as pl
from jax.experimental.pallas import tpu as pltpu

D_MODEL = 1024
CHUNK = 128
A_GROUPS = 8
CONV_WIDTH = 31
HIST = 32
N_EXPERTS = 32
TOP_K = 4
D_EXPERT = 1024
SWIGLU_LIMIT = 7.0
SWIGLU_ALPHA = 1.702
EPS = 1e-5
LANES = 128
SUBLANES = 8
ROW_TILES = D_MODEL // LANES
TT = 256
CONV_ROWS = 64
SB = 32
BLK = 512
SUB = 256
VMEM_LIMIT = 56 * 1024 * 1024

_F32 = jnp.float32
_BF16 = jnp.bfloat16
_INV_SQRT2 = 0.7071067811865476


def _rms(x, g):
    return x * lax.rsqrt(jnp.mean(x * x, axis=-1, keepdims=True) + EPS) * g


def _ln(x, g, b):
    mu = jnp.mean(x, axis=-1, keepdims=True)
    xc = x - mu
    var = jnp.mean(xc * xc, axis=-1, keepdims=True)
    return xc * lax.rsqrt(var + EPS) * g + b


def _gelu(x):
    return 0.5 * x * (1.0 + lax.erf(x * _INV_SQRT2))


def _sigmoid(x):
    return 1.0 / (1.0 + jnp.exp(-x))


def _dot(a, b):
    return jnp.dot(a, b, preferred_element_type=_F32)


def _store_row_tiles(ref, lead, n, val, start=0):
    for j in range(ROW_TILES):
        ref[lead + (pl.ds(start + j, n, stride=ROW_TILES), slice(None))] = (
            val[:, j * LANES:(j + 1) * LANES])


def _load_row_tiles(ref, start, n):
    return jnp.concatenate(
        [ref[pl.ds(start + j, n, stride=ROW_TILES), :] for j in range(ROW_TILES)], axis=1)


def _route(h2, wr_ref, br_ref, carry_ref):
    n = h2.shape[0]
    h_hi = h2.astype(_BF16)
    h_lo = (h2 - h_hi.astype(_F32)).astype(_BF16)
    by_hi = _dot(h_hi, wr_ref[...])
    logits = (by_hi[:, :LANES] + by_hi[:, LANES:] + _dot(h_lo, wr_ref[:, :LANES])) + br_ref[...]
    lane = lax.broadcasted_iota(jnp.int32, (n, LANES), 1)
    l = jnp.where(lane < N_EXPERTS, logits, -jnp.inf)
    onehots, vals, idxs = [], [], []
    for _ in range(TOP_K):
        m = jnp.max(l, axis=-1, keepdims=True)
        idx = jnp.min(jnp.where(l == m, lane, LANES), axis=-1, keepdims=True)
        oh = lane == idx
        onehots.append(oh)
        vals.append(m)
        idxs.append(idx)
        l = jnp.where(oh, -jnp.inf, l)
    exps = [jnp.exp(v - vals[0]) for v in vals]
    denom = exps[0] + exps[1] + exps[2] + exps[3]
    gates = [e / denom for e in exps]

    oh_all = jnp.zeros((n, LANES), _F32)
    for oh in onehots:
        oh_all = oh_all + jnp.where(oh, 1.0, 0.0)
    ri = lax.broadcasted_iota(jnp.int32, (n, n), 0)
    ci = lax.broadcasted_iota(jnp.int32, (n, n), 1)
    tri = jnp.where(ci < ri, 1.0, 0.0).astype(_BF16)
    base = _dot(tri, oh_all.astype(_BF16)) + carry_ref[0:1, :]
    ranks = [jnp.sum(jnp.where(oh, base, 0.0), axis=-1, keepdims=True) for oh in onehots]
    carry_ref[...] = carry_ref[...] + jnp.sum(oh_all, axis=0, keepdims=True)

    meta_i = jnp.zeros((n, LANES), jnp.int32)
    meta_f = jnp.zeros((n, LANES), _F32)
    for k in range(TOP_K):
        meta_i = jnp.where(lane == k, idxs[k], meta_i)
        meta_i = jnp.where(lane == TOP_K + k, ranks[k].astype(jnp.int32), meta_i)
        meta_f = jnp.where(lane == k, gates[k], meta_f)
    return meta_i, meta_f


def _mixer_prompt_kernel(x_ref, gmix_ref, win_ref, lnvg_ref, lnvb_ref, wsp_ref, bsp_ref, wa_ref,
                         cw_ref, cb_ref, lncg_ref, lncb_ref, wb_ref, bb_ref, bga_ref, bgb_ref,
                         wout_ref, gffn_ref, wr_ref, br_ref,
                         x1_ref, h2_ref, mi_ref, mf_ref, cnt_ref, cst_ref,
                         cbuf, carry):
    b = pl.program_id(0)
    t = pl.program_id(1)

    @pl.when((b == 0) & (t == 0))
    def _():
        carry[...] = jnp.zeros_like(carry)

    @pl.when(t == 0)
    def _():
        cbuf[0:HIST, :] = jnp.zeros((HIST, D_MODEL), _F32)
        cbuf[HIST + TT:, :] = jnp.zeros((SUBLANES, D_MODEL), _F32)

    x = x_ref[...]
    hb = _rms(x, gmix_ref[...]).astype(_BF16)

    def proj(s):
        return _dot(hb, win_ref[:, s * D_MODEL:(s + 1) * D_MODEL])

    glu = proj(2) * _sigmoid(proj(3))
    cbuf[HIST:HIST + TT, :] = glu
    off = HIST - (CONV_WIDTH - 1)
    span = HIST + CONV_ROWS + SUBLANES
    conv_cols = []
    other, half, pieces = (0, 1, 4, 5), D_MODEL // 2, []
    order_after = None
    for c in range(D_MODEL // LANES):
        cols = slice(c * LANES, (c + 1) * LANES)
        accs = []
        for r in range(TT // CONV_ROWS):
            acc = jnp.zeros((CONV_ROWS, LANES), _F32) + cb_ref[:, cols]
            window = cbuf[r * CONV_ROWS:r * CONV_ROWS + span, cols]
            if order_after is not None:
                zero = pltpu.bitcast(lax.shift_right_logical(
                    pltpu.bitcast(order_after, jnp.uint32), jnp.uint32(32)), _F32)
                window = window + jnp.concatenate([zero] * (span // SUBLANES), axis=0)
            for sh in range(SUBLANES):
                shifted = window if sh == 0 else pltpu.roll(window, span - sh, axis=0)
                for a in range(HIST // SUBLANES + 1):
                    k = a * SUBLANES + sh - off
                    if 0 <= k < CONV_WIDTH:
                        acc = acc + (cw_ref[k:k + 1, cols]
                                     * shifted[a * SUBLANES:a * SUBLANES + CONV_ROWS, :])
            accs.append(acc)
            order_after = acc[:SUBLANES, :]
        conv_cols.append(jnp.concatenate(accs, axis=0))
        lo = other[c // 2] * D_MODEL + (c % 2) * half
        pieces.append(_dot(hb, win_ref[:, lo:lo + half]))
    conv = jnp.concatenate(conv_cols, axis=1)
    p_u, p_v, p_ga, p_gb = [jnp.concatenate(pieces[2 * i:2 * i + 2], axis=1) for i in range(4)]
    cbuf[0:HIST, :] = cbuf[TT:TT + HIST, :]
    cn = _ln(conv, lncg_ref[...], lncb_ref[...])
    y_b = _dot((cn * _sigmoid(cn)).astype(_BF16), wb_ref[...]) + bb_ref[...]

    u = _gelu(p_u)
    v = _ln(_gelu(p_v), lnvg_ref[...], lnvb_ref[...])
    ri = lax.broadcasted_iota(jnp.int32, (CHUNK, CHUNK), 0)
    ci = lax.broadcasted_iota(jnp.int32, (CHUNK, CHUNK), 1)
    s_rows = []
    for c in range(TT // CHUNK):
        s_cols = []
        for g in range(A_GROUPS):
            wm = jnp.where(ci <= ri, wsp_ref[g], 0.0).astype(_BF16)
            vb = v[c * CHUNK:(c + 1) * CHUNK, g * LANES:(g + 1) * LANES].astype(_BF16)
            s_cols.append(_dot(wm, vb))
        s_rows.append(jnp.concatenate(s_cols, axis=1) + bsp_ref[...])
    s = jnp.concatenate(s_rows, axis=0)
    y_a = _dot((u * s).astype(_BF16), wa_ref[...])

    merged = _sigmoid(p_ga + bga_ref[...]) * y_a + _sigmoid(p_gb + bgb_ref[...]) * y_b
    x1 = x + _dot(merged.astype(_BF16), wout_ref[...])
    x1_ref[...] = x1

    h2 = _rms(x1, gffn_ref[...])
    _store_row_tiles(h2_ref, (), TT, h2)
    meta_i, meta_f = _route(h2, wr_ref, br_ref, carry)
    mi_ref[...] = meta_i
    mf_ref[...] = meta_f
    cnt_ref[...] = carry[...]

    @pl.when(t == pl.num_programs(1) - 1)
    def _():
        cst_ref[...] = cbuf[HIST - (CONV_WIDTH - 1):HIST, :]


def _full(shape):
    return pl.BlockSpec(shape, lambda *_: (0,) * len(shape))


def _mixer_prompt(x, w):
    bsz, seq, d = x.shape
    n_tok = bsz * seq
    nt = seq // TT
    tok = lambda b, t: (b * nt + t, 0)
    weights = (w["gmix"], w["win"], w["lnvg"], w["lnvb"], w["wsp"], w["bsp"], w["wa"], w["cw"],
               w["cb"], w["lncg"], w["lncb"], w["wb"], w["bb"], w["bga"], w["bgb"], w["wout"],
               w["gffn"], w["wr"], w["br"])
    return pl.pallas_call(
        _mixer_prompt_kernel,
        grid=(bsz, nt),
        in_specs=[pl.BlockSpec((None, TT, d), lambda b, t: (b, t, 0))]
                 + [_full(a.shape) for a in weights],
        out_specs=[
            pl.BlockSpec((TT, d), tok),
            pl.BlockSpec((TT * ROW_TILES, LANES), tok),
            pl.BlockSpec((TT, LANES), tok),
            pl.BlockSpec((TT, LANES), tok),
            _full((SUBLANES, LANES)),
            pl.BlockSpec((None, CONV_WIDTH - 1, d), lambda b, t: (b, 0, 0)),
        ],
        out_shape=[
            jax.ShapeDtypeStruct((n_tok, d), _F32),
            jax.ShapeDtypeStruct((n_tok * ROW_TILES, LANES), _F32),
            jax.ShapeDtypeStruct((n_tok, LANES), jnp.int32),
            jax.ShapeDtypeStruct((n_tok, LANES), _F32),
            jax.ShapeDtypeStruct((SUBLANES, LANES), _F32),
            jax.ShapeDtypeStruct((bsz, CONV_WIDTH - 1, d), _F32),
        ],
        scratch_shapes=[pltpu.VMEM((HIST + TT + SUBLANES, d), _F32),
                        pltpu.VMEM((SUBLANES, LANES), _F32)],
        compiler_params=pltpu.CompilerParams(
            dimension_semantics=("arbitrary", "arbitrary"), vmem_limit_bytes=VMEM_LIMIT),
        name="mixer_prompt",
    )(x, *weights)


def _mixer_sample_kernel(x_ref, cache_ref, cnt0_ref, gmix_ref, win_ref, lnvg_ref, lnvb_ref,
                         wcs_ref, bcs_ref, wa_ref, cw_ref, cb_ref, lncg_ref, lncb_ref, wb_ref,
                         bb_ref, bga_ref, bgb_ref, wout_ref, gffn_ref, wr_ref, br_ref,
                         x1_ref, h2_ref, mi_ref, mf_ref, cnt_ref, cst_ref, v_ref,
                         carry):
    nt = x_ref.shape[0]
    n = nt * SB

    @pl.when(pl.program_id(0) == 0)
    def _():
        carry[...] = cnt0_ref[...]

    x = x_ref[...].reshape(n, D_MODEL)
    hb = _rms(x, gmix_ref[...]).astype(_BF16)

    def proj(s):
        return _dot(hb, win_ref[:, s * D_MODEL:(s + 1) * D_MODEL])

    u = _gelu(proj(0))
    v = _ln(_gelu(proj(1)), lnvg_ref[...], lnvb_ref[...])
    v_ref[...] = v.reshape(nt, SB, D_MODEL)
    s_rows = []
    for i in range(nt):
        acc = jnp.zeros((SB, D_MODEL), _F32) + bcs_ref[i:i + 1, :]
        for j in range(i + 1):
            acc = acc + wcs_ref[i * nt + j:i * nt + j + 1, :] * v[j * SB:(j + 1) * SB, :]
        s_rows.append(acc)
    s = jnp.concatenate(s_rows, axis=0)
    y_a = _dot((u * s).astype(_BF16), wa_ref[...])

    glu = proj(2) * _sigmoid(proj(3))
    hist = CONV_WIDTH - 1
    conv_rows = []
    for i in range(nt):
        acc = jnp.zeros((SB, D_MODEL), _F32) + cb_ref[...]
        for k in range(CONV_WIDTH):
            p = i + k
            src = cache_ref[:, p, :] if p < hist else glu[(p - hist) * SB:(p - hist + 1) * SB, :]
            acc = acc + cw_ref[k:k + 1, :] * src
        conv_rows.append(acc)
    cst_ref[:, 0:hist - nt, :] = cache_ref[:, nt:hist, :]
    for i in range(nt):
        cst_ref[:, hist - nt + i, :] = glu[i * SB:(i + 1) * SB, :]
    cn = _ln(jnp.concatenate(conv_rows, axis=0), lncg_ref[...], lncb_ref[...])
    y_b = _dot((cn * _sigmoid(cn)).astype(_BF16), wb_ref[...]) + bb_ref[...]

    merged = (_sigmoid(proj(4) + bga_ref[...]) * y_a + _sigmoid(proj(5) + bgb_ref[...]) * y_b)
    x1 = x + _dot(merged.astype(_BF16), wout_ref[...])
    x1_ref[...] = x1.reshape(nt, SB, D_MODEL)

    h2 = _rms(x1, gffn_ref[...])
    for i in range(nt):
        _store_row_tiles(h2_ref, (i,), SB, h2[i * SB:(i + 1) * SB, :])
    meta_i, meta_f = _route(h2, wr_ref, br_ref, carry)
    mi_ref[...] = meta_i.reshape(nt, SB, LANES)
    mf_ref[...] = meta_f.reshape(nt, SB, LANES)
    cnt_ref[...] = carry[...]


def _mixer_sample(x_t, cache, cnt0, w):
    nt, nseq, d = x_t.shape
    hist = cache.shape[1]
    per_seq = lambda i: (i, 0, 0)
    weights = (w["gmix"], w["win"], w["lnvg"], w["lnvb"], w["wcs"], w["bcs"], w["wa"], w["cw"],
               w["cb"], w["lncg"], w["lncb"], w["wb"], w["bb"], w["bga"], w["bgb"], w["wout"],
               w["gffn"], w["wr"], w["br"])
    seqs = lambda i: (0, i, 0)
    return pl.pallas_call(
        _mixer_sample_kernel,
        grid=(nseq // SB,),
        in_specs=[pl.BlockSpec((nt, SB, d), seqs), pl.BlockSpec((SB, hist, d), per_seq),
                  _full(cnt0.shape)] + [_full(a.shape) for a in weights],
        out_specs=[
            pl.BlockSpec((nt, SB, d), seqs),
            pl.BlockSpec((nt, SB * ROW_TILES, LANES), seqs),
            pl.BlockSpec((nt, SB, LANES), seqs),
            pl.BlockSpec((nt, SB, LANES), seqs),
            _full((SUBLANES, LANES)),
            pl.BlockSpec((SB, hist, d), per_seq),
            pl.BlockSpec((nt, SB, d), seqs),
        ],
        out_shape=[
            jax.ShapeDtypeStruct((nt, nseq, d), _F32),
            jax.ShapeDtypeStruct((nt, nseq * ROW_TILES, LANES), _F32),
            jax.ShapeDtypeStruct((nt, nseq, LANES), jnp.int32),
            jax.ShapeDtypeStruct((nt, nseq, LANES), _F32),
            jax.ShapeDtypeStruct((SUBLANES, LANES), _F32),
            jax.ShapeDtypeStruct((nseq, hist, d), _F32),
            jax.ShapeDtypeStruct((nt, nseq, d), _F32),
        ],
        scratch_shapes=[pltpu.VMEM((SUBLANES, LANES), _F32)],
        compiler_params=pltpu.CompilerParams(
            dimension_semantics=("arbitrary",), vmem_limit_bytes=VMEM_LIMIT),
        name="mixer_sample",
    )(x_t, cache, cnt0, *weights)


def _row_tile(ref, row):
    return ref.at[pl.ds(pl.multiple_of(row * ROW_TILES, ROW_TILES), ROW_TILES), :]


def _dispatch_kernel(n_first, n_blk, pad_start_ref, pad_len_ref, nused_ref, dest_ref, h2a_ref,
                     h2b_ref, xs_hbm, zbuf, sem, zsem):
    i = pl.program_id(0)

    def fill(start):
        def go(copy):
            copy.start() if start else copy.wait()

        def per_expert(e, carry):
            first, length = pad_start_ref[e], pad_len_ref[e]
            for s in range(1, BLK.bit_length()):
                p = BLK >> s

                @pl.when((length & p) != 0)
                def _():
                    row = first + (length & (-2 * p))
                    go(pltpu.make_async_copy(
                        zbuf.at[pl.ds(0, p * ROW_TILES), :],
                        xs_hbm.at[pl.ds(pl.multiple_of(row * ROW_TILES, ROW_TILES), p * ROW_TILES), :],
                        zsem))
            return carry

        def per_tail_block(b, carry):
            go(pltpu.make_async_copy(
                zbuf, xs_hbm.at[pl.ds(pl.multiple_of(b * BLK * ROW_TILES, ROW_TILES),
                                      BLK * ROW_TILES), :], zsem))
            return carry

        lax.fori_loop(0, N_EXPERTS, per_expert, 0)
        lax.fori_loop(nused_ref[0], n_blk, per_tail_block, 0)

    @pl.when(i == 0)
    def _():
        zbuf[...] = jnp.zeros_like(zbuf)
        fill(True)

    def scatter(h2_ref):
        def issue(t, carry):
            src = _row_tile(h2_ref, t)
            for k in range(TOP_K):
                pltpu.make_async_copy(src, _row_tile(xs_hbm, dest_ref[0, 0, t * TOP_K + k]),
                                      sem).start(priority=k % 2)
            return carry

        lax.fori_loop(0, TT, issue, 0)
        for _ in range(TOP_K):
            pltpu.make_async_copy(h2_ref, xs_hbm.at[pl.ds(0, TT * ROW_TILES), :], sem).wait()

    @pl.when(i < n_first)
    def _():
        scatter(h2a_ref)

    @pl.when(i >= n_first)
    def _():
        scatter(h2b_ref)

    @pl.when(i == 0)
    def _():
        fill(False)


def _dispatch(pad_start, pad_len, nused, dest, h2_a, h2_b, n_blk):
    n_tiles = dest.shape[0]
    n_first = h2_a.shape[0] // (TT * ROW_TILES)
    tile = (TT * ROW_TILES, LANES)
    grid_spec = pltpu.PrefetchScalarGridSpec(
        num_scalar_prefetch=3,
        grid=(n_tiles,),
        in_specs=[
            pl.BlockSpec((1, 1, TT * TOP_K), lambda i, *_: (i, 0, 0), memory_space=pltpu.SMEM),
            pl.BlockSpec(tile, lambda i, *_: (jnp.minimum(i, n_first - 1), 0)),
            pl.BlockSpec(tile, lambda i, *_: (jnp.maximum(i - n_first, 0), 0)),
        ],
        out_specs=pl.BlockSpec(memory_space=pl.ANY),
        scratch_shapes=[pltpu.VMEM((BLK * ROW_TILES, LANES), _F32), pltpu.SemaphoreType.DMA,
                        pltpu.SemaphoreType.DMA],
    )
    return pl.pallas_call(
        functools.partial(_dispatch_kernel, n_first, n_blk),
        grid_spec=grid_spec,
        out_shape=jax.ShapeDtypeStruct((n_blk * BLK * ROW_TILES, LANES), _F32),
        compiler_params=pltpu.CompilerParams(
            dimension_semantics=("arbitrary",), has_side_effects=True),
        name="dispatch",
    )(pad_start, pad_len, nused, dest, h2_a, h2_b)


def _experts_kernel(be_ref, nused_ref, nvalid_ref, slot_ref, next_ref, xs_ref, bup_ref, bdn_ref,
                    wup_hbm, wdn_hbm, out_ref, wup_buf, wdn_buf, sems):
    del nused_ref
    i = pl.program_id(0)
    expert = be_ref[i]
    slot = slot_ref[i]
    first_of_run = (i == 0) | (expert != be_ref[jnp.maximum(i - 1, 0)])

    def weight_copies(e, s):
        return (pltpu.make_async_copy(wup_hbm.at[e], wup_buf.at[s], sems.at[s]),
                pltpu.make_async_copy(wdn_hbm.at[e], wdn_buf.at[s], sems.at[s]))

    @pl.when(i == 0)
    def _():
        for copy in weight_copies(expert, slot):
            copy.start()

    @pl.when(first_of_run & (next_ref[i] >= 0))
    def _():
        for copy in weight_copies(next_ref[i], 1 - slot):
            copy.start()

    @pl.when(first_of_run)
    def _():
        for copy in weight_copies(expert, slot):
            copy.wait()

    wup = wup_buf.at[slot]
    wdn = wdn_buf.at[slot]

    def first_rows(n):
        if n:
            x = _load_row_tiles(xs_ref, 0, n)
            gate = _dot(x, wup[:, :D_EXPERT]) + bup_ref[:, :D_EXPERT]
            up = _dot(x, wup[:, D_EXPERT:]) + bup_ref[:, D_EXPERT:]
            gate = jnp.minimum(gate, SWIGLU_LIMIT)
            up = jnp.clip(up, -SWIGLU_LIMIT, SWIGLU_LIMIT)
            act = (up + 1.0) * (gate * _sigmoid(SWIGLU_ALPHA * gate))
            o = _dot(act, wdn[...]) + bdn_ref[...]
            _store_row_tiles(out_ref, (), n, o)
        if n < BLK:
            out_ref[n * ROW_TILES:, :] = jnp.zeros(((BLK - n) * ROW_TILES, LANES), _F32)

    n_valid = nvalid_ref[i]
    for live in range(BLK // SUB + 1):
        lo, hi = (live - 1) * SUB, live * SUB

        @pl.when((n_valid > lo) & (n_valid <= hi) if live else n_valid <= 0)
        def _():
            first_rows(live * SUB)


def _experts(block_expert, nused, nvalid, slot, next_expert, xs, w_up, b_up, w_down, b_down):
    n_blk = block_expert.shape[0]
    rows = lambda i, be, nu, *_: (jnp.minimum(i, nu[0] - 1), 0)
    expert = lambda i, be, *_: (be[i], 0, 0)
    d, f2 = w_up.shape[1], w_up.shape[2]
    grid_spec = pltpu.PrefetchScalarGridSpec(
        num_scalar_prefetch=5,
        grid=(n_blk,),
        in_specs=[
            pl.BlockSpec((BLK * ROW_TILES, LANES), rows),
            pl.BlockSpec((None, 1, f2), expert),
            pl.BlockSpec((None, 1, d), expert),
            pl.BlockSpec(memory_space=pl.ANY),
            pl.BlockSpec(memory_space=pl.ANY),
        ],
        out_specs=pl.BlockSpec((BLK * ROW_TILES, LANES), lambda i, *_: (i, 0)),
        scratch_shapes=[pltpu.VMEM((2, d, f2), _F32), pltpu.VMEM((2, w_down.shape[1], d), _F32),
                        pltpu.SemaphoreType.DMA((2,))],
    )
    return pl.pallas_call(
        _experts_kernel,
        grid_spec=grid_spec,
        out_shape=jax.ShapeDtypeStruct(xs.shape, _F32),
        compiler_params=pltpu.CompilerParams(
            dimension_semantics=("arbitrary",), vmem_limit_bytes=VMEM_LIMIT),
        name="experts",
    )(block_expert, nused, nvalid, slot, next_expert, xs, b_up, b_down, w_up, w_down)


def _combine_kernel(n, dest_ref, dest_next_ref, x1_ref, mf_ref, gfin_ref, outs_hbm, y_ref, buf,
                    sems):
    i = pl.program_id(0)

    def gather(d_ref, slot):
        def issue(t, carry):
            for k in range(TOP_K):
                dst = buf.at[slot, pl.ds(pl.multiple_of((k * TT + t) * ROW_TILES, ROW_TILES),
                                         ROW_TILES), :]
                pltpu.make_async_copy(_row_tile(outs_hbm, d_ref[0, 0, t * TOP_K + k]), dst,
                                      sems.at[slot]).start(priority=k % 2)
            return carry

        lax.fori_loop(0, TT, issue, 0)

    slot = i % 2

    @pl.when(i == 0)
    def _():
        gather(dest_ref, 0)

    @pl.when(i + 1 < n)
    def _():
        gather(dest_next_ref, 1 - slot)

    pltpu.make_async_copy(outs_hbm.at[pl.ds(0, buf.shape[1]), :], buf.at[slot], sems.at[slot]).wait()

    y = x1_ref[...]
    gates = mf_ref[...]
    rows = buf.at[slot]
    for k in range(TOP_K):
        y = y + gates[:, k:k + 1] * _load_row_tiles(rows, k * TT * ROW_TILES, TT)
    y_ref[...] = _rms(y, gfin_ref[...])


def _combine(dest, x1, meta_f, g_final, outs):
    n_tok, d = x1.shape
    n_tiles = n_tok // TT
    tok = lambda i: (i, 0)
    return pl.pallas_call(
        functools.partial(_combine_kernel, n_tiles),
        grid=(n_tiles,),
        in_specs=[
            pl.BlockSpec((1, 1, TT * TOP_K), lambda i: (i, 0, 0), memory_space=pltpu.SMEM),
            pl.BlockSpec((1, 1, TT * TOP_K), lambda i: (jnp.minimum(i + 1, n_tiles - 1), 0, 0),
                         memory_space=pltpu.SMEM),
            pl.BlockSpec((TT, d), tok),
            pl.BlockSpec((TT, LANES), tok),
            _full(g_final.shape),
            pl.BlockSpec(memory_space=pl.ANY),
        ],
        out_specs=pl.BlockSpec((TT, d), tok),
        out_shape=jax.ShapeDtypeStruct((n_tok, d), _F32),
        scratch_shapes=[pltpu.VMEM((2, TOP_K * TT * ROW_TILES, LANES), _F32),
                        pltpu.SemaphoreType.DMA((2,))],
        compiler_params=pltpu.CompilerParams(
            dimension_semantics=("arbitrary",), vmem_limit_bytes=VMEM_LIMIT),
        name="combine",
    )(dest, dest, x1, meta_f, g_final, outs)


def _prep_weights(norm_mix_g, w_in, ln_v_g, ln_v_b, w_spatial, b_spatial, w_branch_a, conv_w,
                  conv_b, ln_conv_g, ln_conv_b, w_branch_b, b_branch_b, b_gate_a, b_gate_b, w_out,
                  norm_ffn_g, w_router, b_router, n_new):
    row = lambda a: a.reshape(1, -1)
    pad = LANES - N_EXPERTS
    wr = jnp.pad(w_router, ((0, 0), (0, pad)))
    wr_hi = wr.astype(_BF16)
    wr_lo = (wr - wr_hi.astype(_F32)).astype(_BF16)
    wcs = jnp.repeat(w_spatial[:, :n_new, :n_new].transpose(1, 2, 0).reshape(n_new * n_new, A_GROUPS),
                     LANES, axis=1)
    return dict(
        gmix=row(norm_mix_g), win=w_in.astype(_BF16), lnvg=row(ln_v_g), lnvb=row(ln_v_b),
        wsp=w_spatial, bsp=jnp.repeat(b_spatial.T, LANES, axis=1), wcs=wcs,
        bcs=jnp.repeat(b_spatial[:, :n_new].T, LANES, axis=1),
        wa=w_branch_a.astype(_BF16), cw=conv_w, cb=row(conv_b), lncg=row(ln_conv_g),
        lncb=row(ln_conv_b), wb=w_branch_b.astype(_BF16), bb=row(b_branch_b), bga=row(b_gate_a),
        bgb=row(b_gate_b), wout=w_out.astype(_BF16), gffn=row(norm_ffn_g),
        wr=jnp.concatenate([wr_hi, wr_lo], axis=1), br=jnp.pad(row(b_router), ((0, 0), (0, pad))))


def kernel(x_prompt, x_sample, cache_conv, norm_mix_g, w_in, ln_v_g, ln_v_b, w_spatial, b_spatial, w_branch_a, conv_w, conv_b, ln_conv_g, ln_conv_b, w_branch_b, b_branch_b, b_gate_a, b_gate_b, w_out, norm_ffn_g, w_router, b_router, w_up, b_up, w_down, b_down, norm_final_g):
    assert w_in.shape[0] == 1, "single trunk layer"
    bsz, seq, d = x_prompt.shape
    nseq, n_new, _ = x_sample.shape
    assert d == D_MODEL and seq % TT == 0 and TT % CHUNK == 0 and nseq % SB == 0
    assert n_new <= CHUNK and (nseq * n_new) % TT == 0
    w = _prep_weights(norm_mix_g[0], w_in[0], ln_v_g[0], ln_v_b[0], w_spatial[0], b_spatial[0],
                      w_branch_a[0], conv_w[0], conv_b[0], ln_conv_g[0], ln_conv_b[0],
                      w_branch_b[0], b_branch_b[0], b_gate_a[0], b_gate_b[0], w_out[0],
                      norm_ffn_g[0], w_router[0], b_router[0], n_new)

    x1_p, h2_p, mi_p, mf_p, cnt_p, cst_p = _mixer_prompt(x_prompt, w)
    x_t = x_sample.transpose(1, 0, 2)
    x1_s, h2_s, mi_s, mf_s, cnt, cst_s, v_s = _mixer_sample(x_t, cache_conv[0], cnt_p, w)
    n_p, n_s = bsz * seq, nseq * n_new
    x1_s = x1_s.reshape(n_s, d)
    h2_s = h2_s.reshape(n_s * ROW_TILES, LANES)
    mi_s = mi_s.reshape(n_s, LANES)
    mf_s = mf_s.reshape(n_s, LANES)

    counts = cnt[0, :N_EXPERTS].astype(jnp.int32)
    padded = ((counts + BLK - 1) // BLK) * BLK
    pend = jnp.cumsum(padded)
    pstart = pend - padded
    n_assign = (n_p + n_s) * TOP_K
    n_blk = -(-(n_assign + N_EXPERTS * (BLK - 1)) // BLK)
    nused = (pend[-1] // BLK).astype(jnp.int32)
    blk_ids = jnp.arange(n_blk, dtype=jnp.int32)
    first_row = jnp.minimum(blk_ids, nused - 1) * BLK
    be = jnp.sum((pend[None, :] <= first_row[:, None]).astype(jnp.int32), axis=1)
    be = jnp.minimum(be, N_EXPERTS - 1)
    experts_row = jnp.arange(N_EXPERTS, dtype=jnp.int32)

    def dest_of(mi):
        sel = mi[:, :TOP_K, None] == experts_row[None, None, :]
        dest = jnp.sum(jnp.where(sel, pstart[None, None, :], 0), axis=-1) + mi[:, TOP_K:2 * TOP_K]
        return dest.reshape(-1, 1, TT * TOP_K)

    dest_p, dest_s = dest_of(mi_p), dest_of(mi_s)
    xs = _dispatch(pstart + counts, padded - counts, nused.reshape(1),
                   jnp.concatenate([dest_p, dest_s], axis=0), h2_p, h2_s, n_blk)
    of_block = be[:, None] == experts_row[None, :]
    row_in_expert = blk_ids * BLK - jnp.sum(jnp.where(of_block, pstart[None, :], 0), axis=1)
    nvalid = jnp.clip(jnp.sum(jnp.where(of_block, counts[None, :], 0), axis=1) - row_in_expert,
                      0, BLK)
    nvalid = jnp.where(blk_ids < nused, nvalid, 0).astype(jnp.int32)
    used = counts > 0
    slot_e = (jnp.cumsum(used.astype(jnp.int32)) - 1) % 2
    later = jnp.where(used[None, :] & (experts_row[None, :] > experts_row[:, None]),
                      experts_row[None, :], N_EXPERTS)
    next_e = jnp.min(later, axis=1)
    next_e = jnp.where(next_e < N_EXPERTS, next_e, -1)
    slot = jnp.sum(jnp.where(of_block, slot_e[None, :], 0), axis=1).astype(jnp.int32)
    next_expert = jnp.sum(jnp.where(of_block, next_e[None, :], 0), axis=1).astype(jnp.int32)
    outs = _experts(be, nused.reshape(1), nvalid, slot, next_expert, xs, w_up[0],
                    b_up[0][:, None, :], w_down[0], b_down[0][:, None, :])
    g_final = norm_final_g.reshape(1, d)
    y_p = _combine(dest_p, x1_p, mf_p, g_final, outs)
    y_s = _combine(dest_s, x1_s, mf_s, g_final, outs)

    y_prompt = y_p.reshape(bsz, seq, d)
    y_sample = y_s.reshape(n_new, nseq, d).transpose(1, 0, 2)
    conv_state_prompt = cst_p[None]
    conv_state_sample = cst_s[None]
    chunk_v_sample = v_s.transpose(1, 0, 2)[None]
    return (y_prompt, y_sample, conv_state_prompt, conv_state_sample, chunk_v_sample)
```

```python
import functools

import jax
import jax.numpy as jnp
from jax import lax
from jax.experimental import pallas as pl
from jax.experimental.pallas import tpu as pltpu

D_MODEL = 1024
CHUNK = 128
A_GROUPS = 8
CONV_WIDTH = 31
HIST = 32
N_EXPERTS = 32
TOP_K = 4
D_EXPERT = 1024
SWIGLU_LIMIT = 7.0
SWIGLU_ALPHA = 1.702
EPS = 1e-5
LANES = 128
SUBLANES = 8
ROW_TILES = D_MODEL // LANES
TT = 256
TD = 512
CONV_ROWS = 64
SB = 32
BLK = 512
SUB = 256
VMEM_LIMIT = 56 * 1024 * 1024

_F32 = jnp.float32
_BF16 = jnp.bfloat16
_INV_SQRT2 = 0.7071067811865476


def _rms(x, g):
    return x * lax.rsqrt(jnp.mean(x * x, axis=-1, keepdims=True) + EPS) * g


def _ln(x, g, b):
    mu = jnp.mean(x, axis=-1, keepdims=True)
    xc = x - mu
    var = jnp.mean(xc * xc, axis=-1, keepdims=True)
    return xc * lax.rsqrt(var + EPS) * g + b


def _gelu(x):
    return 0.5 * x * (1.0 + lax.erf(x * _INV_SQRT2))


def _sigmoid(x):
    return 1.0 / (1.0 + jnp.exp(-x))


def _dot(a, b):
    return jnp.dot(a, b, preferred_element_type=_F32)


def _store_row_tiles(ref, lead, n, val, start=0):
    for j in range(ROW_TILES):
        ref[lead + (pl.ds(start + j, n, stride=ROW_TILES), slice(None))] = (
            val[:, j * LANES:(j + 1) * LANES])


def _load_row_tiles(ref, start, n):
    return jnp.concatenate(
        [ref[pl.ds(start + j, n, stride=ROW_TILES), :] for j in range(ROW_TILES)], axis=1)


def _route(h2, wr_ref, br_ref, carry_ref):
    n = h2.shape[0]
    h_hi = h2.astype(_BF16)
    h_lo = (h2 - h_hi.astype(_F32)).astype(_BF16)
    by_hi = _dot(h_hi, wr_ref[...])
    logits = (by_hi[:, :LANES] + by_hi[:, LANES:] + _dot(h_lo, wr_ref[:, :LANES])) + br_ref[...]
    lane = lax.broadcasted_iota(jnp.int32, (n, LANES), 1)
    l = jnp.where(lane < N_EXPERTS, logits, -jnp.inf)
    onehots, vals, idxs = [], [], []
    for _ in range(TOP_K):
        m = jnp.max(l, axis=-1, keepdims=True)
        idx = jnp.min(jnp.where(l == m, lane, LANES), axis=-1, keepdims=True)
        oh = lane == idx
        onehots.append(oh)
        vals.append(m)
        idxs.append(idx)
        l = jnp.where(oh, -jnp.inf, l)
    exps = [jnp.exp(v - vals[0]) for v in vals]
    denom = exps[0] + exps[1] + exps[2] + exps[3]
    gates = [e / denom for e in exps]

    oh_all = jnp.zeros((n, LANES), _F32)
    for oh in onehots:
        oh_all = oh_all + jnp.where(oh, 1.0, 0.0)
    ri = lax.broadcasted_iota(jnp.int32, (n, n), 0)
    ci = lax.broadcasted_iota(jnp.int32, (n, n), 1)
    tri = jnp.where(ci < ri, 1.0, 0.0).astype(_BF16)
    base = _dot(tri, oh_all.astype(_BF16)) + carry_ref[0:1, :]
    ranks = [jnp.sum(jnp.where(oh, base, 0.0), axis=-1, keepdims=True) for oh in onehots]
    carry_ref[...] = carry_ref[...] + jnp.sum(oh_all, axis=0, keepdims=True)

    meta_i = jnp.zeros((n, LANES), jnp.int32)
    meta_f = jnp.zeros((n, LANES), _F32)
    for k in range(TOP_K):
        meta_i = jnp.where(lane == k, idxs[k], meta_i)
        meta_i = jnp.where(lane == TOP_K + k, ranks[k].astype(jnp.int32), meta_i)
        meta_f = jnp.where(lane == k, gates[k], meta_f)
    return meta_i, meta_f


def _mixer_prompt_kernel(x_ref, gmix_ref, win_ref, lnvg_ref, lnvb_ref, wsp_ref, bsp_ref, wa_ref,
                         cw_ref, cb_ref, lncg_ref, lncb_ref, wb_ref, bb_ref, bga_ref, bgb_ref,
                         wout_ref, gffn_ref, wr_ref, br_ref,
                         x1_ref, h2_ref, mi_ref, mf_ref, cnt_ref, cst_ref,
                         cbuf, carry):
    b = pl.program_id(0)
    t = pl.program_id(1)

    @pl.when((b == 0) & (t == 0))
    def _():
        carry[...] = jnp.zeros_like(carry)

    @pl.when(t == 0)
    def _():
        cbuf[0:HIST, :] = jnp.zeros((HIST, D_MODEL), _F32)
        cbuf[HIST + TT:, :] = jnp.zeros((SUBLANES, D_MODEL), _F32)

    x = x_ref[...]
    hb = _rms(x, gmix_ref[...]).astype(_BF16)

    def proj(s):
        return _dot(hb, win_ref[:, s * D_MODEL:(s + 1) * D_MODEL])

    glu = proj(2) * _sigmoid(proj(3))
    cbuf[HIST:HIST + TT, :] = glu
    off = HIST - (CONV_WIDTH - 1)
    span = HIST + CONV_ROWS + SUBLANES
    conv_cols = []
    other, half, pieces = (0, 1, 4, 5), D_MODEL // 2, []
    order_after = None
    for c in range(D_MODEL // LANES):
        cols = slice(c * LANES, (c + 1) * LANES)
        accs = []
        for r in range(TT // CONV_ROWS):
            acc = jnp.zeros((CONV_ROWS, LANES), _F32) + cb_ref[:, cols]
            window = cbuf[r * CONV_ROWS:r * CONV_ROWS + span, cols]
            if order_after is not None:
                zero = pltpu.bitcast(lax.shift_right_logical(
                    pltpu.bitcast(order_after, jnp.uint32), jnp.uint32(32)), _F32)
                window = window + jnp.concatenate([zero] * (span // SUBLANES), axis=0)
            for sh in range(SUBLANES):
                shifted = window if sh == 0 else pltpu.roll(window, span - sh, axis=0)
                for a in range(HIST // SUBLANES + 1):
                    k = a * SUBLANES + sh - off
                    if 0 <= k < CONV_WIDTH:
                        acc = acc + (cw_ref[k:k + 1, cols]
                                     * shifted[a * SUBLANES:a * SUBLANES + CONV_ROWS, :])
            accs.append(acc)
            order_after = acc[:SUBLANES, :]
        conv_cols.append(jnp.concatenate(accs, axis=0))
        lo = other[c // 2] * D_MODEL + (c % 2) * half
        pieces.append(_dot(hb, win_ref[:, lo:lo + half]))
    conv = jnp.concatenate(conv_cols, axis=1)
    p_u, p_v, p_ga, p_gb = [jnp.concatenate(pieces[2 * i:2 * i + 2], axis=1) for i in range(4)]
    cbuf[0:HIST, :] = cbuf[TT:TT + HIST, :]
    cn = _ln(conv, lncg_ref[...], lncb_ref[...])
    y_b = _dot((cn * _sigmoid(cn)).astype(_BF16), wb_ref[...]) + bb_ref[...]

    u = _gelu(p_u)
    v = _ln(_gelu(p_v), lnvg_ref[...], lnvb_ref[...])
    ri = lax.broadcasted_iota(jnp.int32, (CHUNK, CHUNK), 0)
    ci = lax.broadcasted_iota(jnp.int32, (CHUNK, CHUNK), 1)
    s_rows = []
    for c in range(TT // CHUNK):
        s_cols = []
        for g in range(A_GROUPS):
            wm = jnp.where(ci <= ri, wsp_ref[g], 0.0).astype(_BF16)
            vb = v[c * CHUNK:(c + 1) * CHUNK, g * LANES:(g + 1) * LANES].astype(_BF16)
            s_cols.append(_dot(wm, vb))
        s_rows.append(jnp.concatenate(s_cols, axis=1) + bsp_ref[...])
    s = jnp.concatenate(s_rows, axis=0)
    y_a = _dot((u * s).astype(_BF16), wa_ref[...])

    merged = _sigmoid(p_ga + bga_ref[...]) * y_a + _sigmoid(p_gb + bgb_ref[...]) * y_b
    x1 = x + _dot(merged.astype(_BF16), wout_ref[...])
    x1_ref[...] = x1

    h2 = _rms(x1, gffn_ref[...])
    _store_row_tiles(h2_ref, (), TT, h2)
    meta_i, meta_f = _route(h2, wr_ref, br_ref, carry)
    mi_ref[...] = meta_i
    mf_ref[...] = meta_f
    cnt_ref[...] = carry[...]

    @pl.when(t == pl.num_programs(1) - 1)
    def _():
        cst_ref[...] = cbuf[HIST - (CONV_WIDTH - 1):HIST, :]


def _full(shape):
    return pl.BlockSpec(shape, lambda *_: (0,) * len(shape))


def _mixer_prompt(x, w):
    bsz, seq, d = x.shape
    n_tok = bsz * seq
    nt = seq // TT
    tok = lambda b, t: (b * nt + t, 0)
    weights = (w["gmix"], w["win"], w["lnvg"], w["lnvb"], w["wsp"], w["bsp"], w["wa"], w["cw"],
               w["cb"], w["lncg"], w["lncb"], w["wb"], w["bb"], w["bga"], w["bgb"], w["wout"],
               w["gffn"], w["wr"], w["br"])
    return pl.pallas_call(
        _mixer_prompt_kernel,
        grid=(bsz, nt),
        in_specs=[pl.BlockSpec((None, TT, d), lambda b, t: (b, t, 0))]
                 + [_full(a.shape) for a in weights],
        out_specs=[
            pl.BlockSpec((TT, d), tok),
            pl.BlockSpec((TT * ROW_TILES, LANES), tok),
            pl.BlockSpec((TT, LANES), tok),
            pl.BlockSpec((TT, LANES), tok),
            _full((SUBLANES, LANES)),
            pl.BlockSpec((None, CONV_WIDTH - 1, d), lambda b, t: (b, 0, 0)),
        ],
        out_shape=[
            jax.ShapeDtypeStruct((n_tok, d), _F32),
            jax.ShapeDtypeStruct((n_tok * ROW_TILES, LANES), _F32),
            jax.ShapeDtypeStruct((n_tok, LANES), jnp.int32),
            jax.ShapeDtypeStruct((n_tok, LANES), _F32),
            jax.ShapeDtypeStruct((SUBLANES, LANES), _F32),
            jax.ShapeDtypeStruct((bsz, CONV_WIDTH - 1, d), _F32),
        ],
        scratch_shapes=[pltpu.VMEM((HIST + TT + SUBLANES, d), _F32),
                        pltpu.VMEM((SUBLANES, LANES), _F32)],
        compiler_params=pltpu.CompilerParams(
            dimension_semantics=("arbitrary", "arbitrary"), vmem_limit_bytes=VMEM_LIMIT),
        name="mixer_prompt",
    )(x, *weights)


def _mixer_sample_kernel(x_ref, cache_ref, cnt0_ref, gmix_ref, win_ref, lnvg_ref, lnvb_ref,
                         wcs_ref, bcs_ref, wa_ref, cw_ref, cb_ref, lncg_ref, lncb_ref, wb_ref,
                         bb_ref, bga_ref, bgb_ref, wout_ref, gffn_ref, wr_ref, br_ref,
                         x1_ref, h2_ref, mi_ref, mf_ref, cnt_ref, glu_ref, v_ref,
                         carry):
    nt = x_ref.shape[0]
    n = nt * SB

    @pl.when(pl.program_id(0) == 0)
    def _():
        carry[...] = cnt0_ref[...]

    x = x_ref[...].reshape(n, D_MODEL)
    hb = _rms(x, gmix_ref[...]).astype(_BF16)

    def proj(s):
        return _dot(hb, win_ref[:, s * D_MODEL:(s + 1) * D_MODEL])

    u = _gelu(proj(0))
    v = _ln(_gelu(proj(1)), lnvg_ref[...], lnvb_ref[...])
    v_ref[...] = v.reshape(nt, SB, D_MODEL)
    s_rows = []
    for i in range(nt):
        acc = jnp.zeros((SB, D_MODEL), _F32) + bcs_ref[i:i + 1, :]
        for j in range(i + 1):
            acc = acc + wcs_ref[i * nt + j:i * nt + j + 1, :] * v[j * SB:(j + 1) * SB, :]
        s_rows.append(acc)
    s = jnp.concatenate(s_rows, axis=0)
    y_a = _dot((u * s).astype(_BF16), wa_ref[...])

    glu = proj(2) * _sigmoid(proj(3))
    glu_ref[...] = glu.reshape(nt, SB, D_MODEL)
    hist = CONV_WIDTH - 1
    conv_rows = []
    for i in range(nt):
        acc = jnp.zeros((SB, D_MODEL), _F32) + cb_ref[...]
        for k in range(CONV_WIDTH):
            p = i + k
            src = cache_ref[p] if p < hist else glu[(p - hist) * SB:(p - hist + 1) * SB, :]
            acc = acc + cw_ref[k:k + 1, :] * src
        conv_rows.append(acc)
    cn = _ln(jnp.concatenate(conv_rows, axis=0), lncg_ref[...], lncb_ref[...])
    y_b = _dot((cn * _sigmoid(cn)).astype(_BF16), wb_ref[...]) + bb_ref[...]

    merged = (_sigmoid(proj(4) + bga_ref[...]) * y_a + _sigmoid(proj(5) + bgb_ref[...]) * y_b)
    x1 = x + _dot(merged.astype(_BF16), wout_ref[...])
    x1_ref[...] = x1.reshape(nt, SB, D_MODEL)

    h2 = _rms(x1, gffn_ref[...])
    for i in range(nt):
        _store_row_tiles(h2_ref, (i,), SB, h2[i * SB:(i + 1) * SB, :])
    meta_i, meta_f = _route(h2, wr_ref, br_ref, carry)
    mi_ref[...] = meta_i.reshape(nt, SB, LANES)
    mf_ref[...] = meta_f.reshape(nt, SB, LANES)
    cnt_ref[...] = carry[...]


def _mixer_sample(x_t, cache_t, cnt0, w):
    nt, nseq, d = x_t.shape
    hist = cache_t.shape[0]
    weights = (w["gmix"], w["win"], w["lnvg"], w["lnvb"], w["wcs"], w["bcs"], w["wa"], w["cw"],
               w["cb"], w["lncg"], w["lncb"], w["wb"], w["bb"], w["bga"], w["bgb"], w["wout"],
               w["gffn"], w["wr"], w["br"])
    seqs = lambda i: (0, i, 0)
    return pl.pallas_call(
        _mixer_sample_kernel,
        grid=(nseq // SB,),
        in_specs=[pl.BlockSpec((nt, SB, d), seqs), pl.BlockSpec((hist, SB, d), seqs),
                  _full(cnt0.shape)] + [_full(a.shape) for a in weights],
        out_specs=[
            pl.BlockSpec((nt, SB, d), seqs),
            pl.BlockSpec((nt, SB * ROW_TILES, LANES), seqs),
            pl.BlockSpec((nt, SB, LANES), seqs),
            pl.BlockSpec((nt, SB, LANES), seqs),
            _full((SUBLANES, LANES)),
            pl.BlockSpec((nt, SB, d), seqs),
            pl.BlockSpec((nt, SB, d), seqs),
        ],
        out_shape=[
            jax.ShapeDtypeStruct((nt, nseq, d), _F32),
            jax.ShapeDtypeStruct((nt, nseq * ROW_TILES, LANES), _F32),
            jax.ShapeDtypeStruct((nt, nseq, LANES), jnp.int32),
            jax.ShapeDtypeStruct((nt, nseq, LANES), _F32),
            jax.ShapeDtypeStruct((SUBLANES, LANES), _F32),
            jax.ShapeDtypeStruct((nt, nseq, d), _F32),
            jax.ShapeDtypeStruct((nt, nseq, d), _F32),
        ],
        scratch_shapes=[pltpu.VMEM((SUBLANES, LANES), _F32)],
        compiler_params=pltpu.CompilerParams(
            dimension_semantics=("arbitrary",), vmem_limit_bytes=VMEM_LIMIT),
        name="mixer_sample",
    )(x_t, cache_t, cnt0, *weights)


def _row_tile(ref, row):
    return ref.at[pl.ds(pl.multiple_of(row * ROW_TILES, ROW_TILES), ROW_TILES), :]


def _dispatch_kernel(n_first, n_blk, pad_start_ref, pad_len_ref, nused_ref, dest_ref, h2a_ref,
                     h2b_ref, xs_hbm, zbuf, sem, zsem):
    i = pl.program_id(0)

    def fill(start):
        def go(copy):
            copy.start() if start else copy.wait()

        def per_expert(e, carry):
            first, length = pad_start_ref[e], pad_len_ref[e]
            for s in range(1, BLK.bit_length()):
                p = BLK >> s

                @pl.when((length & p) != 0)
                def _():
                    row = first + (length & (-2 * p))
                    go(pltpu.make_async_copy(
                        zbuf.at[pl.ds(0, p * ROW_TILES), :],
                        xs_hbm.at[pl.ds(pl.multiple_of(row * ROW_TILES, ROW_TILES), p * ROW_TILES), :],
                        zsem))
            return carry

        def per_tail_block(b, carry):
            go(pltpu.make_async_copy(
                zbuf, xs_hbm.at[pl.ds(pl.multiple_of(b * BLK * ROW_TILES, ROW_TILES),
                                      BLK * ROW_TILES), :], zsem))
            return carry

        lax.fori_loop(0, N_EXPERTS, per_expert, 0)
        lax.fori_loop(nused_ref[0], n_blk, per_tail_block, 0)

    @pl.when(i == 0)
    def _():
        zbuf[...] = jnp.zeros_like(zbuf)
        fill(True)

    def scatter(h2_ref):
        def issue(t, carry):
            src = _row_tile(h2_ref, t)
            for k in range(TOP_K):
                pltpu.make_async_copy(src, _row_tile(xs_hbm, dest_ref[0, 0, t * TOP_K + k]),
                                      sem).start(priority=k % 2)
            return carry

        lax.fori_loop(0, TD, issue, 0)
        for _ in range(TOP_K):
            pltpu.make_async_copy(h2_ref, xs_hbm.at[pl.ds(0, TD * ROW_TILES), :], sem).wait()

    @pl.when(i < n_first)
    def _():
        scatter(h2a_ref)

    @pl.when(i >= n_first)
    def _():
        scatter(h2b_ref)

    @pl.when(i == 0)
    def _():
        fill(False)


def _dispatch(pad_start, pad_len, nused, dest, h2_a, h2_b, n_blk):
    n_tiles = dest.shape[0]
    n_first = h2_a.shape[0] // (TD * ROW_TILES)
    tile = (TD * ROW_TILES, LANES)
    grid_spec = pltpu.PrefetchScalarGridSpec(
        num_scalar_prefetch=3,
        grid=(n_tiles,),
        in_specs=[
            pl.BlockSpec((1, 1, TD * TOP_K), lambda i, *_: (i, 0, 0), memory_space=pltpu.SMEM),
            pl.BlockSpec(tile, lambda i, *_: (jnp.minimum(i, n_first - 1), 0)),
            pl.BlockSpec(tile, lambda i, *_: (jnp.maximum(i - n_first, 0), 0)),
        ],
        out_specs=pl.BlockSpec(memory_space=pl.ANY),
        scratch_shapes=[pltpu.VMEM((BLK * ROW_TILES, LANES), _F32), pltpu.SemaphoreType.DMA,
                        pltpu.SemaphoreType.DMA],
    )
    return pl.pallas_call(
        functools.partial(_dispatch_kernel, n_first, n_blk),
        grid_spec=grid_spec,
        out_shape=jax.ShapeDtypeStruct((n_blk * BLK * ROW_TILES, LANES), _F32),
        compiler_params=pltpu.CompilerParams(
            dimension_semantics=("arbitrary",), has_side_effects=True),
        name="dispatch",
    )(pad_start, pad_len, nused, dest, h2_a, h2_b)


def _experts_kernel(be_ref, nused_ref, nvalid_ref, slot_ref, next_ref, xs_ref, bup_ref, bdn_ref,
                    wup_hbm, wdn_hbm, out_ref, wup_buf, wdn_buf, sems):
    del nused_ref
    i = pl.program_id(0)
    expert = be_ref[i]
    slot = slot_ref[i]
    first_of_run = (i == 0) | (expert != be_ref[jnp.maximum(i - 1, 0)])

    def weight_copies(e, s):
        return (pltpu.make_async_copy(wup_hbm.at[e], wup_buf.at[s], sems.at[s]),
                pltpu.make_async_copy(wdn_hbm.at[e], wdn_buf.at[s], sems.at[s]))

    @pl.when(i == 0)
    def _():
        for copy in weight_copies(expert, slot):
            copy.start()

    @pl.when(first_of_run & (next_ref[i] >= 0))
    def _():
        for copy in weight_copies(next_ref[i], 1 - slot):
            copy.start()

    @pl.when(first_of_run)
    def _():
        for copy in weight_copies(expert, slot):
            copy.wait()

    wup = wup_buf.at[slot]
    wdn = wdn_buf.at[slot]

    def first_rows(n):
        if n:
            x = _load_row_tiles(xs_ref, 0, n)
            gate = _dot(x, wup[:, :D_EXPERT]) + bup_ref[:, :D_EXPERT]
            up = _dot(x, wup[:, D_EXPERT:]) + bup_ref[:, D_EXPERT:]
            gate = jnp.minimum(gate, SWIGLU_LIMIT)
            up = jnp.clip(up, -SWIGLU_LIMIT, SWIGLU_LIMIT)
            act = (up + 1.0) * (gate * _sigmoid(SWIGLU_ALPHA * gate))
            o = _dot(act, wdn[...]) + bdn_ref[...]
            _store_row_tiles(out_ref, (), n, o)
        if n < BLK:
            out_ref[n * ROW_TILES:, :] = jnp.zeros(((BLK - n) * ROW_TILES, LANES), _F32)

    n_valid = nvalid_ref[i]
    for live in range(BLK // SUB + 1):
        lo, hi = (live - 1) * SUB, live * SUB

        @pl.when((n_valid > lo) & (n_valid <= hi) if live else n_valid <= 0)
        def _():
            first_rows(live * SUB)


def _experts(block_expert, nused, nvalid, slot, next_expert, xs, w_up, b_up, w_down, b_down):
    n_blk = block_expert.shape[0]
    rows = lambda i, be, nu, *_: (jnp.minimum(i, nu[0] - 1), 0)
    expert = lambda i, be, *_: (be[i], 0, 0)
    d, f2 = w_up.shape[1], w_up.shape[2]
    grid_spec = pltpu.PrefetchScalarGridSpec(
        num_scalar_prefetch=5,
        grid=(n_blk,),
        in_specs=[
            pl.BlockSpec((BLK * ROW_TILES, LANES), rows),
            pl.BlockSpec((None, 1, f2), expert),
            pl.BlockSpec((None, 1, d), expert),
            pl.BlockSpec(memory_space=pl.ANY),
            pl.BlockSpec(memory_space=pl.ANY),
        ],
        out_specs=pl.BlockSpec((BLK * ROW_TILES, LANES), lambda i, *_: (i, 0)),
        scratch_shapes=[pltpu.VMEM((2, d, f2), _F32), pltpu.VMEM((2, w_down.shape[1], d), _F32),
                        pltpu.SemaphoreType.DMA((2,))],
    )
    return pl.pallas_call(
        _experts_kernel,
        grid_spec=grid_spec,
        out_shape=jax.ShapeDtypeStruct(xs.shape, _F32),
        compiler_params=pltpu.CompilerParams(
            dimension_semantics=("arbitrary",), vmem_limit_bytes=VMEM_LIMIT),
        name="experts",
    )(block_expert, nused, nvalid, slot, next_expert, xs, b_up, b_down, w_up, w_down)


def _combine_kernel(n, dest_ref, dest_next_ref, x1_ref, mf_ref, gfin_ref, outs_hbm, y_ref, buf,
                    sems):
    i = pl.program_id(0)

    def gather(d_ref, slot):
        def issue(t, carry):
            for k in range(TOP_K):
                dst = buf.at[slot, pl.ds(pl.multiple_of((k * TD + t) * ROW_TILES, ROW_TILES),
                                         ROW_TILES), :]
                pltpu.make_async_copy(_row_tile(outs_hbm, d_ref[0, 0, t * TOP_K + k]), dst,
                                      sems.at[slot]).start(priority=k % 2)
            return carry

        lax.fori_loop(0, TD, issue, 0)

    slot = i % 2

    @pl.when(i == 0)
    def _():
        gather(dest_ref, 0)

    @pl.when(i + 1 < n)
    def _():
        gather(dest_next_ref, 1 - slot)

    pltpu.make_async_copy(outs_hbm.at[pl.ds(0, buf.shape[1]), :], buf.at[slot], sems.at[slot]).wait()

    y = x1_ref[...]
    gates = mf_ref[...]
    rows = buf.at[slot]
    for k in range(TOP_K):
        y = y + gates[:, k:k + 1] * _load_row_tiles(rows, k * TD * ROW_TILES, TD)
    y_ref[...] = _rms(y, gfin_ref[...])


def _combine(dest, x1, meta_f, g_final, outs):
    n_tok, d = x1.shape
    n_tiles = n_tok // TD
    tok = lambda i: (i, 0)
    return pl.pallas_call(
        functools.partial(_combine_kernel, n_tiles),
        grid=(n_tiles,),
        in_specs=[
            pl.BlockSpec((1, 1, TD * TOP_K), lambda i: (i, 0, 0), memory_space=pltpu.SMEM),
            pl.BlockSpec((1, 1, TD * TOP_K), lambda i: (jnp.minimum(i + 1, n_tiles - 1), 0, 0),
                         memory_space=pltpu.SMEM),
            pl.BlockSpec((TD, d), tok),
            pl.BlockSpec((TD, LANES), tok),
            _full(g_final.shape),
            pl.BlockSpec(memory_space=pl.ANY),
        ],
        out_specs=pl.BlockSpec((TD, d), tok),
        out_shape=jax.ShapeDtypeStruct((n_tok, d), _F32),
        scratch_shapes=[pltpu.VMEM((2, TOP_K * TD * ROW_TILES, LANES), _F32),
                        pltpu.SemaphoreType.DMA((2,))],
        compiler_params=pltpu.CompilerParams(
            dimension_semantics=("arbitrary",), vmem_limit_bytes=VMEM_LIMIT),
        name="combine",
    )(dest, dest, x1, meta_f, g_final, outs)


def _prep_weights(norm_mix_g, w_in, ln_v_g, ln_v_b, w_spatial, b_spatial, w_branch_a, conv_w,
                  conv_b, ln_conv_g, ln_conv_b, w_branch_b, b_branch_b, b_gate_a, b_gate_b, w_out,
                  norm_ffn_g, w_router, b_router, n_new):
    row = lambda a: a.reshape(1, -1)
    pad = LANES - N_EXPERTS
    wr = jnp.pad(w_router, ((0, 0), (0, pad)))
    wr_hi = wr.astype(_BF16)
    wr_lo = (wr - wr_hi.astype(_F32)).astype(_BF16)
    wcs = jnp.repeat(w_spatial[:, :n_new, :n_new].transpose(1, 2, 0).reshape(n_new * n_new, A_GROUPS),
                     LANES, axis=1)
    return dict(
        gmix=row(norm_mix_g), win=w_in.astype(_BF16), lnvg=row(ln_v_g), lnvb=row(ln_v_b),
        wsp=w_spatial, bsp=jnp.repeat(b_spatial.T, LANES, axis=1), wcs=wcs,
        bcs=jnp.repeat(b_spatial[:, :n_new].T, LANES, axis=1),
        wa=w_branch_a.astype(_BF16), cw=conv_w, cb=row(conv_b), lncg=row(ln_conv_g),
        lncb=row(ln_conv_b), wb=w_branch_b.astype(_BF16), bb=row(b_branch_b), bga=row(b_gate_a),
        bgb=row(b_gate_b), wout=w_out.astype(_BF16), gffn=row(norm_ffn_g),
        wr=jnp.concatenate([wr_hi, wr_lo], axis=1), br=jnp.pad(row(b_router), ((0, 0), (0, pad))))


def kernel(x_prompt, x_sample, cache_conv, norm_mix_g, w_in, ln_v_g, ln_v_b, w_spatial, b_spatial, w_branch_a, conv_w, conv_b, ln_conv_g, ln_conv_b, w_branch_b, b_branch_b, b_gate_a, b_gate_b, w_out, norm_ffn_g, w_router, b_router, w_up, b_up, w_down, b_down, norm_final_g):
    assert w_in.shape[0] == 1, "single trunk layer"
    bsz, seq, d = x_prompt.shape
    nseq, n_new, _ = x_sample.shape
    assert d == D_MODEL and seq % TT == 0 and TT % CHUNK == 0 and nseq % SB == 0
    assert n_new <= CHUNK and (nseq * n_new) % TD == 0 and (bsz * seq) % TD == 0
    w = _prep_weights(norm_mix_g[0], w_in[0], ln_v_g[0], ln_v_b[0], w_spatial[0], b_spatial[0],
                      w_branch_a[0], conv_w[0], conv_b[0], ln_conv_g[0], ln_conv_b[0],
                      w_branch_b[0], b_branch_b[0], b_gate_a[0], b_gate_b[0], w_out[0],
                      norm_ffn_g[0], w_router[0], b_router[0], n_new)

    x1_p, h2_p, mi_p, mf_p, cnt_p, cst_p = _mixer_prompt(x_prompt, w)
    x_t = x_sample.transpose(1, 0, 2)
    cache_t = cache_conv[0].transpose(1, 0, 2)
    x1_s, h2_s, mi_s, mf_s, cnt, glu_s, v_s = _mixer_sample(x_t, cache_t, cnt_p, w)
    n_p, n_s = bsz * seq, nseq * n_new
    x1_s = x1_s.reshape(n_s, d)
    h2_s = h2_s.reshape(n_s * ROW_TILES, LANES)
    mi_s = mi_s.reshape(n_s, LANES)
    mf_s = mf_s.reshape(n_s, LANES)

    counts = cnt[0, :N_EXPERTS].astype(jnp.int32)
    padded = ((counts + BLK - 1) // BLK) * BLK
    pend = jnp.cumsum(padded)
    pstart = pend - padded
    n_assign = (n_p + n_s) * TOP_K
    n_blk = -(-(n_assign + N_EXPERTS * (BLK - 1)) // BLK)
    nused = (pend[-1] // BLK).astype(jnp.int32)
    blk_ids = jnp.arange(n_blk, dtype=jnp.int32)
    first_row = jnp.minimum(blk_ids, nused - 1) * BLK
    be = jnp.sum((pend[None, :] <= first_row[:, None]).astype(jnp.int32), axis=1)
    be = jnp.minimum(be, N_EXPERTS - 1)
    experts_row = jnp.arange(N_EXPERTS, dtype=jnp.int32)

    def dest_of(mi):
        sel = mi[:, :TOP_K, None] == experts_row[None, None, :]
        dest = jnp.sum(jnp.where(sel, pstart[None, None, :], 0), axis=-1) + mi[:, TOP_K:2 * TOP_K]
        return dest.reshape(-1, 1, TD * TOP_K)

    dest_p, dest_s = dest_of(mi_p), dest_of(mi_s)
    xs = _dispatch(pstart + counts, padded - counts, nused.reshape(1),
                   jnp.concatenate([dest_p, dest_s], axis=0), h2_p, h2_s, n_blk)
    of_block = be[:, None] == experts_row[None, :]
    row_in_expert = blk_ids * BLK - jnp.sum(jnp.where(of_block, pstart[None, :], 0), axis=1)
    nvalid = jnp.clip(jnp.sum(jnp.where(of_block, counts[None, :], 0), axis=1) - row_in_expert,
                      0, BLK)
    nvalid = jnp.where(blk_ids < nused, nvalid, 0).astype(jnp.int32)
    used = counts > 0
    slot_e = (jnp.cumsum(used.astype(jnp.int32)) - 1) % 2
    later = jnp.where(used[None, :] & (experts_row[None, :] > experts_row[:, None]),
                      experts_row[None, :], N_EXPERTS)
    next_e = jnp.min(later, axis=1)
    next_e = jnp.where(next_e < N_EXPERTS, next_e, -1)
    slot = jnp.sum(jnp.where(of_block, slot_e[None, :], 0), axis=1).astype(jnp.int32)
    next_expert = jnp.sum(jnp.where(of_block, next_e[None, :], 0), axis=1).astype(jnp.int32)
    outs = _experts(be, nused.reshape(1), nvalid, slot, next_expert, xs, w_up[0],
                    b_up[0][:, None, :], w_down[0], b_down[0][:, None, :])
    g_final = norm_final_g.reshape(1, d)
    y_p = _combine(dest_p, x1_p, mf_p, g_final, outs)
    y_s = _combine(dest_s, x1_s, mf_s, g_final, outs)

    y_prompt = y_p.reshape(bsz, seq, d)
    y_sample = y_s.reshape(n_new, nseq, d).transpose(1, 0, 2)
    conv_state_prompt = cst_p[None]
    glu_new = glu_s.transpose(1, 0, 2)
    conv_state_sample = jnp.concatenate([cache_conv[0][:, n_new:], glu_new], axis=1)[None]
    chunk_v_sample = v_s.transpose(1, 0, 2)[None]
    return (y_prompt, y_sample, conv_state_prompt, conv_state_sample, chunk_v_sample)
```

```python
import functools

import jax
import jax.numpy as jnp
from jax import lax
from jax.experimental import pallas as pl
from jax.experimental.pallas import tpu as pltpu

D_MODEL = 1024
CHUNK = 128
A_GROUPS = 8
CONV_WIDTH = 31
HIST = 32
N_EXPERTS = 32
TOP_K = 4
META = 2 * TOP_K
D_EXPERT = 1024
SWIGLU_LIMIT = 7.0
SWIGLU_ALPHA = 1.702
EPS = 1e-5
LANES = 128
SUBLANES = 8
ROW_TILES = D_MODEL // LANES
TT = 256
TD = 512
CONV_ROWS = 64
SB = 32
BLK = 512
SUB = 256
VMEM_LIMIT = 56 * 1024 * 1024

_F32 = jnp.float32
_BF16 = jnp.bfloat16
_INV_SQRT2 = 0.7071067811865476


def _rms(x, g):
    return x * lax.rsqrt(jnp.mean(x * x, axis=-1, keepdims=True) + EPS) * g


def _ln(x, g, b):
    mu = jnp.mean(x, axis=-1, keepdims=True)
    xc = x - mu
    var = jnp.mean(xc * xc, axis=-1, keepdims=True)
    return xc * lax.rsqrt(var + EPS) * g + b


def _gelu(x):
    return 0.5 * x * (1.0 + lax.erf(x * _INV_SQRT2))


def _sigmoid(x):
    return 1.0 / (1.0 + jnp.exp(-x))


def _dot(a, b):
    return jnp.dot(a, b, preferred_element_type=_F32)


def _store_row_tiles(ref, lead, n, val, start=0):
    for j in range(ROW_TILES):
        ref[lead + (pl.ds(start + j, n, stride=ROW_TILES), slice(None))] = (
            val[:, j * LANES:(j + 1) * LANES])


def _load_row_tiles(ref, start, n):
    return jnp.concatenate(
        [ref[pl.ds(start + j, n, stride=ROW_TILES), :] for j in range(ROW_TILES)], axis=1)


def _route(h2, wr_ref, br_ref, carry_ref):
    n = h2.shape[0]
    h_hi = h2.astype(_BF16)
    h_lo = (h2 - h_hi.astype(_F32)).astype(_BF16)
    by_hi = _dot(h_hi, wr_ref[...])
    logits = (by_hi[:, :LANES] + by_hi[:, LANES:] + _dot(h_lo, wr_ref[:, :LANES])) + br_ref[...]
    lane = lax.broadcasted_iota(jnp.int32, (n, LANES), 1)
    l = jnp.where(lane < N_EXPERTS, logits, -jnp.inf)
    onehots, vals, idxs = [], [], []
    for _ in range(TOP_K):
        m = jnp.max(l, axis=-1, keepdims=True)
        idx = jnp.min(jnp.where(l == m, lane, LANES), axis=-1, keepdims=True)
        oh = lane == idx
        onehots.append(oh)
        vals.append(m)
        idxs.append(idx)
        l = jnp.where(oh, -jnp.inf, l)
    exps = [jnp.exp(v - vals[0]) for v in vals]
    denom = exps[0] + exps[1] + exps[2] + exps[3]
    gates = [e / denom for e in exps]

    oh_all = jnp.zeros((n, LANES), _F32)
    for oh in onehots:
        oh_all = oh_all + jnp.where(oh, 1.0, 0.0)
    ri = lax.broadcasted_iota(jnp.int32, (n, n), 0)
    ci = lax.broadcasted_iota(jnp.int32, (n, n), 1)
    tri = jnp.where(ci < ri, 1.0, 0.0).astype(_BF16)
    base = _dot(tri, oh_all.astype(_BF16)) + carry_ref[0:1, :]
    ranks = [jnp.sum(jnp.where(oh, base, 0.0), axis=-1, keepdims=True) for oh in onehots]
    carry_ref[...] = carry_ref[...] + jnp.sum(oh_all, axis=0, keepdims=True)

    meta_i = jnp.zeros((n, LANES), jnp.int32)
    meta_f = jnp.zeros((n, LANES), _F32)
    for k in range(TOP_K):
        meta_i = jnp.where(lane == k, idxs[k], meta_i)
        meta_i = jnp.where(lane == TOP_K + k, ranks[k].astype(jnp.int32), meta_i)
        meta_f = jnp.where(lane == k, gates[k], meta_f)
    return meta_i, meta_f


def _mixer_prompt_kernel(x_ref, gmix_ref, win_ref, lnvg_ref, lnvb_ref, wsp_ref, bsp_ref, wa_ref,
                         cw_ref, cb_ref, lncg_ref, lncb_ref, wb_ref, bb_ref, bga_ref, bgb_ref,
                         wout_ref, gffn_ref, wr_ref, br_ref,
                         x1_ref, h2_ref, mi_ref, mf_ref, cnt_ref, cst_ref,
                         cbuf, carry):
    b = pl.program_id(0)
    t = pl.program_id(1)

    @pl.when((b == 0) & (t == 0))
    def _():
        carry[...] = jnp.zeros_like(carry)

    @pl.when(t == 0)
    def _():
        cbuf[0:HIST, :] = jnp.zeros((HIST, D_MODEL), _F32)
        cbuf[HIST + TT:, :] = jnp.zeros((SUBLANES, D_MODEL), _F32)

    x = x_ref[...]
    hb = _rms(x, gmix_ref[...]).astype(_BF16)

    def proj(s):
        return _dot(hb, win_ref[:, s * D_MODEL:(s + 1) * D_MODEL])

    glu = proj(2) * _sigmoid(proj(3))
    cbuf[HIST:HIST + TT, :] = glu
    off = HIST - (CONV_WIDTH - 1)
    span = HIST + CONV_ROWS + SUBLANES
    conv_cols = []
    other, half, pieces = (0, 1, 4, 5), D_MODEL // 2, []
    order_after = None
    for c in range(D_MODEL // LANES):
        cols = slice(c * LANES, (c + 1) * LANES)
        accs = []
        for r in range(TT // CONV_ROWS):
            acc = jnp.zeros((CONV_ROWS, LANES), _F32) + cb_ref[:, cols]
            window = cbuf[r * CONV_ROWS:r * CONV_ROWS + span, cols]
            if order_after is not None:
                zero = pltpu.bitcast(lax.shift_right_logical(
                    pltpu.bitcast(order_after, jnp.uint32), jnp.uint32(32)), _F32)
                window = window + jnp.concatenate([zero] * (span // SUBLANES), axis=0)
            for sh in range(SUBLANES):
                shifted = window if sh == 0 else pltpu.roll(window, span - sh, axis=0)
                for a in range(HIST // SUBLANES + 1):
                    k = a * SUBLANES + sh - off
                    if 0 <= k < CONV_WIDTH:
                        acc = acc + (cw_ref[k:k + 1, cols]
                                     * shifted[a * SUBLANES:a * SUBLANES + CONV_ROWS, :])
            accs.append(acc)
            order_after = acc[:SUBLANES, :]
        conv_cols.append(jnp.concatenate(accs, axis=0))
        lo = other[c // 2] * D_MODEL + (c % 2) * half
        pieces.append(_dot(hb, win_ref[:, lo:lo + half]))
    conv = jnp.concatenate(conv_cols, axis=1)
    p_u, p_v, p_ga, p_gb = [jnp.concatenate(pieces[2 * i:2 * i + 2], axis=1) for i in range(4)]
    cbuf[0:HIST, :] = cbuf[TT:TT + HIST, :]
    cn = _ln(conv, lncg_ref[...], lncb_ref[...])
    y_b = _dot((cn * _sigmoid(cn)).astype(_BF16), wb_ref[...]) + bb_ref[...]

    u = _gelu(p_u)
    v = _ln(_gelu(p_v), lnvg_ref[...], lnvb_ref[...])
    ri = lax.broadcasted_iota(jnp.int32, (CHUNK, CHUNK), 0)
    ci = lax.broadcasted_iota(jnp.int32, (CHUNK, CHUNK), 1)
    n_chunks = TT // CHUNK
    vb = v.astype(_BF16)
    s_groups = []
    for g in range(A_GROUPS):
        wm = jnp.where(ci <= ri, wsp_ref[g], 0.0).astype(_BF16)
        cols = slice(g * LANES, (g + 1) * LANES)
        s_groups.append(_dot(wm, jnp.concatenate(
            [vb[c * CHUNK:(c + 1) * CHUNK, cols] for c in range(n_chunks)], axis=1)))
    s = jnp.concatenate(
        [jnp.concatenate([sg[:, c * LANES:(c + 1) * LANES] for sg in s_groups], axis=1)
         + bsp_ref[...] for c in range(n_chunks)], axis=0)
    y_a = _dot((u * s).astype(_BF16), wa_ref[...])

    merged = _sigmoid(p_ga + bga_ref[...]) * y_a + _sigmoid(p_gb + bgb_ref[...]) * y_b
    x1 = x + _dot(merged.astype(_BF16), wout_ref[...])
    x1_ref[...] = x1

    h2 = _rms(x1, gffn_ref[...])
    _store_row_tiles(h2_ref, (), TT, h2)
    meta_i, meta_f = _route(h2, wr_ref, br_ref, carry)
    mi_ref[...] = meta_i[:, :META]
    mf_ref[...] = meta_f
    cnt_ref[...] = carry[...]

    @pl.when(t == pl.num_programs(1) - 1)
    def _():
        cst_ref[...] = cbuf[HIST - (CONV_WIDTH - 1):HIST, :]


def _full(shape):
    return pl.BlockSpec(shape, lambda *_: (0,) * len(shape))


def _mixer_prompt(x, w):
    bsz, seq, d = x.shape
    n_tok = bsz * seq
    nt = seq // TT
    tok = lambda b, t: (b * nt + t, 0)
    weights = (w["gmix"], w["win"], w["lnvg"], w["lnvb"], w["wsp"], w["bsp"], w["wa"], w["cw"],
               w["cb"], w["lncg"], w["lncb"], w["wb"], w["bb"], w["bga"], w["bgb"], w["wout"],
               w["gffn"], w["wr"], w["br"])
    return pl.pallas_call(
        _mixer_prompt_kernel,
        grid=(bsz, nt),
        in_specs=[pl.BlockSpec((None, TT, d), lambda b, t: (b, t, 0))]
                 + [_full(a.shape) for a in weights],
        out_specs=[
            pl.BlockSpec((TT, d), tok),
            pl.BlockSpec((TT * ROW_TILES, LANES), tok),
            pl.BlockSpec((TT, META), tok),
            pl.BlockSpec((TT, LANES), tok),
            _full((SUBLANES, LANES)),
            pl.BlockSpec((None, CONV_WIDTH - 1, d), lambda b, t: (b, 0, 0)),
        ],
        out_shape=[
            jax.ShapeDtypeStruct((n_tok, d), _F32),
            jax.ShapeDtypeStruct((n_tok * ROW_TILES, LANES), _F32),
            jax.ShapeDtypeStruct((n_tok, META), jnp.int32),
            jax.ShapeDtypeStruct((n_tok, LANES), _F32),
            jax.ShapeDtypeStruct((SUBLANES, LANES), _F32),
            jax.ShapeDtypeStruct((bsz, CONV_WIDTH - 1, d), _F32),
        ],
        scratch_shapes=[pltpu.VMEM((HIST + TT + SUBLANES, d), _F32),
                        pltpu.VMEM((SUBLANES, LANES), _F32)],
        compiler_params=pltpu.CompilerParams(
            dimension_semantics=("arbitrary", "arbitrary"), vmem_limit_bytes=VMEM_LIMIT),
        name="mixer_prompt",
    )(x, *weights)


def _mixer_sample_kernel(x_ref, cache_ref, cnt0_ref, gmix_ref, win_ref, lnvg_ref, lnvb_ref,
                         wcs_ref, bcs_ref, wa_ref, cw_ref, cb_ref, lncg_ref, lncb_ref, wb_ref,
                         bb_ref, bga_ref, bgb_ref, wout_ref, gffn_ref, wr_ref, br_ref,
                         x1_ref, h2_ref, mi_ref, mf_ref, cnt_ref, cst_ref, v_ref,
                         carry):
    nt = x_ref.shape[0]
    n = nt * SB

    @pl.when(pl.program_id(0) == 0)
    def _():
        carry[...] = cnt0_ref[...]

    x = x_ref[...].reshape(n, D_MODEL)
    hb = _rms(x, gmix_ref[...]).astype(_BF16)

    def proj(s):
        return _dot(hb, win_ref[:, s * D_MODEL:(s + 1) * D_MODEL])

    u = _gelu(proj(0))
    v = _ln(_gelu(proj(1)), lnvg_ref[...], lnvb_ref[...])
    v_ref[...] = v.reshape(nt, SB, D_MODEL)
    s_rows = []
    for i in range(nt):
        acc = jnp.zeros((SB, D_MODEL), _F32) + bcs_ref[i:i + 1, :]
        for j in range(i + 1):
            acc = acc + wcs_ref[i * nt + j:i * nt + j + 1, :] * v[j * SB:(j + 1) * SB, :]
        s_rows.append(acc)
    s = jnp.concatenate(s_rows, axis=0)
    y_a = _dot((u * s).astype(_BF16), wa_ref[...])

    glu = proj(2) * _sigmoid(proj(3))
    hist = CONV_WIDTH - 1
    cst_ref[0:hist - nt] = cache_ref[nt:hist]
    cst_ref[hist - nt:hist] = glu.reshape(nt, SB, D_MODEL)
    conv_rows = []
    for i in range(nt):
        acc = jnp.zeros((SB, D_MODEL), _F32) + cb_ref[...]
        for k in range(CONV_WIDTH):
            p = i + k
            src = cache_ref[p] if p < hist else glu[(p - hist) * SB:(p - hist + 1) * SB, :]
            acc = acc + cw_ref[k:k + 1, :] * src
        conv_rows.append(acc)
    cn = _ln(jnp.concatenate(conv_rows, axis=0), lncg_ref[...], lncb_ref[...])
    y_b = _dot((cn * _sigmoid(cn)).astype(_BF16), wb_ref[...]) + bb_ref[...]

    merged = (_sigmoid(proj(4) + bga_ref[...]) * y_a + _sigmoid(proj(5) + bgb_ref[...]) * y_b)
    x1 = x + _dot(merged.astype(_BF16), wout_ref[...])
    x1_ref[...] = x1.reshape(nt, SB, D_MODEL)

    h2 = _rms(x1, gffn_ref[...])
    for i in range(nt):
        _store_row_tiles(h2_ref, (i,), SB, h2[i * SB:(i + 1) * SB, :])
    meta_i, meta_f = _route(h2, wr_ref, br_ref, carry)
    mi_ref[...] = meta_i[:, :META].reshape(nt, SB, META)
    mf_ref[...] = meta_f.reshape(nt, SB, LANES)
    cnt_ref[...] = carry[...]


def _mixer_sample(x_t, cache_t, cnt0, w):
    nt, nseq, d = x_t.shape
    hist = cache_t.shape[0]
    weights = (w["gmix"], w["win"], w["lnvg"], w["lnvb"], w["wcs"], w["bcs"], w["wa"], w["cw"],
               w["cb"], w["lncg"], w["lncb"], w["wb"], w["bb"], w["bga"], w["bgb"], w["wout"],
               w["gffn"], w["wr"], w["br"])
    seqs = lambda i: (0, i, 0)
    return pl.pallas_call(
        _mixer_sample_kernel,
        grid=(nseq // SB,),
        in_specs=[pl.BlockSpec((nt, SB, d), seqs), pl.BlockSpec((hist, SB, d), seqs),
                  _full(cnt0.shape)] + [_full(a.shape) for a in weights],
        out_specs=[
            pl.BlockSpec((nt, SB, d), seqs),
            pl.BlockSpec((nt, SB * ROW_TILES, LANES), seqs),
            pl.BlockSpec((nt, SB, META), seqs),
            pl.BlockSpec((nt, SB, LANES), seqs),
            _full((SUBLANES, LANES)),
            pl.BlockSpec((hist, SB, d), seqs),
            pl.BlockSpec((nt, SB, d), seqs),
        ],
        out_shape=[
            jax.ShapeDtypeStruct((nt, nseq, d), _F32),
            jax.ShapeDtypeStruct((nt, nseq * ROW_TILES, LANES), _F32),
            jax.ShapeDtypeStruct((nt, nseq, META), jnp.int32),
            jax.ShapeDtypeStruct((nt, nseq, LANES), _F32),
            jax.ShapeDtypeStruct((SUBLANES, LANES), _F32),
            jax.ShapeDtypeStruct((hist, nseq, d), _F32),
            jax.ShapeDtypeStruct((nt, nseq, d), _F32),
        ],
        scratch_shapes=[pltpu.VMEM((SUBLANES, LANES), _F32)],
        compiler_params=pltpu.CompilerParams(
            dimension_semantics=("arbitrary",), vmem_limit_bytes=VMEM_LIMIT),
        name="mixer_sample",
    )(x_t, cache_t, cnt0, *weights)


def _row_tile(ref, row):
    return ref.at[pl.ds(pl.multiple_of(row * ROW_TILES, ROW_TILES), ROW_TILES), :]


def _dispatch_kernel(n_first, n_blk, pad_start_ref, pad_len_ref, nused_ref, dest_ref, h2a_ref,
                     h2b_ref, xs_hbm, zbuf, sem, zsem):
    i = pl.program_id(0)

    def fill(start):
        def go(copy):
            copy.start() if start else copy.wait()

        def per_expert(e, carry):
            first, length = pad_start_ref[e], pad_len_ref[e]
            for s in range(1, BLK.bit_length()):
                p = BLK >> s

                @pl.when((length & p) != 0)
                def _():
                    row = first + (length & (-2 * p))
                    go(pltpu.make_async_copy(
                        zbuf.at[pl.ds(0, p * ROW_TILES), :],
                        xs_hbm.at[pl.ds(pl.multiple_of(row * ROW_TILES, ROW_TILES), p * ROW_TILES), :],
                        zsem))
            return carry

        def per_tail_block(b, carry):
            go(pltpu.make_async_copy(
                zbuf, xs_hbm.at[pl.ds(pl.multiple_of(b * BLK * ROW_TILES, ROW_TILES),
                                      BLK * ROW_TILES), :], zsem))
            return carry

        lax.fori_loop(0, N_EXPERTS, per_expert, 0)
        lax.fori_loop(nused_ref[0], n_blk, per_tail_block, 0)

    @pl.when(i == 0)
    def _():
        zbuf[...] = jnp.zeros_like(zbuf)
        fill(True)

    def scatter(h2_ref):
        def issue(t, carry):
            src = _row_tile(h2_ref, t)
            for k in range(TOP_K):
                pltpu.make_async_copy(src, _row_tile(xs_hbm, dest_ref[0, 0, t * TOP_K + k]),
                                      sem).start(priority=k % 2)
            return carry

        lax.fori_loop(0, TD, issue, 0)
        for _ in range(TOP_K):
            pltpu.make_async_copy(h2_ref, xs_hbm.at[pl.ds(0, TD * ROW_TILES), :], sem).wait()

    @pl.when(i < n_first)
    def _():
        scatter(h2a_ref)

    @pl.when(i >= n_first)
    def _():
        scatter(h2b_ref)

    @pl.when(i == 0)
    def _():
        fill(False)


def _dispatch(pad_start, pad_len, nused, dest, h2_a, h2_b, n_blk):
    n_tiles = dest.shape[0]
    n_first = h2_a.shape[0] // (TD * ROW_TILES)
    tile = (TD * ROW_TILES, LANES)
    grid_spec = pltpu.PrefetchScalarGridSpec(
        num_scalar_prefetch=3,
        grid=(n_tiles,),
        in_specs=[
            pl.BlockSpec((1, 1, TD * TOP_K), lambda i, *_: (i, 0, 0), memory_space=pltpu.SMEM),
            pl.BlockSpec(tile, lambda i, *_: (jnp.minimum(i, n_first - 1), 0)),
            pl.BlockSpec(tile, lambda i, *_: (jnp.maximum(i - n_first, 0), 0)),
        ],
        out_specs=pl.BlockSpec(memory_space=pl.ANY),
        scratch_shapes=[pltpu.VMEM((BLK * ROW_TILES, LANES), _F32), pltpu.SemaphoreType.DMA,
                        pltpu.SemaphoreType.DMA],
    )
    return pl.pallas_call(
        functools.partial(_dispatch_kernel, n_first, n_blk),
        grid_spec=grid_spec,
        out_shape=jax.ShapeDtypeStruct((n_blk * BLK * ROW_TILES, LANES), _F32),
        compiler_params=pltpu.CompilerParams(
            dimension_semantics=("arbitrary",), has_side_effects=True),
        name="dispatch",
    )(pad_start, pad_len, nused, dest, h2_a, h2_b)


def _experts_kernel(be_ref, nused_ref, nvalid_ref, slot_ref, next_ref, xs_ref, bup_ref, bdn_ref,
                    wup_hbm, wdn_hbm, out_ref, wup_buf, wdn_buf, sems):
    del nused_ref
    i = pl.program_id(0)
    expert = be_ref[i]
    slot = slot_ref[i]
    first_of_run = (i == 0) | (expert != be_ref[jnp.maximum(i - 1, 0)])

    def weight_copies(e, s):
        return (pltpu.make_async_copy(wup_hbm.at[e], wup_buf.at[s], sems.at[s]),
                pltpu.make_async_copy(wdn_hbm.at[e], wdn_buf.at[s], sems.at[s]))

    @pl.when(i == 0)
    def _():
        for copy in weight_copies(expert, slot):
            copy.start()

    @pl.when(first_of_run & (next_ref[i] >= 0))
    def _():
        for copy in weight_copies(next_ref[i], 1 - slot):
            copy.start()

    @pl.when(first_of_run)
    def _():
        for copy in weight_copies(expert, slot):
            copy.wait()

    wup = wup_buf.at[slot]
    wdn = wdn_buf.at[slot]

    def first_rows(n):
        if n:
            x = _load_row_tiles(xs_ref, 0, n)
            gate = _dot(x, wup[:, :D_EXPERT]) + bup_ref[:, :D_EXPERT]
            up = _dot(x, wup[:, D_EXPERT:]) + bup_ref[:, D_EXPERT:]
            gate = jnp.minimum(gate, SWIGLU_LIMIT)
            up = jnp.clip(up, -SWIGLU_LIMIT, SWIGLU_LIMIT)
            act = (up + 1.0) * (gate * _sigmoid(SWIGLU_ALPHA * gate))
            o = _dot(act, wdn[...]) + bdn_ref[...]
            _store_row_tiles(out_ref, (), n, o)
        if n < BLK:
            out_ref[n * ROW_TILES:, :] = jnp.zeros(((BLK - n) * ROW_TILES, LANES), _F32)

    n_valid = nvalid_ref[i]
    for live in range(BLK // SUB + 1):
        lo, hi = (live - 1) * SUB, live * SUB

        @pl.when((n_valid > lo) & (n_valid <= hi) if live else n_valid <= 0)
        def _():
            first_rows(live * SUB)


def _experts(block_expert, nused, nvalid, slot, next_expert, xs, w_up, b_up, w_down, b_down):
    n_blk = block_expert.shape[0]
    rows = lambda i, be, nu, *_: (jnp.minimum(i, nu[0] - 1), 0)
    expert = lambda i, be, *_: (be[i], 0, 0)
    d, f2 = w_up.shape[1], w_up.shape[2]
    grid_spec = pltpu.PrefetchScalarGridSpec(
        num_scalar_prefetch=5,
        grid=(n_blk,),
        in_specs=[
            pl.BlockSpec((BLK * ROW_TILES, LANES), rows),
            pl.BlockSpec((None, 1, f2), expert),
            pl.BlockSpec((None, 1, d), expert),
            pl.BlockSpec(memory_space=pl.ANY),
            pl.BlockSpec(memory_space=pl.ANY),
        ],
        out_specs=pl.BlockSpec((BLK * ROW_TILES, LANES), lambda i, *_: (i, 0)),
        scratch_shapes=[pltpu.VMEM((2, d, f2), _F32), pltpu.VMEM((2, w_down.shape[1], d), _F32),
                        pltpu.SemaphoreType.DMA((2,))],
    )
    return pl.pallas_call(
        _experts_kernel,
        grid_spec=grid_spec,
        out_shape=jax.ShapeDtypeStruct(xs.shape, _F32),
        compiler_params=pltpu.CompilerParams(
            dimension_semantics=("arbitrary",), vmem_limit_bytes=VMEM_LIMIT),
        name="experts",
    )(block_expert, nused, nvalid, slot, next_expert, xs, b_up, b_down, w_up, w_down)


def _combine_kernel(n, dest_ref, dest_next_ref, x1_ref, mf_ref, gfin_ref, outs_hbm, y_ref, buf,
                    sems):
    i = pl.program_id(0)

    def gather(d_ref, slot):
        def issue(t, carry):
            for k in range(TOP_K):
                dst = buf.at[slot, pl.ds(pl.multiple_of((k * TD + t) * ROW_TILES, ROW_TILES),
                                         ROW_TILES), :]
                pltpu.make_async_copy(_row_tile(outs_hbm, d_ref[0, 0, t * TOP_K + k]), dst,
                                      sems.at[slot]).start(priority=k % 2)
            return carry

        lax.fori_loop(0, TD, issue, 0)

    slot = i % 2

    @pl.when(i == 0)
    def _():
        gather(dest_ref, 0)

    @pl.when(i + 1 < n)
    def _():
        gather(dest_next_ref, 1 - slot)

    pltpu.make_async_copy(outs_hbm.at[pl.ds(0, buf.shape[1]), :], buf.at[slot], sems.at[slot]).wait()

    y = x1_ref[...]
    gates = mf_ref[...]
    rows = buf.at[slot]
    for k in range(TOP_K):
        y = y + gates[:, k:k + 1] * _load_row_tiles(rows, k * TD * ROW_TILES, TD)
    y_ref[...] = _rms(y, gfin_ref[...])


def _combine(dest, x1, meta_f, g_final, outs):
    n_tok, d = x1.shape
    n_tiles = n_tok // TD
    tok = lambda i: (i, 0)
    return pl.pallas_call(
        functools.partial(_combine_kernel, n_tiles),
        grid=(n_tiles,),
        in_specs=[
            pl.BlockSpec((1, 1, TD * TOP_K), lambda i: (i, 0, 0), memory_space=pltpu.SMEM),
            pl.BlockSpec((1, 1, TD * TOP_K), lambda i: (jnp.minimum(i + 1, n_tiles - 1), 0, 0),
                         memory_space=pltpu.SMEM),
            pl.BlockSpec((TD, d), tok),
            pl.BlockSpec((TD, LANES), tok),
            _full(g_final.shape),
            pl.BlockSpec(memory_space=pl.ANY),
        ],
        out_specs=pl.BlockSpec((TD, d), tok),
        out_shape=jax.ShapeDtypeStruct((n_tok, d), _F32),
        scratch_shapes=[pltpu.VMEM((2, TOP_K * TD * ROW_TILES, LANES), _F32),
                        pltpu.SemaphoreType.DMA((2,))],
        compiler_params=pltpu.CompilerParams(
            dimension_semantics=("arbitrary",), vmem_limit_bytes=VMEM_LIMIT),
        name="combine",
    )(dest, dest, x1, meta_f, g_final, outs)


def _prep_weights(norm_mix_g, w_in, ln_v_g, ln_v_b, w_spatial, b_spatial, w_branch_a, conv_w,
                  conv_b, ln_conv_g, ln_conv_b, w_branch_b, b_branch_b, b_gate_a, b_gate_b, w_out,
                  norm_ffn_g, w_router, b_router, n_new):
    row = lambda a: a.reshape(1, -1)
    pad = LANES - N_EXPERTS
    wr = jnp.pad(w_router, ((0, 0), (0, pad)))
    wr_hi = wr.astype(_BF16)
    wr_lo = (wr - wr_hi.astype(_F32)).astype(_BF16)
    wcs = jnp.repeat(w_spatial[:, :n_new, :n_new].transpose(1, 2, 0).reshape(n_new * n_new, A_GROUPS),
                     LANES, axis=1)
    return dict(
        gmix=row(norm_mix_g), win=w_in.astype(_BF16), lnvg=row(ln_v_g), lnvb=row(ln_v_b),
        wsp=w_spatial, bsp=jnp.repeat(b_spatial.T, LANES, axis=1), wcs=wcs,
        bcs=jnp.repeat(b_spatial[:, :n_new].T, LANES, axis=1),
        wa=w_branch_a.astype(_BF16), cw=conv_w, cb=row(conv_b), lncg=row(ln_conv_g),
        lncb=row(ln_conv_b), wb=w_branch_b.astype(_BF16), bb=row(b_branch_b), bga=row(b_gate_a),
        bgb=row(b_gate_b), wout=w_out.astype(_BF16), gffn=row(norm_ffn_g),
        wr=jnp.concatenate([wr_hi, wr_lo], axis=1), br=jnp.pad(row(b_router), ((0, 0), (0, pad))))


def kernel(x_prompt, x_sample, cache_conv, norm_mix_g, w_in, ln_v_g, ln_v_b, w_spatial, b_spatial, w_branch_a, conv_w, conv_b, ln_conv_g, ln_conv_b, w_branch_b, b_branch_b, b_gate_a, b_gate_b, w_out, norm_ffn_g, w_router, b_router, w_up, b_up, w_down, b_down, norm_final_g):
    assert w_in.shape[0] == 1, "single trunk layer"
    bsz, seq, d = x_prompt.shape
    nseq, n_new, _ = x_sample.shape
    assert d == D_MODEL and seq % TT == 0 and TT % CHUNK == 0 and nseq % SB == 0
    assert n_new <= CHUNK and (nseq * n_new) % TD == 0 and (bsz * seq) % TD == 0
    w = _prep_weights(norm_mix_g[0], w_in[0], ln_v_g[0], ln_v_b[0], w_spatial[0], b_spatial[0],
                      w_branch_a[0], conv_w[0], conv_b[0], ln_conv_g[0], ln_conv_b[0],
                      w_branch_b[0], b_branch_b[0], b_gate_a[0], b_gate_b[0], w_out[0],
                      norm_ffn_g[0], w_router[0], b_router[0], n_new)

    x1_p, h2_p, mi_p, mf_p, cnt_p, cst_p = _mixer_prompt(x_prompt, w)
    x_t = x_sample.transpose(1, 0, 2)
    cache_t = cache_conv[0].transpose(1, 0, 2)
    x1_s, h2_s, mi_s, mf_s, cnt, cst_s, v_s = _mixer_sample(x_t, cache_t, cnt_p, w)
    n_p, n_s = bsz * seq, nseq * n_new
    x1_s = x1_s.reshape(n_s, d)
    h2_s = h2_s.reshape(n_s * ROW_TILES, LANES)
    mi_s = mi_s.reshape(n_s, META)
    mf_s = mf_s.reshape(n_s, LANES)

    counts = cnt[0, :N_EXPERTS].astype(jnp.int32)
    padded = ((counts + BLK - 1) // BLK) * BLK
    pend = jnp.cumsum(padded)
    pstart = pend - padded
    n_assign = (n_p + n_s) * TOP_K
    n_blk = -(-(n_assign + N_EXPERTS * (BLK - 1)) // BLK)
    nused = (pend[-1] // BLK).astype(jnp.int32)
    blk_ids = jnp.arange(n_blk, dtype=jnp.int32)
    first_row = jnp.minimum(blk_ids, nused - 1) * BLK
    be = jnp.sum((pend[None, :] <= first_row[:, None]).astype(jnp.int32), axis=1)
    be = jnp.minimum(be, N_EXPERTS - 1)
    experts_row = jnp.arange(N_EXPERTS, dtype=jnp.int32)

    def dest_of(mi):
        sel = mi[:, :TOP_K, None] == experts_row[None, None, :]
        dest = jnp.sum(jnp.where(sel, pstart[None, None, :], 0), axis=-1) + mi[:, TOP_K:2 * TOP_K]
        return dest.reshape(-1, 1, TD * TOP_K)

    dest_p, dest_s = dest_of(mi_p), dest_of(mi_s)
    xs = _dispatch(pstart + counts, padded - counts, nused.reshape(1),
                   jnp.concatenate([dest_p, dest_s], axis=0), h2_p, h2_s, n_blk)
    of_block = be[:, None] == experts_row[None, :]
    row_in_expert = blk_ids * BLK - jnp.sum(jnp.where(of_block, pstart[None, :], 0), axis=1)
    nvalid = jnp.clip(jnp.sum(jnp.where(of_block, counts[None, :], 0), axis=1) - row_in_expert,
                      0, BLK)
    nvalid = jnp.where(blk_ids < nused, nvalid, 0).astype(jnp.int32)
    used = counts > 0
    slot_e = (jnp.cumsum(used.astype(jnp.int32)) - 1) % 2
    later = jnp.where(used[None, :] & (experts_row[None, :] > experts_row[:, None]),
                      experts_row[None, :], N_EXPERTS)
    next_e = jnp.min(later, axis=1)
    next_e = jnp.where(next_e < N_EXPERTS, next_e, -1)
    slot = jnp.sum(jnp.where(of_block, slot_e[None, :], 0), axis=1).astype(jnp.int32)
    next_expert = jnp.sum(jnp.where(of_block, next_e[None, :], 0), axis=1).astype(jnp.int32)
    outs = _experts(be, nused.reshape(1), nvalid, slot, next_expert, xs, w_up[0],
                    b_up[0][:, None, :], w_down[0], b_down[0][:, None, :])
    g_final = norm_final_g.reshape(1, d)
    y_p = _combine(dest_p, x1_p, mf_p, g_final, outs)
    y_s = _combine(dest_s, x1_s, mf_s, g_final, outs)

    y_prompt = y_p.reshape(bsz, seq, d)
    y_sample = y_s.reshape(n_new, nseq, d).transpose(1, 0, 2)
    conv_state_prompt = cst_p[None]
    conv_state_sample = cst_s.transpose(1, 0, 2)[None]
    chunk_v_sample = v_s.transpose(1, 0, 2)[None]
    return (y_prompt, y_sample, conv_state_prompt, conv_state_sample, chunk_v_sample)
```

```python
import functools

import jax
import jax.numpy as jnp
from jax import lax
from jax.experimental import pallas as pl
from jax.experimental.pallas import tpu as pltpu

D_MODEL = 1024
CHUNK = 128
A_GROUPS = 8
CONV_WIDTH = 31
HIST = 32
N_EXPERTS = 32
TOP_K = 4
META = 2 * TOP_K
D_EXPERT = 1024
SWIGLU_LIMIT = 7.0
SWIGLU_ALPHA = 1.702
EPS = 1e-5
LANES = 128
SUBLANES = 8
ROW_TILES = D_MODEL // LANES
TT = 256
TD = 512
CONV_ROWS = 64
SB = 32
BLK = 512
SUB = 256
VMEM_LIMIT = 56 * 1024 * 1024

_F32 = jnp.float32
_BF16 = jnp.bfloat16
_INV_SQRT2 = 0.7071067811865476


def _rms(x, g):
    return x * lax.rsqrt(jnp.mean(x * x, axis=-1, keepdims=True) + EPS) * g


def _ln(x, g, b):
    mu = jnp.mean(x, axis=-1, keepdims=True)
    xc = x - mu
    var = jnp.mean(xc * xc, axis=-1, keepdims=True)
    return xc * lax.rsqrt(var + EPS) * g + b


def _gelu(x):
    return 0.5 * x * (1.0 + lax.erf(x * _INV_SQRT2))


def _sigmoid(x):
    return 1.0 / (1.0 + jnp.exp(-x))


def _dot(a, b):
    return jnp.dot(a, b, preferred_element_type=_F32)


def _store_row_tiles(ref, lead, n, val, start=0):
    for j in range(ROW_TILES):
        ref[lead + (pl.ds(start + j, n, stride=ROW_TILES), slice(None))] = (
            val[:, j * LANES:(j + 1) * LANES])


def _load_row_tiles(ref, start, n):
    return jnp.concatenate(
        [ref[pl.ds(start + j, n, stride=ROW_TILES), :] for j in range(ROW_TILES)], axis=1)


def _route(h2, wr_ref, br_ref, carry_ref):
    n = h2.shape[0]
    h_hi = h2.astype(_BF16)
    h_lo = (h2 - h_hi.astype(_F32)).astype(_BF16)
    by_hi = _dot(h_hi, wr_ref[...])
    logits = (by_hi[:, :LANES] + by_hi[:, LANES:] + _dot(h_lo, wr_ref[:, :LANES])) + br_ref[...]
    lane = lax.broadcasted_iota(jnp.int32, (n, LANES), 1)
    l = jnp.where(lane < N_EXPERTS, logits, -jnp.inf)
    onehots, vals, idxs = [], [], []
    for _ in range(TOP_K):
        m = jnp.max(l, axis=-1, keepdims=True)
        idx = jnp.min(jnp.where(l == m, lane, LANES), axis=-1, keepdims=True)
        oh = lane == idx
        onehots.append(oh)
        vals.append(m)
        idxs.append(idx)
        l = jnp.where(oh, -jnp.inf, l)
    exps = [jnp.exp(v - vals[0]) for v in vals]
    denom = exps[0] + exps[1] + exps[2] + exps[3]
    gates = [e / denom for e in exps]

    oh_all = jnp.zeros((n, LANES), _F32)
    for oh in onehots:
        oh_all = oh_all + jnp.where(oh, 1.0, 0.0)
    ri = lax.broadcasted_iota(jnp.int32, (n, n), 0)
    ci = lax.broadcasted_iota(jnp.int32, (n, n), 1)
    tri = jnp.where(ci < ri, 1.0, 0.0).astype(_BF16)
    base = _dot(tri, oh_all.astype(_BF16)) + carry_ref[0:1, :]
    ranks = [jnp.sum(jnp.where(oh, base, 0.0), axis=-1, keepdims=True) for oh in onehots]
    carry_ref[...] = carry_ref[...] + jnp.sum(oh_all, axis=0, keepdims=True)

    meta_i = jnp.zeros((n, LANES), jnp.int32)
    meta_f = jnp.zeros((n, LANES), _F32)
    for k in range(TOP_K):
        meta_i = jnp.where(lane == k, idxs[k], meta_i)
        meta_i = jnp.where(lane == TOP_K + k, ranks[k].astype(jnp.int32), meta_i)
        meta_f = jnp.where(lane == k, gates[k], meta_f)
    return meta_i, meta_f


def _mixer_prompt_kernel(x_ref, gmix_ref, win_ref, lnvg_ref, lnvb_ref, wsp_ref, bsp_ref, wa_ref,
                         cw_ref, cb_ref, lncg_ref, lncb_ref, wb_ref, bb_ref, bga_ref, bgb_ref,
                         wout_ref, gffn_ref, wr_ref, br_ref,
                         x1_ref, h2_ref, mi_ref, mf_ref, cnt_ref, cst_ref,
                         cbuf, carry):
    b = pl.program_id(0)
    t = pl.program_id(1)

    @pl.when((b == 0) & (t == 0))
    def _():
        carry[...] = jnp.zeros_like(carry)

    @pl.when(t == 0)
    def _():
        cbuf[0:HIST, :] = jnp.zeros((HIST, D_MODEL), _F32)
        cbuf[HIST + TT:, :] = jnp.zeros((SUBLANES, D_MODEL), _F32)

    x = x_ref[...]
    hb = _rms(x, gmix_ref[...]).astype(_BF16)

    def proj(s):
        return _dot(hb, win_ref[:, s * D_MODEL:(s + 1) * D_MODEL])

    glu = proj(2) * _sigmoid(proj(3))
    cbuf[HIST:HIST + TT, :] = glu
    off = HIST - (CONV_WIDTH - 1)
    span = HIST + CONV_ROWS + SUBLANES
    conv_cols = []
    other, half, pieces = (0, 1, 4, 5), D_MODEL // 2, []
    order_after = None
    for c in range(D_MODEL // LANES):
        cols = slice(c * LANES, (c + 1) * LANES)
        accs = []
        for r in range(TT // CONV_ROWS):
            acc = jnp.zeros((CONV_ROWS, LANES), _F32) + cb_ref[:, cols]
            window = cbuf[r * CONV_ROWS:r * CONV_ROWS + span, cols]
            if order_after is not None:
                zero = pltpu.bitcast(lax.shift_right_logical(
                    pltpu.bitcast(order_after, jnp.uint32), jnp.uint32(32)), _F32)
                window = window + jnp.concatenate([zero] * (span // SUBLANES), axis=0)
            for sh in range(SUBLANES):
                shifted = window if sh == 0 else pltpu.roll(window, span - sh, axis=0)
                for a in range(HIST // SUBLANES + 1):
                    k = a * SUBLANES + sh - off
                    if 0 <= k < CONV_WIDTH:
                        acc = acc + (cw_ref[k:k + 1, cols]
                                     * shifted[a * SUBLANES:a * SUBLANES + CONV_ROWS, :])
            accs.append(acc)
            order_after = acc[:SUBLANES, :]
        conv_cols.append(jnp.concatenate(accs, axis=0))
        lo = other[c // 2] * D_MODEL + (c % 2) * half
        pieces.append(_dot(hb, win_ref[:, lo:lo + half]))
    conv = jnp.concatenate(conv_cols, axis=1)
    p_u, p_v, p_ga, p_gb = [jnp.concatenate(pieces[2 * i:2 * i + 2], axis=1) for i in range(4)]
    cbuf[0:HIST, :] = cbuf[TT:TT + HIST, :]
    cn = _ln(conv, lncg_ref[...], lncb_ref[...])
    y_b = _dot((cn * _sigmoid(cn)).astype(_BF16), wb_ref[...]) + bb_ref[...]

    u = _gelu(p_u)
    v = _ln(_gelu(p_v), lnvg_ref[...], lnvb_ref[...])
    ri = lax.broadcasted_iota(jnp.int32, (CHUNK, CHUNK), 0)
    ci = lax.broadcasted_iota(jnp.int32, (CHUNK, CHUNK), 1)
    n_chunks = TT // CHUNK
    vb = v.astype(_BF16)
    s_groups = []
    for g in range(A_GROUPS):
        wm = jnp.where(ci <= ri, wsp_ref[g], 0.0).astype(_BF16)
        cols = slice(g * LANES, (g + 1) * LANES)
        s_groups.append(_dot(wm, jnp.concatenate(
            [vb[c * CHUNK:(c + 1) * CHUNK, cols] for c in range(n_chunks)], axis=1)))
    s = jnp.concatenate(
        [jnp.concatenate([sg[:, c * LANES:(c + 1) * LANES] for sg in s_groups], axis=1)
         + bsp_ref[...] for c in range(n_chunks)], axis=0)
    y_a = _dot((u * s).astype(_BF16), wa_ref[...])

    merged = _sigmoid(p_ga + bga_ref[...]) * y_a + _sigmoid(p_gb + bgb_ref[...]) * y_b
    x1 = x + _dot(merged.astype(_BF16), wout_ref[...])
    x1_ref[...] = x1

    h2 = _rms(x1, gffn_ref[...])
    _store_row_tiles(h2_ref, (), TT, h2)
    meta_i, meta_f = _route(h2, wr_ref, br_ref, carry)
    mi_ref[...] = meta_i[:, :META]
    mf_ref[...] = meta_f
    cnt_ref[...] = carry[...]

    @pl.when(t == pl.num_programs(1) - 1)
    def _():
        cst_ref[...] = cbuf[HIST - (CONV_WIDTH - 1):HIST, :]


def _full(shape):
    return pl.BlockSpec(shape, lambda *_: (0,) * len(shape))


def _mixer_prompt(x, w):
    bsz, seq, d = x.shape
    n_tok = bsz * seq
    nt = seq // TT
    tok = lambda b, t: (b * nt + t, 0)
    weights = (w["gmix"], w["win"], w["lnvg"], w["lnvb"], w["wsp"], w["bsp"], w["wa"], w["cw"],
               w["cb"], w["lncg"], w["lncb"], w["wb"], w["bb"], w["bga"], w["bgb"], w["wout"],
               w["gffn"], w["wr"], w["br"])
    return pl.pallas_call(
        _mixer_prompt_kernel,
        grid=(bsz, nt),
        in_specs=[pl.BlockSpec((None, TT, d), lambda b, t: (b, t, 0))]
                 + [_full(a.shape) for a in weights],
        out_specs=[
            pl.BlockSpec((TT, d), tok),
            pl.BlockSpec((TT * ROW_TILES, LANES), tok),
            pl.BlockSpec((TT, META), tok),
            pl.BlockSpec((TT, LANES), tok),
            _full((SUBLANES, LANES)),
            pl.BlockSpec((None, CONV_WIDTH - 1, d), lambda b, t: (b, 0, 0)),
        ],
        out_shape=[
            jax.ShapeDtypeStruct((n_tok, d), _F32),
            jax.ShapeDtypeStruct((n_tok * ROW_TILES, LANES), _F32),
            jax.ShapeDtypeStruct((n_tok, META), jnp.int32),
            jax.ShapeDtypeStruct((n_tok, LANES), _F32),
            jax.ShapeDtypeStruct((SUBLANES, LANES), _F32),
            jax.ShapeDtypeStruct((bsz, CONV_WIDTH - 1, d), _F32),
        ],
        scratch_shapes=[pltpu.VMEM((HIST + TT + SUBLANES, d), _F32),
                        pltpu.VMEM((SUBLANES, LANES), _F32)],
        compiler_params=pltpu.CompilerParams(
            dimension_semantics=("arbitrary", "arbitrary"), vmem_limit_bytes=VMEM_LIMIT),
        name="mixer_prompt",
    )(x, *weights)


def _mixer_sample_kernel(x_ref, cache_ref, cnt0_ref, gmix_ref, win_ref, lnvg_ref, lnvb_ref,
                         wcs_ref, bcs_ref, wa_ref, cw_ref, cb_ref, lncg_ref, lncb_ref, wb_ref,
                         bb_ref, bga_ref, bgb_ref, wout_ref, gffn_ref, wr_ref, br_ref,
                         x1_ref, h2_ref, mi_ref, mf_ref, cnt_ref, cst_ref, v_ref,
                         carry):
    nt = x_ref.shape[0]
    n = nt * SB

    @pl.when(pl.program_id(0) == 0)
    def _():
        carry[...] = cnt0_ref[...]

    x = x_ref[...].reshape(n, D_MODEL)
    hb = _rms(x, gmix_ref[...]).astype(_BF16)

    def proj(s):
        return _dot(hb, win_ref[:, s * D_MODEL:(s + 1) * D_MODEL])

    u = _gelu(proj(0))
    v = _ln(_gelu(proj(1)), lnvg_ref[...], lnvb_ref[...])
    v_ref[...] = v.reshape(nt, SB, D_MODEL)
    s_rows = []
    for i in range(nt):
        acc = jnp.zeros((SB, D_MODEL), _F32) + bcs_ref[i:i + 1, :]
        for j in range(i + 1):
            acc = acc + wcs_ref[i * nt + j:i * nt + j + 1, :] * v[j * SB:(j + 1) * SB, :]
        s_rows.append(acc)
    s = jnp.concatenate(s_rows, axis=0)
    y_a = _dot((u * s).astype(_BF16), wa_ref[...])

    glu = proj(2) * _sigmoid(proj(3))
    hist = CONV_WIDTH - 1
    cst_ref[0:hist - nt] = cache_ref[nt:hist]
    cst_ref[hist - nt:hist] = glu.reshape(nt, SB, D_MODEL)
    conv_rows = []
    for i in range(nt):
        acc = jnp.zeros((SB, D_MODEL), _F32) + cb_ref[...]
        for k in range(CONV_WIDTH):
            p = i + k
            src = cache_ref[p] if p < hist else glu[(p - hist) * SB:(p - hist + 1) * SB, :]
            acc = acc + cw_ref[k:k + 1, :] * src
        conv_rows.append(acc)
    cn = _ln(jnp.concatenate(conv_rows, axis=0), lncg_ref[...], lncb_ref[...])
    y_b = _dot((cn * _sigmoid(cn)).astype(_BF16), wb_ref[...]) + bb_ref[...]

    merged = (_sigmoid(proj(4) + bga_ref[...]) * y_a + _sigmoid(proj(5) + bgb_ref[...]) * y_b)
    x1 = x + _dot(merged.astype(_BF16), wout_ref[...])
    x1_ref[...] = x1.reshape(nt, SB, D_MODEL)

    h2 = _rms(x1, gffn_ref[...])
    for i in range(nt):
        _store_row_tiles(h2_ref, (i,), SB, h2[i * SB:(i + 1) * SB, :])
    meta_i, meta_f = _route(h2, wr_ref, br_ref, carry)
    mi_ref[...] = meta_i[:, :META].reshape(nt, SB, META)
    mf_ref[...] = meta_f.reshape(nt, SB, LANES)
    cnt_ref[...] = carry[...]


def _mixer_sample(x_t, cache_t, cnt0, w):
    nt, nseq, d = x_t.shape
    hist = cache_t.shape[0]
    weights = (w["gmix"], w["win"], w["lnvg"], w["lnvb"], w["wcs"], w["bcs"], w["wa"], w["cw"],
               w["cb"], w["lncg"], w["lncb"], w["wb"], w["bb"], w["bga"], w["bgb"], w["wout"],
               w["gffn"], w["wr"], w["br"])
    seqs = lambda i: (0, i, 0)
    return pl.pallas_call(
        _mixer_sample_kernel,
        grid=(nseq // SB,),
        in_specs=[pl.BlockSpec((nt, SB, d), seqs), pl.BlockSpec((hist, SB, d), seqs),
                  _full(cnt0.shape)] + [_full(a.shape) for a in weights],
        out_specs=[
            pl.BlockSpec((nt, SB, d), seqs),
            pl.BlockSpec((nt, SB * ROW_TILES, LANES), seqs),
            pl.BlockSpec((nt, SB, META), seqs),
            pl.BlockSpec((nt, SB, LANES), seqs),
            _full((SUBLANES, LANES)),
            pl.BlockSpec((hist, SB, d), seqs),
            pl.BlockSpec((nt, SB, d), seqs),
        ],
        out_shape=[
            jax.ShapeDtypeStruct((nt, nseq, d), _F32),
            jax.ShapeDtypeStruct((nt, nseq * ROW_TILES, LANES), _F32),
            jax.ShapeDtypeStruct((nt, nseq, META), jnp.int32),
            jax.ShapeDtypeStruct((nt, nseq, LANES), _F32),
            jax.ShapeDtypeStruct((SUBLANES, LANES), _F32),
            jax.ShapeDtypeStruct((hist, nseq, d), _F32),
            jax.ShapeDtypeStruct((nt, nseq, d), _F32),
        ],
        scratch_shapes=[pltpu.VMEM((SUBLANES, LANES), _F32)],
        compiler_params=pltpu.CompilerParams(
            dimension_semantics=("arbitrary",), vmem_limit_bytes=VMEM_LIMIT),
        name="mixer_sample",
    )(x_t, cache_t, cnt0, *weights)


def _row_tile(ref, row):
    return ref.at[pl.ds(pl.multiple_of(row * ROW_TILES, ROW_TILES), ROW_TILES), :]


def _dispatch_kernel(n_first, n_tiles, n_blk, pad_start_ref, pad_len_ref, nused_ref, dest_ref,
                     h2a_ref, h2b_ref, xs_hbm, zbuf, sem, zsem):
    i = pl.program_id(0)

    def fill(start):
        def go(copy):
            copy.start() if start else copy.wait()

        def per_expert(e, carry):
            first, length = pad_start_ref[e], pad_len_ref[e]
            for s in range(1, BLK.bit_length()):
                p = BLK >> s

                @pl.when((length & p) != 0)
                def _():
                    row = first + (length & (-2 * p))
                    go(pltpu.make_async_copy(
                        zbuf.at[pl.ds(0, p * ROW_TILES), :],
                        xs_hbm.at[pl.ds(pl.multiple_of(row * ROW_TILES, ROW_TILES), p * ROW_TILES), :],
                        zsem))
            return carry

        def per_tail_block(b, carry):
            go(pltpu.make_async_copy(
                zbuf, xs_hbm.at[pl.ds(pl.multiple_of(b * BLK * ROW_TILES, ROW_TILES),
                                      BLK * ROW_TILES), :], zsem))
            return carry

        lax.fori_loop(0, N_EXPERTS, per_expert, 0)
        lax.fori_loop(nused_ref[0], n_blk, per_tail_block, 0)

    @pl.when(i == 0)
    def _():
        zbuf[...] = jnp.zeros_like(zbuf)
        fill(True)

    def scatter(h2_ref):
        def issue(t, carry):
            src = _row_tile(h2_ref, t)
            for k in range(TOP_K):
                pltpu.make_async_copy(src, _row_tile(xs_hbm, dest_ref[0, 0, t * TOP_K + k]),
                                      sem).start(priority=k % 2)
            return carry

        lax.fori_loop(0, TD, issue, 0)
        for _ in range(TOP_K):
            pltpu.make_async_copy(h2_ref, xs_hbm.at[pl.ds(0, TD * ROW_TILES), :], sem).wait()

    @pl.when(i < n_first)
    def _():
        scatter(h2a_ref)

    @pl.when(i >= n_first)
    def _():
        scatter(h2b_ref)

    @pl.when(i == n_tiles - 1)
    def _():
        fill(False)


def _dispatch(pad_start, pad_len, nused, dest, h2_a, h2_b, n_blk):
    n_tiles = dest.shape[0]
    n_first = h2_a.shape[0] // (TD * ROW_TILES)
    tile = (TD * ROW_TILES, LANES)
    grid_spec = pltpu.PrefetchScalarGridSpec(
        num_scalar_prefetch=3,
        grid=(n_tiles,),
        in_specs=[
            pl.BlockSpec((1, 1, TD * TOP_K), lambda i, *_: (i, 0, 0), memory_space=pltpu.SMEM),
            pl.BlockSpec(tile, lambda i, *_: (jnp.minimum(i, n_first - 1), 0)),
            pl.BlockSpec(tile, lambda i, *_: (jnp.maximum(i - n_first, 0), 0)),
        ],
        out_specs=pl.BlockSpec(memory_space=pl.ANY),
        scratch_shapes=[pltpu.VMEM((BLK * ROW_TILES, LANES), _F32), pltpu.SemaphoreType.DMA,
                        pltpu.SemaphoreType.DMA],
    )
    return pl.pallas_call(
        functools.partial(_dispatch_kernel, n_first, n_tiles, n_blk),
        grid_spec=grid_spec,
        out_shape=jax.ShapeDtypeStruct((n_blk * BLK * ROW_TILES, LANES), _F32),
        compiler_params=pltpu.CompilerParams(
            dimension_semantics=("arbitrary",), has_side_effects=True),
        name="dispatch",
    )(pad_start, pad_len, nused, dest, h2_a, h2_b)


def _experts_kernel(be_ref, nused_ref, nvalid_ref, slot_ref, next_ref, xs_ref, bup_ref, bdn_ref,
                    wup_hbm, wdn_hbm, out_ref, wup_buf, wdn_buf, sems):
    del nused_ref
    i = pl.program_id(0)
    expert = be_ref[i]
    slot = slot_ref[i]
    first_of_run = (i == 0) | (expert != be_ref[jnp.maximum(i - 1, 0)])

    def weight_copies(e, s):
        return (pltpu.make_async_copy(wup_hbm.at[e], wup_buf.at[s], sems.at[s]),
                pltpu.make_async_copy(wdn_hbm.at[e], wdn_buf.at[s], sems.at[s]))

    @pl.when(i == 0)
    def _():
        for copy in weight_copies(expert, slot):
            copy.start()

    @pl.when(first_of_run & (next_ref[i] >= 0))
    def _():
        for copy in weight_copies(next_ref[i], 1 - slot):
            copy.start()

    @pl.when(first_of_run)
    def _():
        for copy in weight_copies(expert, slot):
            copy.wait()

    wup = wup_buf.at[slot]
    wdn = wdn_buf.at[slot]

    def first_rows(n):
        if n:
            x = _load_row_tiles(xs_ref, 0, n)
            gate = _dot(x, wup[:, :D_EXPERT]) + bup_ref[:, :D_EXPERT]
            up = _dot(x, wup[:, D_EXPERT:]) + bup_ref[:, D_EXPERT:]
            gate = jnp.minimum(gate, SWIGLU_LIMIT)
            up = jnp.clip(up, -SWIGLU_LIMIT, SWIGLU_LIMIT)
            act = (up + 1.0) * (gate * _sigmoid(SWIGLU_ALPHA * gate))
            o = _dot(act, wdn[...]) + bdn_ref[...]
            _store_row_tiles(out_ref, (), n, o)
        if n < BLK:
            out_ref[n * ROW_TILES:, :] = jnp.zeros(((BLK - n) * ROW_TILES, LANES), _F32)

    n_valid = nvalid_ref[i]
    for live in range(BLK // SUB + 1):
        lo, hi = (live - 1) * SUB, live * SUB

        @pl.when((n_valid > lo) & (n_valid <= hi) if live else n_valid <= 0)
        def _():
            first_rows(live * SUB)


def _experts(block_expert, nused, nvalid, slot, next_expert, xs, w_up, b_up, w_down, b_down):
    n_blk = block_expert.shape[0]
    rows = lambda i, be, nu, *_: (jnp.minimum(i, nu[0] - 1), 0)
    expert = lambda i, be, *_: (be[i], 0, 0)
    d, f2 = w_up.shape[1], w_up.shape[2]
    grid_spec = pltpu.PrefetchScalarGridSpec(
        num_scalar_prefetch=5,
        grid=(n_blk,),
        in_specs=[
            pl.BlockSpec((BLK * ROW_TILES, LANES), rows),
            pl.BlockSpec((None, 1, f2), expert),
            pl.BlockSpec((None, 1, d), expert),
            pl.BlockSpec(memory_space=pl.ANY),
            pl.BlockSpec(memory_space=pl.ANY),
        ],
        out_specs=pl.BlockSpec((BLK * ROW_TILES, LANES), lambda i, *_: (i, 0)),
        scratch_shapes=[pltpu.VMEM((2, d, f2), _F32), pltpu.VMEM((2, w_down.shape[1], d), _F32),
                        pltpu.SemaphoreType.DMA((2,))],
    )
    return pl.pallas_call(
        _experts_kernel,
        grid_spec=grid_spec,
        out_shape=jax.ShapeDtypeStruct(xs.shape, _F32),
        compiler_params=pltpu.CompilerParams(
            dimension_semantics=("arbitrary",), vmem_limit_bytes=VMEM_LIMIT),
        name="experts",
    )(block_expert, nused, nvalid, slot, next_expert, xs, b_up, b_down, w_up, w_down)


def _combine_kernel(n, dest_ref, dest_next_ref, x1_ref, mf_ref, gfin_ref, outs_hbm, y_ref, buf,
                    sems):
    i = pl.program_id(0)

    def gather(d_ref, slot):
        def issue(t, carry):
            for k in range(TOP_K):
                dst = buf.at[slot, pl.ds(pl.multiple_of((k * TD + t) * ROW_TILES, ROW_TILES),
                                         ROW_TILES), :]
                pltpu.make_async_copy(_row_tile(outs_hbm, d_ref[0, 0, t * TOP_K + k]), dst,
                                      sems.at[slot]).start(priority=k % 2)
            return carry

        lax.fori_loop(0, TD, issue, 0)

    slot = i % 2

    @pl.when(i == 0)
    def _():
        gather(dest_ref, 0)

    @pl.when(i + 1 < n)
    def _():
        gather(dest_next_ref, 1 - slot)

    pltpu.make_async_copy(outs_hbm.at[pl.ds(0, buf.shape[1]), :], buf.at[slot], sems.at[slot]).wait()

    y = x1_ref[...]
    gates = mf_ref[...]
    rows = buf.at[slot]
    for k in range(TOP_K):
        y = y + gates[:, k:k + 1] * _load_row_tiles(rows, k * TD * ROW_TILES, TD)
    y_ref[...] = _rms(y, gfin_ref[...])


def _combine(dest, x1, meta_f, g_final, outs):
    n_tok, d = x1.shape
    n_tiles = n_tok // TD
    tok = lambda i: (i, 0)
    return pl.pallas_call(
        functools.partial(_combine_kernel, n_tiles),
        grid=(n_tiles,),
        in_specs=[
            pl.BlockSpec((1, 1, TD * TOP_K), lambda i: (i, 0, 0), memory_space=pltpu.SMEM),
            pl.BlockSpec((1, 1, TD * TOP_K), lambda i: (jnp.minimum(i + 1, n_tiles - 1), 0, 0),
                         memory_space=pltpu.SMEM),
            pl.BlockSpec((TD, d), tok),
            pl.BlockSpec((TD, LANES), tok),
            _full(g_final.shape),
            pl.BlockSpec(memory_space=pl.ANY),
        ],
        out_specs=pl.BlockSpec((TD, d), tok),
        out_shape=jax.ShapeDtypeStruct((n_tok, d), _F32),
        scratch_shapes=[pltpu.VMEM((2, TOP_K * TD * ROW_TILES, LANES), _F32),
                        pltpu.SemaphoreType.DMA((2,))],
        compiler_params=pltpu.CompilerParams(
            dimension_semantics=("arbitrary",), vmem_limit_bytes=VMEM_LIMIT),
        name="combine",
    )(dest, dest, x1, meta_f, g_final, outs)


def _prep_weights(norm_mix_g, w_in, ln_v_g, ln_v_b, w_spatial, b_spatial, w_branch_a, conv_w,
                  conv_b, ln_conv_g, ln_conv_b, w_branch_b, b_branch_b, b_gate_a, b_gate_b, w_out,
                  norm_ffn_g, w_router, b_router, n_new):
    row = lambda a: a.reshape(1, -1)
    pad = LANES - N_EXPERTS
    wr = jnp.pad(w_router, ((0, 0), (0, pad)))
    wr_hi = wr.astype(_BF16)
    wr_lo = (wr - wr_hi.astype(_F32)).astype(_BF16)
    wcs = jnp.repeat(w_spatial[:, :n_new, :n_new].transpose(1, 2, 0).reshape(n_new * n_new, A_GROUPS),
                     LANES, axis=1)
    return dict(
        gmix=row(norm_mix_g), win=w_in.astype(_BF16), lnvg=row(ln_v_g), lnvb=row(ln_v_b),
        wsp=w_spatial, bsp=jnp.repeat(b_spatial.T, LANES, axis=1), wcs=wcs,
        bcs=jnp.repeat(b_spatial[:, :n_new].T, LANES, axis=1),
        wa=w_branch_a.astype(_BF16), cw=conv_w, cb=row(conv_b), lncg=row(ln_conv_g),
        lncb=row(ln_conv_b), wb=w_branch_b.astype(_BF16), bb=row(b_branch_b), bga=row(b_gate_a),
        bgb=row(b_gate_b), wout=w_out.astype(_BF16), gffn=row(norm_ffn_g),
        wr=jnp.concatenate([wr_hi, wr_lo], axis=1), br=jnp.pad(row(b_router), ((0, 0), (0, pad))))


def kernel(x_prompt, x_sample, cache_conv, norm_mix_g, w_in, ln_v_g, ln_v_b, w_spatial, b_spatial, w_branch_a, conv_w, conv_b, ln_conv_g, ln_conv_b, w_branch_b, b_branch_b, b_gate_a, b_gate_b, w_out, norm_ffn_g, w_router, b_router, w_up, b_up, w_down, b_down, norm_final_g):
    assert w_in.shape[0] == 1, "single trunk layer"
    bsz, seq, d = x_prompt.shape
    nseq, n_new, _ = x_sample.shape
    assert d == D_MODEL and seq % TT == 0 and TT % CHUNK == 0 and nseq % SB == 0
    assert n_new <= CHUNK and (nseq * n_new) % TD == 0 and (bsz * seq) % TD == 0
    w = _prep_weights(norm_mix_g[0], w_in[0], ln_v_g[0], ln_v_b[0], w_spatial[0], b_spatial[0],
                      w_branch_a[0], conv_w[0], conv_b[0], ln_conv_g[0], ln_conv_b[0],
                      w_branch_b[0], b_branch_b[0], b_gate_a[0], b_gate_b[0], w_out[0],
                      norm_ffn_g[0], w_router[0], b_router[0], n_new)

    x1_p, h2_p, mi_p, mf_p, cnt_p, cst_p = _mixer_prompt(x_prompt, w)
    x_t = x_sample.transpose(1, 0, 2)
    cache_t = cache_conv[0].transpose(1, 0, 2)
    x1_s, h2_s, mi_s, mf_s, cnt, cst_s, v_s = _mixer_sample(x_t, cache_t, cnt_p, w)
    n_p, n_s = bsz * seq, nseq * n_new
    x1_s = x1_s.reshape(n_s, d)
    h2_s = h2_s.reshape(n_s * ROW_TILES, LANES)
    mi_s = mi_s.reshape(n_s, META)
    mf_s = mf_s.reshape(n_s, LANES)

    counts = cnt[0, :N_EXPERTS].astype(jnp.int32)
    padded = ((counts + BLK - 1) // BLK) * BLK
    pend = jnp.cumsum(padded)
    pstart = pend - padded
    n_assign = (n_p + n_s) * TOP_K
    n_blk = -(-(n_assign + N_EXPERTS * (BLK - 1)) // BLK)
    nused = (pend[-1] // BLK).astype(jnp.int32)
    blk_ids = jnp.arange(n_blk, dtype=jnp.int32)
    first_row = jnp.minimum(blk_ids, nused - 1) * BLK
    be = jnp.sum((pend[None, :] <= first_row[:, None]).astype(jnp.int32), axis=1)
    be = jnp.minimum(be, N_EXPERTS - 1)
    experts_row = jnp.arange(N_EXPERTS, dtype=jnp.int32)

    def dest_of(mi):
        sel = mi[:, :TOP_K, None] == experts_row[None, None, :]
        dest = jnp.sum(jnp.where(sel, pstart[None, None, :], 0), axis=-1) + mi[:, TOP_K:2 * TOP_K]
        return dest.reshape(-1, 1, TD * TOP_K)

    dest_p, dest_s = dest_of(mi_p), dest_of(mi_s)
    xs = _dispatch(pstart + counts, padded - counts, nused.reshape(1),
                   jnp.concatenate([dest_p, dest_s], axis=0), h2_p, h2_s, n_blk)
    of_block = be[:, None] == experts_row[None, :]
    row_in_expert = blk_ids * BLK - jnp.sum(jnp.where(of_block, pstart[None, :], 0), axis=1)
    nvalid = jnp.clip(jnp.sum(jnp.where(of_block, counts[None, :], 0), axis=1) - row_in_expert,
                      0, BLK)
    nvalid = jnp.where(blk_ids < nused, nvalid, 0).astype(jnp.int32)
    used = counts > 0
    slot_e = (jnp.cumsum(used.astype(jnp.int32)) - 1) % 2
    later = jnp.where(used[None, :] & (experts_row[None, :] > experts_row[:, None]),
                      experts_row[None, :], N_EXPERTS)
    next_e = jnp.min(later, axis=1)
    next_e = jnp.where(next_e < N_EXPERTS, next_e, -1)
    slot = jnp.sum(jnp.where(of_block, slot_e[None, :], 0), axis=1).astype(jnp.int32)
    next_expert = jnp.sum(jnp.where(of_block, next_e[None, :], 0), axis=1).astype(jnp.int32)
    outs = _experts(be, nused.reshape(1), nvalid, slot, next_expert, xs, w_up[0],
                    b_up[0][:, None, :], w_down[0], b_down[0][:, None, :])
    g_final = norm_final_g.reshape(1, d)
    y_p = _combine(dest_p, x1_p, mf_p, g_final, outs)
    y_s = _combine(dest_s, x1_s, mf_s, g_final, outs)

    y_prompt = y_p.reshape(bsz, seq, d)
    y_sample = y_s.reshape(n_new, nseq, d).transpose(1, 0, 2)
    conv_state_prompt = cst_p[None]
    conv_state_sample = cst_s.transpose(1, 0, 2)[None]
    chunk_v_sample = v_s.transpose(1, 0, 2)[None]
    return (y_prompt, y_sample, conv_state_prompt, conv_state_sample, chunk_v_sample)
```

```python
import functools

import jax
import jax.numpy as jnp
from jax import lax
from jax.experimental import pallas as pl
from jax.experimental.pallas import tpu as pltpu

D_MODEL = 1024
CHUNK = 128
A_GROUPS = 8
CONV_WIDTH = 31
HIST = 32
N_EXPERTS = 32
TOP_K = 4
META = 2 * TOP_K
D_EXPERT = 1024
SWIGLU_LIMIT = 7.0
SWIGLU_ALPHA = 1.702
EPS = 1e-5
LANES = 128
SUBLANES = 8
ROW_TILES = D_MODEL // LANES
TT = 256
TD = 512
CONV_ROWS = 64
SB = 32
BLK = 512
SUB = 128
VMEM_LIMIT = 56 * 1024 * 1024

_F32 = jnp.float32
_BF16 = jnp.bfloat16
_INV_SQRT2 = 0.7071067811865476


def _rms(x, g):
    return x * lax.rsqrt(jnp.mean(x * x, axis=-1, keepdims=True) + EPS) * g


def _ln(x, g, b):
    mu = jnp.mean(x, axis=-1, keepdims=True)
    xc = x - mu
    var = jnp.mean(xc * xc, axis=-1, keepdims=True)
    return xc * lax.rsqrt(var + EPS) * g + b


def _gelu(x):
    return 0.5 * x * (1.0 + lax.erf(x * _INV_SQRT2))


def _sigmoid(x):
    return 1.0 / (1.0 + jnp.exp(-x))


def _dot(a, b):
    return jnp.dot(a, b, preferred_element_type=_F32)


def _store_row_tiles(ref, lead, n, val, start=0):
    for j in range(ROW_TILES):
        ref[lead + (pl.ds(start + j, n, stride=ROW_TILES), slice(None))] = (
            val[:, j * LANES:(j + 1) * LANES])


def _load_row_tiles(ref, start, n):
    return jnp.concatenate(
        [ref[pl.ds(start + j, n, stride=ROW_TILES), :] for j in range(ROW_TILES)], axis=1)


def _route(h2, wr_ref, br_ref, carry_ref):
    n = h2.shape[0]
    h_hi = h2.astype(_BF16)
    h_lo = (h2 - h_hi.astype(_F32)).astype(_BF16)
    by_hi = _dot(h_hi, wr_ref[...])
    logits = (by_hi[:, :LANES] + by_hi[:, LANES:] + _dot(h_lo, wr_ref[:, :LANES])) + br_ref[...]
    lane = lax.broadcasted_iota(jnp.int32, (n, LANES), 1)
    l = jnp.where(lane < N_EXPERTS, logits, -jnp.inf)
    onehots, vals, idxs = [], [], []
    for _ in range(TOP_K):
        m = jnp.max(l, axis=-1, keepdims=True)
        idx = jnp.min(jnp.where(l == m, lane, LANES), axis=-1, keepdims=True)
        oh = lane == idx
        onehots.append(oh)
        vals.append(m)
        idxs.append(idx)
        l = jnp.where(oh, -jnp.inf, l)
    exps = [jnp.exp(v - vals[0]) for v in vals]
    denom = exps[0] + exps[1] + exps[2] + exps[3]
    gates = [e / denom for e in exps]

    oh_all = jnp.zeros((n, LANES), _F32)
    for oh in onehots:
        oh_all = oh_all + jnp.where(oh, 1.0, 0.0)
    ri = lax.broadcasted_iota(jnp.int32, (n, n), 0)
    ci = lax.broadcasted_iota(jnp.int32, (n, n), 1)
    tri = jnp.where(ci < ri, 1.0, 0.0).astype(_BF16)
    base = _dot(tri, oh_all.astype(_BF16)) + carry_ref[0:1, :]
    ranks = [jnp.sum(jnp.where(oh, base, 0.0), axis=-1, keepdims=True) for oh in onehots]
    carry_ref[...] = carry_ref[...] + jnp.sum(oh_all, axis=0, keepdims=True)

    meta_i = jnp.zeros((n, LANES), jnp.int32)
    meta_f = jnp.zeros((n, LANES), _F32)
    for k in range(TOP_K):
        meta_i = jnp.where(lane == k, idxs[k], meta_i)
        meta_i = jnp.where(lane == TOP_K + k, ranks[k].astype(jnp.int32), meta_i)
        meta_f = jnp.where(lane == k, gates[k], meta_f)
    return meta_i, meta_f


def _mixer_prompt_kernel(x_ref, gmix_ref, win_ref, lnvg_ref, lnvb_ref, wsp_ref, bsp_ref, wa_ref,
                         cw_ref, cb_ref, lncg_ref, lncb_ref, wb_ref, bb_ref, bga_ref, bgb_ref,
                         wout_ref, gffn_ref, wr_ref, br_ref,
                         x1_ref, h2_ref, mi_ref, mf_ref, cnt_ref, cst_ref,
                         cbuf, carry):
    b = pl.program_id(0)
    t = pl.program_id(1)

    @pl.when((b == 0) & (t == 0))
    def _():
        carry[...] = jnp.zeros_like(carry)

    @pl.when(t == 0)
    def _():
        cbuf[0:HIST, :] = jnp.zeros((HIST, D_MODEL), _F32)
        cbuf[HIST + TT:, :] = jnp.zeros((SUBLANES, D_MODEL), _F32)

    x = x_ref[...]
    hb = _rms(x, gmix_ref[...]).astype(_BF16)

    def proj(s):
        return _dot(hb, win_ref[:, s * D_MODEL:(s + 1) * D_MODEL])

    glu = proj(2) * _sigmoid(proj(3))
    cbuf[HIST:HIST + TT, :] = glu
    off = HIST - (CONV_WIDTH - 1)
    span = HIST + CONV_ROWS + SUBLANES
    conv_cols = []
    other, half, pieces = (0, 1, 4, 5), D_MODEL // 2, []
    order_after = None
    for c in range(D_MODEL // LANES):
        cols = slice(c * LANES, (c + 1) * LANES)
        accs = []
        for r in range(TT // CONV_ROWS):
            acc = jnp.zeros((CONV_ROWS, LANES), _F32) + cb_ref[:, cols]
            window = cbuf[r * CONV_ROWS:r * CONV_ROWS + span, cols]
            if order_after is not None:
                zero = pltpu.bitcast(lax.shift_right_logical(
                    pltpu.bitcast(order_after, jnp.uint32), jnp.uint32(32)), _F32)
                window = window + jnp.concatenate([zero] * (span // SUBLANES), axis=0)
            for sh in range(SUBLANES):
                shifted = window if sh == 0 else pltpu.roll(window, span - sh, axis=0)
                for a in range(HIST // SUBLANES + 1):
                    k = a * SUBLANES + sh - off
                    if 0 <= k < CONV_WIDTH:
                        acc = acc + (cw_ref[k:k + 1, cols]
                                     * shifted[a * SUBLANES:a * SUBLANES + CONV_ROWS, :])
            accs.append(acc)
            order_after = acc[:SUBLANES, :]
        conv_cols.append(jnp.concatenate(accs, axis=0))
        lo = other[c // 2] * D_MODEL + (c % 2) * half
        pieces.append(_dot(hb, win_ref[:, lo:lo + half]))
    conv = jnp.concatenate(conv_cols, axis=1)
    p_u, p_v, p_ga, p_gb = [jnp.concatenate(pieces[2 * i:2 * i + 2], axis=1) for i in range(4)]
    cbuf[0:HIST, :] = cbuf[TT:TT + HIST, :]
    cn = _ln(conv, lncg_ref[...], lncb_ref[...])
    y_b = _dot((cn * _sigmoid(cn)).astype(_BF16), wb_ref[...]) + bb_ref[...]

    u = _gelu(p_u)
    v = _ln(_gelu(p_v), lnvg_ref[...], lnvb_ref[...])
    ri = lax.broadcasted_iota(jnp.int32, (CHUNK, CHUNK), 0)
    ci = lax.broadcasted_iota(jnp.int32, (CHUNK, CHUNK), 1)
    n_chunks = TT // CHUNK
    vb = v.astype(_BF16)
    s_groups = []
    for g in range(A_GROUPS):
        wm = jnp.where(ci <= ri, wsp_ref[g], 0.0).astype(_BF16)
        cols = slice(g * LANES, (g + 1) * LANES)
        s_groups.append(_dot(wm, jnp.concatenate(
            [vb[c * CHUNK:(c + 1) * CHUNK, cols] for c in range(n_chunks)], axis=1)))
    s = jnp.concatenate(
        [jnp.concatenate([sg[:, c * LANES:(c + 1) * LANES] for sg in s_groups], axis=1)
         + bsp_ref[...] for c in range(n_chunks)], axis=0)
    y_a = _dot((u * s).astype(_BF16), wa_ref[...])

    merged = _sigmoid(p_ga + bga_ref[...]) * y_a + _sigmoid(p_gb + bgb_ref[...]) * y_b
    x1 = x + _dot(merged.astype(_BF16), wout_ref[...])
    x1_ref[...] = x1

    h2 = _rms(x1, gffn_ref[...])
    _store_row_tiles(h2_ref, (), TT, h2)
    meta_i, meta_f = _route(h2, wr_ref, br_ref, carry)
    mi_ref[...] = meta_i[:, :META]
    mf_ref[...] = meta_f
    cnt_ref[...] = carry[...]

    @pl.when(t == pl.num_programs(1) - 1)
    def _():
        cst_ref[...] = cbuf[HIST - (CONV_WIDTH - 1):HIST, :]


def _full(shape):
    return pl.BlockSpec(shape, lambda *_: (0,) * len(shape))


def _mixer_prompt(x, w):
    bsz, seq, d = x.shape
    n_tok = bsz * seq
    nt = seq // TT
    tok = lambda b, t: (b * nt + t, 0)
    weights = (w["gmix"], w["win"], w["lnvg"], w["lnvb"], w["wsp"], w["bsp"], w["wa"], w["cw"],
               w["cb"], w["lncg"], w["lncb"], w["wb"], w["bb"], w["bga"], w["bgb"], w["wout"],
               w["gffn"], w["wr"], w["br"])
    return pl.pallas_call(
        _mixer_prompt_kernel,
        grid=(bsz, nt),
        in_specs=[pl.BlockSpec((None, TT, d), lambda b, t: (b, t, 0))]
                 + [_full(a.shape) for a in weights],
        out_specs=[
            pl.BlockSpec((TT, d), tok),
            pl.BlockSpec((TT * ROW_TILES, LANES), tok),
            pl.BlockSpec((TT, META), tok),
            pl.BlockSpec((TT, LANES), tok),
            _full((SUBLANES, LANES)),
            pl.BlockSpec((None, CONV_WIDTH - 1, d), lambda b, t: (b, 0, 0)),
        ],
        out_shape=[
            jax.ShapeDtypeStruct((n_tok, d), _F32),
            jax.ShapeDtypeStruct((n_tok * ROW_TILES, LANES), _F32),
            jax.ShapeDtypeStruct((n_tok, META), jnp.int32),
            jax.ShapeDtypeStruct((n_tok, LANES), _F32),
            jax.ShapeDtypeStruct((SUBLANES, LANES), _F32),
            jax.ShapeDtypeStruct((bsz, CONV_WIDTH - 1, d), _F32),
        ],
        scratch_shapes=[pltpu.VMEM((HIST + TT + SUBLANES, d), _F32),
                        pltpu.VMEM((SUBLANES, LANES), _F32)],
        compiler_params=pltpu.CompilerParams(
            dimension_semantics=("arbitrary", "arbitrary"), vmem_limit_bytes=VMEM_LIMIT),
        name="mixer_prompt",
    )(x, *weights)


def _mixer_sample_kernel(x_ref, cache_ref, cnt0_ref, gmix_ref, win_ref, lnvg_ref, lnvb_ref,
                         wcs_ref, bcs_ref, wa_ref, cw_ref, cb_ref, lncg_ref, lncb_ref, wb_ref,
                         bb_ref, bga_ref, bgb_ref, wout_ref, gffn_ref, wr_ref, br_ref,
                         x1_ref, h2_ref, mi_ref, mf_ref, cnt_ref, cst_ref, v_ref,
                         carry):
    nt = x_ref.shape[0]
    n = nt * SB

    @pl.when(pl.program_id(0) == 0)
    def _():
        carry[...] = cnt0_ref[...]

    x = x_ref[...].reshape(n, D_MODEL)
    hb = _rms(x, gmix_ref[...]).astype(_BF16)

    def proj(s):
        return _dot(hb, win_ref[:, s * D_MODEL:(s + 1) * D_MODEL])

    u = _gelu(proj(0))
    v = _ln(_gelu(proj(1)), lnvg_ref[...], lnvb_ref[...])
    v_ref[...] = v.reshape(nt, SB, D_MODEL)
    s_rows = []
    for i in range(nt):
        acc = jnp.zeros((SB, D_MODEL), _F32) + bcs_ref[i:i + 1, :]
        for j in range(i + 1):
            acc = acc + wcs_ref[i * nt + j:i * nt + j + 1, :] * v[j * SB:(j + 1) * SB, :]
        s_rows.append(acc)
    s = jnp.concatenate(s_rows, axis=0)
    y_a = _dot((u * s).astype(_BF16), wa_ref[...])

    glu = proj(2) * _sigmoid(proj(3))
    hist = CONV_WIDTH - 1
    cst_ref[0:hist - nt] = cache_ref[nt:hist]
    cst_ref[hist - nt:hist] = glu.reshape(nt, SB, D_MODEL)
    conv_rows = []
    for i in range(nt):
        acc = jnp.zeros((SB, D_MODEL), _F32) + cb_ref[...]
        for k in range(CONV_WIDTH):
            p = i + k
            src = cache_ref[p] if p < hist else glu[(p - hist) * SB:(p - hist + 1) * SB, :]
            acc = acc + cw_ref[k:k + 1, :] * src
        conv_rows.append(acc)
    cn = _ln(jnp.concatenate(conv_rows, axis=0), lncg_ref[...], lncb_ref[...])
    y_b = _dot((cn * _sigmoid(cn)).astype(_BF16), wb_ref[...]) + bb_ref[...]

    merged = (_sigmoid(proj(4) + bga_ref[...]) * y_a + _sigmoid(proj(5) + bgb_ref[...]) * y_b)
    x1 = x + _dot(merged.astype(_BF16), wout_ref[...])
    x1_ref[...] = x1.reshape(nt, SB, D_MODEL)

    h2 = _rms(x1, gffn_ref[...])
    for i in range(nt):
        _store_row_tiles(h2_ref, (i,), SB, h2[i * SB:(i + 1) * SB, :])
    meta_i, meta_f = _route(h2, wr_ref, br_ref, carry)
    mi_ref[...] = meta_i[:, :META].reshape(nt, SB, META)
    mf_ref[...] = meta_f.reshape(nt, SB, LANES)
    cnt_ref[...] = carry[...]


def _mixer_sample(x_t, cache_t, cnt0, w):
    nt, nseq, d = x_t.shape
    hist = cache_t.shape[0]
    weights = (w["gmix"], w["win"], w["lnvg"], w["lnvb"], w["wcs"], w["bcs"], w["wa"], w["cw"],
               w["cb"], w["lncg"], w["lncb"], w["wb"], w["bb"], w["bga"], w["bgb"], w["wout"],
               w["gffn"], w["wr"], w["br"])
    seqs = lambda i: (0, i, 0)
    return pl.pallas_call(
        _mixer_sample_kernel,
        grid=(nseq // SB,),
        in_specs=[pl.BlockSpec((nt, SB, d), seqs), pl.BlockSpec((hist, SB, d), seqs),
                  _full(cnt0.shape)] + [_full(a.shape) for a in weights],
        out_specs=[
            pl.BlockSpec((nt, SB, d), seqs),
            pl.BlockSpec((nt, SB * ROW_TILES, LANES), seqs),
            pl.BlockSpec((nt, SB, META), seqs),
            pl.BlockSpec((nt, SB, LANES), seqs),
            _full((SUBLANES, LANES)),
            pl.BlockSpec((hist, SB, d), seqs),
            pl.BlockSpec((nt, SB, d), seqs),
        ],
        out_shape=[
            jax.ShapeDtypeStruct((nt, nseq, d), _F32),
            jax.ShapeDtypeStruct((nt, nseq * ROW_TILES, LANES), _F32),
            jax.ShapeDtypeStruct((nt, nseq, META), jnp.int32),
            jax.ShapeDtypeStruct((nt, nseq, LANES), _F32),
            jax.ShapeDtypeStruct((SUBLANES, LANES), _F32),
            jax.ShapeDtypeStruct((hist, nseq, d), _F32),
            jax.ShapeDtypeStruct((nt, nseq, d), _F32),
        ],
        scratch_shapes=[pltpu.VMEM((SUBLANES, LANES), _F32)],
        compiler_params=pltpu.CompilerParams(
            dimension_semantics=("arbitrary",), vmem_limit_bytes=VMEM_LIMIT),
        name="mixer_sample",
    )(x_t, cache_t, cnt0, *weights)


def _row_tile(ref, row):
    return ref.at[pl.ds(pl.multiple_of(row * ROW_TILES, ROW_TILES), ROW_TILES), :]


def _dispatch_kernel(n_first, n_tiles, n_blk, pad_start_ref, pad_len_ref, nused_ref, dest_ref,
                     h2a_ref, h2b_ref, xs_hbm, zbuf, sem, zsem):
    i = pl.program_id(0)

    def fill(start):
        def go(copy):
            copy.start() if start else copy.wait()

        def per_expert(e, carry):
            first, length = pad_start_ref[e], pad_len_ref[e]
            for s in range(1, BLK.bit_length()):
                p = BLK >> s

                @pl.when((length & p) != 0)
                def _():
                    row = first + (length & (-2 * p))
                    go(pltpu.make_async_copy(
                        zbuf.at[pl.ds(0, p * ROW_TILES), :],
                        xs_hbm.at[pl.ds(pl.multiple_of(row * ROW_TILES, ROW_TILES), p * ROW_TILES), :],
                        zsem))
            return carry

        def per_tail_block(b, carry):
            go(pltpu.make_async_copy(
                zbuf, xs_hbm.at[pl.ds(pl.multiple_of(b * BLK * ROW_TILES, ROW_TILES),
                                      BLK * ROW_TILES), :], zsem))
            return carry

        lax.fori_loop(0, N_EXPERTS, per_expert, 0)
        lax.fori_loop(nused_ref[0], n_blk, per_tail_block, 0)

    @pl.when(i == 0)
    def _():
        zbuf[...] = jnp.zeros_like(zbuf)
        fill(True)

    def scatter(h2_ref):
        def issue(t, carry):
            src = _row_tile(h2_ref, t)
            for k in range(TOP_K):
                pltpu.make_async_copy(src, _row_tile(xs_hbm, dest_ref[0, 0, t * TOP_K + k]),
                                      sem).start(priority=k % 2)
            return carry

        lax.fori_loop(0, TD, issue, 0)
        for _ in range(TOP_K):
            pltpu.make_async_copy(h2_ref, xs_hbm.at[pl.ds(0, TD * ROW_TILES), :], sem).wait()

    @pl.when(i < n_first)
    def _():
        scatter(h2a_ref)

    @pl.when(i >= n_first)
    def _():
        scatter(h2b_ref)

    @pl.when(i == n_tiles - 1)
    def _():
        fill(False)


def _dispatch(pad_start, pad_len, nused, dest, h2_a, h2_b, n_blk):
    n_tiles = dest.shape[0]
    n_first = h2_a.shape[0] // (TD * ROW_TILES)
    tile = (TD * ROW_TILES, LANES)
    grid_spec = pltpu.PrefetchScalarGridSpec(
        num_scalar_prefetch=3,
        grid=(n_tiles,),
        in_specs=[
            pl.BlockSpec((1, 1, TD * TOP_K), lambda i, *_: (i, 0, 0), memory_space=pltpu.SMEM),
            pl.BlockSpec(tile, lambda i, *_: (jnp.minimum(i, n_first - 1), 0)),
            pl.BlockSpec(tile, lambda i, *_: (jnp.maximum(i - n_first, 0), 0)),
        ],
        out_specs=pl.BlockSpec(memory_space=pl.ANY),
        scratch_shapes=[pltpu.VMEM((BLK * ROW_TILES, LANES), _F32), pltpu.SemaphoreType.DMA,
                        pltpu.SemaphoreType.DMA],
    )
    return pl.pallas_call(
        functools.partial(_dispatch_kernel, n_first, n_tiles, n_blk),
        grid_spec=grid_spec,
        out_shape=jax.ShapeDtypeStruct((n_blk * BLK * ROW_TILES, LANES), _F32),
        compiler_params=pltpu.CompilerParams(
            dimension_semantics=("arbitrary",), has_side_effects=True),
        name="dispatch",
    )(pad_start, pad_len, nused, dest, h2_a, h2_b)


def _experts_kernel(be_ref, nused_ref, nvalid_ref, slot_ref, next_ref, xs_ref, bup_ref, bdn_ref,
                    wup_hbm, wdn_hbm, out_ref, wup_buf, wdn_buf, sems):
    del nused_ref
    i = pl.program_id(0)
    expert = be_ref[i]
    slot = slot_ref[i]
    first_of_run = (i == 0) | (expert != be_ref[jnp.maximum(i - 1, 0)])

    def weight_copies(e, s):
        return (pltpu.make_async_copy(wup_hbm.at[e], wup_buf.at[s], sems.at[s]),
                pltpu.make_async_copy(wdn_hbm.at[e], wdn_buf.at[s], sems.at[s]))

    @pl.when(i == 0)
    def _():
        for copy in weight_copies(expert, slot):
            copy.start()

    @pl.when(first_of_run & (next_ref[i] >= 0))
    def _():
        for copy in weight_copies(next_ref[i], 1 - slot):
            copy.start()

    @pl.when(first_of_run)
    def _():
        for copy in weight_copies(expert, slot):
            copy.wait()

    wup = wup_buf.at[slot]
    wdn = wdn_buf.at[slot]

    def first_rows(n):
        if n:
            x = _load_row_tiles(xs_ref, 0, n)
            gate = _dot(x, wup[:, :D_EXPERT]) + bup_ref[:, :D_EXPERT]
            up = _dot(x, wup[:, D_EXPERT:]) + bup_ref[:, D_EXPERT:]
            gate = jnp.minimum(gate, SWIGLU_LIMIT)
            up = jnp.clip(up, -SWIGLU_LIMIT, SWIGLU_LIMIT)
            act = (up + 1.0) * (gate * _sigmoid(SWIGLU_ALPHA * gate))
            o = _dot(act, wdn[...]) + bdn_ref[...]
            _store_row_tiles(out_ref, (), n, o)
        if n < BLK:
            out_ref[n * ROW_TILES:, :] = jnp.zeros(((BLK - n) * ROW_TILES, LANES), _F32)

    n_valid = nvalid_ref[i]
    for live in range(BLK // SUB + 1):
        lo, hi = (live - 1) * SUB, live * SUB

        @pl.when((n_valid > lo) & (n_valid <= hi) if live else n_valid <= 0)
        def _():
            first_rows(live * SUB)


def _experts(block_expert, nused, nvalid, slot, next_expert, xs, w_up, b_up, w_down, b_down):
    n_blk = block_expert.shape[0]
    rows = lambda i, be, nu, *_: (jnp.minimum(i, nu[0] - 1), 0)
    expert = lambda i, be, *_: (be[i], 0, 0)
    d, f2 = w_up.shape[1], w_up.shape[2]
    grid_spec = pltpu.PrefetchScalarGridSpec(
        num_scalar_prefetch=5,
        grid=(n_blk,),
        in_specs=[
            pl.BlockSpec((BLK * ROW_TILES, LANES), rows),
            pl.BlockSpec((None, 1, f2), expert),
            pl.BlockSpec((None, 1, d), expert),
            pl.BlockSpec(memory_space=pl.ANY),
            pl.BlockSpec(memory_space=pl.ANY),
        ],
        out_specs=pl.BlockSpec((BLK * ROW_TILES, LANES), lambda i, *_: (i, 0)),
        scratch_shapes=[pltpu.VMEM((2, d, f2), _F32), pltpu.VMEM((2, w_down.shape[1], d), _F32),
                        pltpu.SemaphoreType.DMA((2,))],
    )
    return pl.pallas_call(
        _experts_kernel,
        grid_spec=grid_spec,
        out_shape=jax.ShapeDtypeStruct(xs.shape, _F32),
        compiler_params=pltpu.CompilerParams(
            dimension_semantics=("arbitrary",), vmem_limit_bytes=VMEM_LIMIT),
        name="experts",
    )(block_expert, nused, nvalid, slot, next_expert, xs, b_up, b_down, w_up, w_down)


def _combine_kernel(n, dest_ref, dest_next_ref, x1_ref, mf_ref, gfin_ref, outs_hbm, y_ref, buf,
                    sems):
    i = pl.program_id(0)

    def gather(d_ref, slot):
        def issue(t, carry):
            for k in range(TOP_K):
                dst = buf.at[slot, pl.ds(pl.multiple_of((k * TD + t) * ROW_TILES, ROW_TILES),
                                         ROW_TILES), :]
                pltpu.make_async_copy(_row_tile(outs_hbm, d_ref[0, 0, t * TOP_K + k]), dst,
                                      sems.at[slot]).start(priority=k % 2)
            return carry

        lax.fori_loop(0, TD, issue, 0)

    slot = i % 2

    @pl.when(i == 0)
    def _():
        gather(dest_ref, 0)

    @pl.when(i + 1 < n)
    def _():
        gather(dest_next_ref, 1 - slot)

    pltpu.make_async_copy(outs_hbm.at[pl.ds(0, buf.shape[1]), :], buf.at[slot], sems.at[slot]).wait()

    y = x1_ref[...]
    gates = mf_ref[...]
    rows = buf.at[slot]
    for k in range(TOP_K):
        y = y + gates[:, k:k + 1] * _load_row_tiles(rows, k * TD * ROW_TILES, TD)
    y_ref[...] = _rms(y, gfin_ref[...])


def _combine(dest, x1, meta_f, g_final, outs):
    n_tok, d = x1.shape
    n_tiles = n_tok // TD
    tok = lambda i: (i, 0)
    return pl.pallas_call(
        functools.partial(_combine_kernel, n_tiles),
        grid=(n_tiles,),
        in_specs=[
            pl.BlockSpec((1, 1, TD * TOP_K), lambda i: (i, 0, 0), memory_space=pltpu.SMEM),
            pl.BlockSpec((1, 1, TD * TOP_K), lambda i: (jnp.minimum(i + 1, n_tiles - 1), 0, 0),
                         memory_space=pltpu.SMEM),
            pl.BlockSpec((TD, d), tok),
            pl.BlockSpec((TD, LANES), tok),
            _full(g_final.shape),
            pl.BlockSpec(memory_space=pl.ANY),
        ],
        out_specs=pl.BlockSpec((TD, d), tok),
        out_shape=jax.ShapeDtypeStruct((n_tok, d), _F32),
        scratch_shapes=[pltpu.VMEM((2, TOP_K * TD * ROW_TILES, LANES), _F32),
                        pltpu.SemaphoreType.DMA((2,))],
        compiler_params=pltpu.CompilerParams(
            dimension_semantics=("arbitrary",), vmem_limit_bytes=VMEM_LIMIT),
        name="combine",
    )(dest, dest, x1, meta_f, g_final, outs)


def _prep_weights(norm_mix_g, w_in, ln_v_g, ln_v_b, w_spatial, b_spatial, w_branch_a, conv_w,
                  conv_b, ln_conv_g, ln_conv_b, w_branch_b, b_branch_b, b_gate_a, b_gate_b, w_out,
                  norm_ffn_g, w_router, b_router, n_new):
    row = lambda a: a.reshape(1, -1)
    pad = LANES - N_EXPERTS
    wr = jnp.pad(w_router, ((0, 0), (0, pad)))
    wr_hi = wr.astype(_BF16)
    wr_lo = (wr - wr_hi.astype(_F32)).astype(_BF16)
    wcs = jnp.repeat(w_spatial[:, :n_new, :n_new].transpose(1, 2, 0).reshape(n_new * n_new, A_GROUPS),
                     LANES, axis=1)
    return dict(
        gmix=row(norm_mix_g), win=w_in.astype(_BF16), lnvg=row(ln_v_g), lnvb=row(ln_v_b),
        wsp=w_spatial, bsp=jnp.repeat(b_spatial.T, LANES, axis=1), wcs=wcs,
        bcs=jnp.repeat(b_spatial[:, :n_new].T, LANES, axis=1),
        wa=w_branch_a.astype(_BF16), cw=conv_w, cb=row(conv_b), lncg=row(ln_conv_g),
        lncb=row(ln_conv_b), wb=w_branch_b.astype(_BF16), bb=row(b_branch_b), bga=row(b_gate_a),
        bgb=row(b_gate_b), wout=w_out.astype(_BF16), gffn=row(norm_ffn_g),
        wr=jnp.concatenate([wr_hi, wr_lo], axis=1), br=jnp.pad(row(b_router), ((0, 0), (0, pad))))


def kernel(x_prompt, x_sample, cache_conv, norm_mix_g, w_in, ln_v_g, ln_v_b, w_spatial, b_spatial, w_branch_a, conv_w, conv_b, ln_conv_g, ln_conv_b, w_branch_b, b_branch_b, b_gate_a, b_gate_b, w_out, norm_ffn_g, w_router, b_router, w_up, b_up, w_down, b_down, norm_final_g):
    assert w_in.shape[0] == 1, "single trunk layer"
    bsz, seq, d = x_prompt.shape
    nseq, n_new, _ = x_sample.shape
    assert d == D_MODEL and seq % TT == 0 and TT % CHUNK == 0 and nseq % SB == 0
    assert n_new <= CHUNK and (nseq * n_new) % TD == 0 and (bsz * seq) % TD == 0
    w = _prep_weights(norm_mix_g[0], w_in[0], ln_v_g[0], ln_v_b[0], w_spatial[0], b_spatial[0],
                      w_branch_a[0], conv_w[0], conv_b[0], ln_conv_g[0], ln_conv_b[0],
                      w_branch_b[0], b_branch_b[0], b_gate_a[0], b_gate_b[0], w_out[0],
                      norm_ffn_g[0], w_router[0], b_router[0], n_new)

    x1_p, h2_p, mi_p, mf_p, cnt_p, cst_p = _mixer_prompt(x_prompt, w)
    x_t = x_sample.transpose(1, 0, 2)
    cache_t = cache_conv[0].transpose(1, 0, 2)
    x1_s, h2_s, mi_s, mf_s, cnt, cst_s, v_s = _mixer_sample(x_t, cache_t, cnt_p, w)
    n_p, n_s = bsz * seq, nseq * n_new
    x1_s = x1_s.reshape(n_s, d)
    h2_s = h2_s.reshape(n_s * ROW_TILES, LANES)
    mi_s = mi_s.reshape(n_s, META)
    mf_s = mf_s.reshape(n_s, LANES)

    counts = cnt[0, :N_EXPERTS].astype(jnp.int32)
    padded = ((counts + BLK - 1) // BLK) * BLK
    pend = jnp.cumsum(padded)
    pstart = pend - padded
    n_assign = (n_p + n_s) * TOP_K
    n_blk = -(-(n_assign + N_EXPERTS * (BLK - 1)) // BLK)
    nused = (pend[-1] // BLK).astype(jnp.int32)
    blk_ids = jnp.arange(n_blk, dtype=jnp.int32)
    first_row = jnp.minimum(blk_ids, nused - 1) * BLK
    be = jnp.sum((pend[None, :] <= first_row[:, None]).astype(jnp.int32), axis=1)
    be = jnp.minimum(be, N_EXPERTS - 1)
    experts_row = jnp.arange(N_EXPERTS, dtype=jnp.int32)

    def dest_of(mi):
        sel = mi[:, :TOP_K, None] == experts_row[None, None, :]
        dest = jnp.sum(jnp.where(sel, pstart[None, None, :], 0), axis=-1) + mi[:, TOP_K:2 * TOP_K]
        return dest.reshape(-1, 1, TD * TOP_K)

    dest_p, dest_s = dest_of(mi_p), dest_of(mi_s)
    xs = _dispatch(pstart + counts, padded - counts, nused.reshape(1),
                   jnp.concatenate([dest_p, dest_s], axis=0), h2_p, h2_s, n_blk)
    of_block = be[:, None] == experts_row[None, :]
    row_in_expert = blk_ids * BLK - jnp.sum(jnp.where(of_block, pstart[None, :], 0), axis=1)
    nvalid = jnp.clip(jnp.sum(jnp.where(of_block, counts[None, :], 0), axis=1) - row_in_expert,
                      0, BLK)
    nvalid = jnp.where(blk_ids < nused, nvalid, 0).astype(jnp.int32)
    used = counts > 0
    slot_e = (jnp.cumsum(used.astype(jnp.int32)) - 1) % 2
    later = jnp.where(used[None, :] & (experts_row[None, :] > experts_row[:, None]),
                      experts_row[None, :], N_EXPERTS)
    next_e = jnp.min(later, axis=1)
    next_e = jnp.where(next_e < N_EXPERTS, next_e, -1)
    slot = jnp.sum(jnp.where(of_block, slot_e[None, :], 0), axis=1).astype(jnp.int32)
    next_expert = jnp.sum(jnp.where(of_block, next_e[None, :], 0), axis=1).astype(jnp.int32)
    outs = _experts(be, nused.reshape(1), nvalid, slot, next_expert, xs, w_up[0],
                    b_up[0][:, None, :], w_down[0], b_down[0][:, None, :])
    g_final = norm_final_g.reshape(1, d)
    y_p = _combine(dest_p, x1_p, mf_p, g_final, outs)
    y_s = _combine(dest_s, x1_s, mf_s, g_final, outs)

    y_prompt = y_p.reshape(bsz, seq, d)
    y_sample = y_s.reshape(n_new, nseq, d).transpose(1, 0, 2)
    conv_state_prompt = cst_p[None]
    conv_state_sample = cst_s.transpose(1, 0, 2)[None]
    chunk_v_sample = v_s.transpose(1, 0, 2)[None]
    return (y_prompt, y_sample, conv_state_prompt, conv_state_sample, chunk_v_sample)
```

```python
import functools

import jax
import jax.numpy as jnp
from jax import lax
from jax.experimental import pallas as pl
from jax.experimental.pallas import tpu as pltpu

D_MODEL = 1024
CHUNK = 128
A_GROUPS = 8
CONV_WIDTH = 31
HIST = 32
N_EXPERTS = 32
TOP_K = 4
META = 2 * TOP_K
D_EXPERT = 1024
SWIGLU_LIMIT = 7.0
SWIGLU_ALPHA = 1.702
EPS = 1e-5
LANES = 128
SUBLANES = 8
ROW_TILES = D_MODEL // LANES
TT = 256
TD = 512
CONV_ROWS = 64
SB = 32
BLK = 512
SUB = 128
VMEM_LIMIT = 56 * 1024 * 1024

_F32 = jnp.float32
_BF16 = jnp.bfloat16
_INV_SQRT2 = 0.7071067811865476


def _rms(x, g):
    return x * lax.rsqrt(jnp.mean(x * x, axis=-1, keepdims=True) + EPS) * g


def _ln(x, g, b):
    mu = jnp.mean(x, axis=-1, keepdims=True)
    xc = x - mu
    var = jnp.mean(xc * xc, axis=-1, keepdims=True)
    return xc * lax.rsqrt(var + EPS) * g + b


def _gelu(x):
    return 0.5 * x * (1.0 + lax.erf(x * _INV_SQRT2))


def _sigmoid(x):
    return 1.0 / (1.0 + jnp.exp(-x))


def _dot(a, b):
    return jnp.dot(a, b, preferred_element_type=_F32)


def _store_row_tiles(ref, lead, n, val, start=0):
    for j in range(ROW_TILES):
        ref[lead + (pl.ds(start + j, n, stride=ROW_TILES), slice(None))] = (
            val[:, j * LANES:(j + 1) * LANES])


def _load_row_tiles(ref, start, n):
    return jnp.concatenate(
        [ref[pl.ds(start + j, n, stride=ROW_TILES), :] for j in range(ROW_TILES)], axis=1)


def _route(h2, wr_ref, br_ref, carry_ref):
    n = h2.shape[0]
    h_hi = h2.astype(_BF16)
    h_lo = (h2 - h_hi.astype(_F32)).astype(_BF16)
    by_hi = _dot(h_hi, wr_ref[...])
    logits = (by_hi[:, :LANES] + by_hi[:, LANES:] + _dot(h_lo, wr_ref[:, :LANES])) + br_ref[...]
    lane = lax.broadcasted_iota(jnp.int32, (n, LANES), 1)
    l = jnp.where(lane < N_EXPERTS, logits, -jnp.inf)
    onehots, vals, idxs = [], [], []
    for _ in range(TOP_K):
        m = jnp.max(l, axis=-1, keepdims=True)
        idx = jnp.min(jnp.where(l == m, lane, LANES), axis=-1, keepdims=True)
        oh = lane == idx
        onehots.append(oh)
        vals.append(m)
        idxs.append(idx)
        l = jnp.where(oh, -jnp.inf, l)
    exps = [jnp.exp(v - vals[0]) for v in vals]
    denom = exps[0] + exps[1] + exps[2] + exps[3]
    gates = [e / denom for e in exps]

    oh_all = jnp.zeros((n, LANES), _F32)
    for oh in onehots:
        oh_all = oh_all + jnp.where(oh, 1.0, 0.0)
    ri = lax.broadcasted_iota(jnp.int32, (n, n), 0)
    ci = lax.broadcasted_iota(jnp.int32, (n, n), 1)
    tri = jnp.where(ci < ri, 1.0, 0.0).astype(_BF16)
    base = _dot(tri, oh_all.astype(_BF16)) + carry_ref[0:1, :]
    ranks = [jnp.sum(jnp.where(oh, base, 0.0), axis=-1, keepdims=True) for oh in onehots]
    carry_ref[...] = carry_ref[...] + jnp.sum(oh_all, axis=0, keepdims=True)

    meta_i = jnp.zeros((n, LANES), jnp.int32)
    meta_f = jnp.zeros((n, LANES), _F32)
    for k in range(TOP_K):
        meta_i = jnp.where(lane == k, idxs[k], meta_i)
        meta_i = jnp.where(lane == TOP_K + k, ranks[k].astype(jnp.int32), meta_i)
        meta_f = jnp.where(lane == k, gates[k], meta_f)
    return meta_i, meta_f


def _mixer_prompt_kernel(x_ref, gmix_ref, win_ref, lnvg_ref, lnvb_ref, wsp_ref, bsp_ref, wa_ref,
                         cw_ref, cb_ref, lncg_ref, lncb_ref, wb_ref, bb_ref, bga_ref, bgb_ref,
                         wout_ref, gffn_ref, wr_ref, br_ref,
                         x1_ref, h2_ref, mi_ref, mf_ref, cnt_ref, cst_ref,
                         cbuf, carry):
    b = pl.program_id(0)
    t = pl.program_id(1)

    @pl.when((b == 0) & (t == 0))
    def _():
        carry[...] = jnp.zeros_like(carry)

    @pl.when(t == 0)
    def _():
        cbuf[0:HIST, :] = jnp.zeros((HIST, D_MODEL), _F32)
        cbuf[HIST + TT:, :] = jnp.zeros((SUBLANES, D_MODEL), _F32)

    x = x_ref[...]
    hb = _rms(x, gmix_ref[...]).astype(_BF16)

    def proj(s):
        return _dot(hb, win_ref[:, s * D_MODEL:(s + 1) * D_MODEL])

    glu = proj(2) * _sigmoid(proj(3))
    cbuf[HIST:HIST + TT, :] = glu
    off = HIST - (CONV_WIDTH - 1)
    span = HIST + CONV_ROWS + SUBLANES
    conv_cols = []
    other, half, pieces = (0, 1, 4, 5), D_MODEL // 2, []
    order_after = None
    for c in range(D_MODEL // LANES):
        cols = slice(c * LANES, (c + 1) * LANES)
        accs = []
        for r in range(TT // CONV_ROWS):
            acc = jnp.zeros((CONV_ROWS, LANES), _F32) + cb_ref[:, cols]
            window = cbuf[r * CONV_ROWS:r * CONV_ROWS + span, cols]
            if order_after is not None:
                zero = pltpu.bitcast(lax.shift_right_logical(
                    pltpu.bitcast(order_after, jnp.uint32), jnp.uint32(32)), _F32)
                window = window + jnp.concatenate([zero] * (span // SUBLANES), axis=0)
            for sh in range(SUBLANES):
                shifted = window if sh == 0 else pltpu.roll(window, span - sh, axis=0)
                for a in range(HIST // SUBLANES + 1):
                    k = a * SUBLANES + sh - off
                    if 0 <= k < CONV_WIDTH:
                        acc = acc + (cw_ref[k:k + 1, cols]
                                     * shifted[a * SUBLANES:a * SUBLANES + CONV_ROWS, :])
            accs.append(acc)
            order_after = acc[:SUBLANES, :]
        conv_cols.append(jnp.concatenate(accs, axis=0))
        lo = other[c // 2] * D_MODEL + (c % 2) * half
        pieces.append(_dot(hb, win_ref[:, lo:lo + half]))
    conv = jnp.concatenate(conv_cols, axis=1)
    p_u, p_v, p_ga, p_gb = [jnp.concatenate(pieces[2 * i:2 * i + 2], axis=1) for i in range(4)]
    cbuf[0:HIST, :] = cbuf[TT:TT + HIST, :]
    cn = _ln(conv, lncg_ref[...], lncb_ref[...])
    y_b = _dot((cn * _sigmoid(cn)).astype(_BF16), wb_ref[...]) + bb_ref[...]

    u = _gelu(p_u)
    v = _ln(_gelu(p_v), lnvg_ref[...], lnvb_ref[...])
    ri = lax.broadcasted_iota(jnp.int32, (CHUNK, CHUNK), 0)
    ci = lax.broadcasted_iota(jnp.int32, (CHUNK, CHUNK), 1)
    n_chunks = TT // CHUNK
    vb = v.astype(_BF16)
    s_groups = []
    for g in range(A_GROUPS):
        wm = jnp.where(ci <= ri, wsp_ref[g], 0.0).astype(_BF16)
        cols = slice(g * LANES, (g + 1) * LANES)
        s_groups.append(_dot(wm, jnp.concatenate(
            [vb[c * CHUNK:(c + 1) * CHUNK, cols] for c in range(n_chunks)], axis=1)))
    s = jnp.concatenate(
        [jnp.concatenate([sg[:, c * LANES:(c + 1) * LANES] for sg in s_groups], axis=1)
         + bsp_ref[...] for c in range(n_chunks)], axis=0)
    y_a = _dot((u * s).astype(_BF16), wa_ref[...])

    merged = _sigmoid(p_ga + bga_ref[...]) * y_a + _sigmoid(p_gb + bgb_ref[...]) * y_b
    x1 = x + _dot(merged.astype(_BF16), wout_ref[...])
    x1_ref[...] = x1

    h2 = _rms(x1, gffn_ref[...])
    h2_ref[...] = h2
    meta_i, meta_f = _route(h2, wr_ref, br_ref, carry)
    mi_ref[...] = meta_i[:, :META]
    mf_ref[...] = meta_f
    cnt_ref[...] = carry[...]

    @pl.when(t == pl.num_programs(1) - 1)
    def _():
        cst_ref[...] = cbuf[HIST - (CONV_WIDTH - 1):HIST, :]


def _full(shape):
    return pl.BlockSpec(shape, lambda *_: (0,) * len(shape))


def _mixer_prompt(x, w):
    bsz, seq, d = x.shape
    n_tok = bsz * seq
    nt = seq // TT
    tok = lambda b, t: (b * nt + t, 0)
    weights = (w["gmix"], w["win"], w["lnvg"], w["lnvb"], w["wsp"], w["bsp"], w["wa"], w["cw"],
               w["cb"], w["lncg"], w["lncb"], w["wb"], w["bb"], w["bga"], w["bgb"], w["wout"],
               w["gffn"], w["wr"], w["br"])
    return pl.pallas_call(
        _mixer_prompt_kernel,
        grid=(bsz, nt),
        in_specs=[pl.BlockSpec((None, TT, d), lambda b, t: (b, t, 0))]
                 + [_full(a.shape) for a in weights],
        out_specs=[
            pl.BlockSpec((TT, d), tok),
            pl.BlockSpec((TT, d), tok),
            pl.BlockSpec((TT, META), tok),
            pl.BlockSpec((TT, LANES), tok),
            _full((SUBLANES, LANES)),
            pl.BlockSpec((None, CONV_WIDTH - 1, d), lambda b, t: (b, 0, 0)),
        ],
        out_shape=[
            jax.ShapeDtypeStruct((n_tok, d), _F32),
            jax.ShapeDtypeStruct((n_tok, d), _F32),
            jax.ShapeDtypeStruct((n_tok, META), jnp.int32),
            jax.ShapeDtypeStruct((n_tok, LANES), _F32),
            jax.ShapeDtypeStruct((SUBLANES, LANES), _F32),
            jax.ShapeDtypeStruct((bsz, CONV_WIDTH - 1, d), _F32),
        ],
        scratch_shapes=[pltpu.VMEM((HIST + TT + SUBLANES, d), _F32),
                        pltpu.VMEM((SUBLANES, LANES), _F32)],
        compiler_params=pltpu.CompilerParams(
            dimension_semantics=("arbitrary", "arbitrary"), vmem_limit_bytes=VMEM_LIMIT),
        name="mixer_prompt",
    )(x, *weights)


def _mixer_sample_kernel(x_ref, cache_ref, cnt0_ref, gmix_ref, win_ref, lnvg_ref, lnvb_ref,
                         wcs_ref, bcs_ref, wa_ref, cw_ref, cb_ref, lncg_ref, lncb_ref, wb_ref,
                         bb_ref, bga_ref, bgb_ref, wout_ref, gffn_ref, wr_ref, br_ref,
                         x1_ref, h2_ref, mi_ref, mf_ref, cnt_ref, cst_ref, v_ref,
                         carry):
    nt = x_ref.shape[0]
    n = nt * SB

    @pl.when(pl.program_id(0) == 0)
    def _():
        carry[...] = cnt0_ref[...]

    x = x_ref[...].reshape(n, D_MODEL)
    hb = _rms(x, gmix_ref[...]).astype(_BF16)

    def proj(s):
        return _dot(hb, win_ref[:, s * D_MODEL:(s + 1) * D_MODEL])

    u = _gelu(proj(0))
    v = _ln(_gelu(proj(1)), lnvg_ref[...], lnvb_ref[...])
    v_ref[...] = v.reshape(nt, SB, D_MODEL)
    s_rows = []
    for i in range(nt):
        acc = jnp.zeros((SB, D_MODEL), _F32) + bcs_ref[i:i + 1, :]
        for j in range(i + 1):
            acc = acc + wcs_ref[i * nt + j:i * nt + j + 1, :] * v[j * SB:(j + 1) * SB, :]
        s_rows.append(acc)
    s = jnp.concatenate(s_rows, axis=0)
    y_a = _dot((u * s).astype(_BF16), wa_ref[...])

    glu = proj(2) * _sigmoid(proj(3))
    hist = CONV_WIDTH - 1
    cst_ref[0:hist - nt] = cache_ref[nt:hist]
    cst_ref[hist - nt:hist] = glu.reshape(nt, SB, D_MODEL)
    conv_rows = []
    for i in range(nt):
        acc = jnp.zeros((SB, D_MODEL), _F32) + cb_ref[...]
        for k in range(CONV_WIDTH):
            p = i + k
            src = cache_ref[p] if p < hist else glu[(p - hist) * SB:(p - hist + 1) * SB, :]
            acc = acc + cw_ref[k:k + 1, :] * src
        conv_rows.append(acc)
    cn = _ln(jnp.concatenate(conv_rows, axis=0), lncg_ref[...], lncb_ref[...])
    y_b = _dot((cn * _sigmoid(cn)).astype(_BF16), wb_ref[...]) + bb_ref[...]

    merged = (_sigmoid(proj(4) + bga_ref[...]) * y_a + _sigmoid(proj(5) + bgb_ref[...]) * y_b)
    x1 = x + _dot(merged.astype(_BF16), wout_ref[...])
    x1_ref[...] = x1.reshape(nt, SB, D_MODEL)

    h2 = _rms(x1, gffn_ref[...])
    h2_ref[...] = h2.reshape(nt, SB, D_MODEL)
    meta_i, meta_f = _route(h2, wr_ref, br_ref, carry)
    mi_ref[...] = meta_i[:, :META].reshape(nt, SB, META)
    mf_ref[...] = meta_f.reshape(nt, SB, LANES)
    cnt_ref[...] = carry[...]


def _mixer_sample(x_t, cache_t, cnt0, w):
    nt, nseq, d = x_t.shape
    hist = cache_t.shape[0]
    weights = (w["gmix"], w["win"], w["lnvg"], w["lnvb"], w["wcs"], w["bcs"], w["wa"], w["cw"],
               w["cb"], w["lncg"], w["lncb"], w["wb"], w["bb"], w["bga"], w["bgb"], w["wout"],
               w["gffn"], w["wr"], w["br"])
    seqs = lambda i: (0, i, 0)
    return pl.pallas_call(
        _mixer_sample_kernel,
        grid=(nseq // SB,),
        in_specs=[pl.BlockSpec((nt, SB, d), seqs), pl.BlockSpec((hist, SB, d), seqs),
                  _full(cnt0.shape)] + [_full(a.shape) for a in weights],
        out_specs=[
            pl.BlockSpec((nt, SB, d), seqs),
            pl.BlockSpec((nt, SB, d), seqs),
            pl.BlockSpec((nt, SB, META), seqs),
            pl.BlockSpec((nt, SB, LANES), seqs),
            _full((SUBLANES, LANES)),
            pl.BlockSpec((hist, SB, d), seqs),
            pl.BlockSpec((nt, SB, d), seqs),
        ],
        out_shape=[
            jax.ShapeDtypeStruct((nt, nseq, d), _F32),
            jax.ShapeDtypeStruct((nt, nseq, d), _F32),
            jax.ShapeDtypeStruct((nt, nseq, META), jnp.int32),
            jax.ShapeDtypeStruct((nt, nseq, LANES), _F32),
            jax.ShapeDtypeStruct((SUBLANES, LANES), _F32),
            jax.ShapeDtypeStruct((hist, nseq, d), _F32),
            jax.ShapeDtypeStruct((nt, nseq, d), _F32),
        ],
        scratch_shapes=[pltpu.VMEM((SUBLANES, LANES), _F32)],
        compiler_params=pltpu.CompilerParams(
            dimension_semantics=("arbitrary",), vmem_limit_bytes=VMEM_LIMIT),
        name="mixer_sample",
    )(x_t, cache_t, cnt0, *weights)


def _row_tile(ref, row):
    return ref.at[pl.ds(pl.multiple_of(row * ROW_TILES, ROW_TILES), ROW_TILES), :]


def _dispatch_kernel(n_first, n_tiles, n_blk, pad_start_ref, pad_len_ref, nused_ref, dest_ref,
                     h2a_ref, h2b_ref, xs_hbm, zbuf, rows, sems, zsem):
    i = pl.program_id(0)
    slot = i % 2

    def fill(start):
        def go(copy):
            copy.start() if start else copy.wait()

        def per_expert(e, carry):
            first, length = pad_start_ref[e], pad_len_ref[e]
            for s in range(1, BLK.bit_length()):
                p = BLK >> s

                @pl.when((length & p) != 0)
                def _():
                    row = first + (length & (-2 * p))
                    go(pltpu.make_async_copy(
                        zbuf.at[pl.ds(0, p * ROW_TILES), :],
                        xs_hbm.at[pl.ds(pl.multiple_of(row * ROW_TILES, ROW_TILES), p * ROW_TILES), :],
                        zsem))
            return carry

        def per_tail_block(b, carry):
            go(pltpu.make_async_copy(
                zbuf, xs_hbm.at[pl.ds(pl.multiple_of(b * BLK * ROW_TILES, ROW_TILES),
                                      BLK * ROW_TILES), :], zsem))
            return carry

        lax.fori_loop(0, N_EXPERTS, per_expert, 0)
        lax.fori_loop(nused_ref[0], n_blk, per_tail_block, 0)

    @pl.when(i == 0)
    def _():
        zbuf[...] = jnp.zeros_like(zbuf)
        fill(True)

    def wait_row_copies(s):
        for _ in range(TOP_K):
            pltpu.make_async_copy(rows.at[s], xs_hbm.at[pl.ds(0, TD * ROW_TILES), :],
                                  sems.at[s]).wait()

    @pl.when(i < n_first)
    def _():
        _store_row_tiles(rows.at[slot], (), TD, h2a_ref[...])

    @pl.when(i >= n_first)
    def _():
        _store_row_tiles(rows.at[slot], (), TD, h2b_ref[...])

    @pl.when(i > 0)
    def _():
        wait_row_copies(1 - slot)

    def issue(t, carry):
        src = _row_tile(rows.at[slot], t)
        for k in range(TOP_K):
            pltpu.make_async_copy(src, _row_tile(xs_hbm, dest_ref[0, 0, t * TOP_K + k]),
                                  sems.at[slot]).start(priority=k % 2)
        return carry

    lax.fori_loop(0, TD, issue, 0)

    @pl.when(i == n_tiles - 1)
    def _():
        wait_row_copies(slot)
        fill(False)


def _dispatch(pad_start, pad_len, nused, dest, h2_a, h2_b, n_blk):
    n_tiles = dest.shape[0]
    n_first = h2_a.shape[0] // TD
    tile = (TD, h2_a.shape[1])
    grid_spec = pltpu.PrefetchScalarGridSpec(
        num_scalar_prefetch=3,
        grid=(n_tiles,),
        in_specs=[
            pl.BlockSpec((1, 1, TD * TOP_K), lambda i, *_: (i, 0, 0), memory_space=pltpu.SMEM),
            pl.BlockSpec(tile, lambda i, *_: (jnp.minimum(i, n_first - 1), 0)),
            pl.BlockSpec(tile, lambda i, *_: (jnp.maximum(i - n_first, 0), 0)),
        ],
        out_specs=pl.BlockSpec(memory_space=pl.ANY),
        scratch_shapes=[pltpu.VMEM((BLK * ROW_TILES, LANES), _F32),
                        pltpu.VMEM((2, TD * ROW_TILES, LANES), _F32),
                        pltpu.SemaphoreType.DMA((2,)), pltpu.SemaphoreType.DMA],
    )
    return pl.pallas_call(
        functools.partial(_dispatch_kernel, n_first, n_tiles, n_blk),
        grid_spec=grid_spec,
        out_shape=jax.ShapeDtypeStruct((n_blk * BLK * ROW_TILES, LANES), _F32),
        compiler_params=pltpu.CompilerParams(
            dimension_semantics=("arbitrary",), has_side_effects=True),
        name="dispatch",
    )(pad_start, pad_len, nused, dest, h2_a, h2_b)


def _experts_kernel(be_ref, nused_ref, nvalid_ref, slot_ref, next_ref, xs_ref, bup_ref, bdn_ref,
                    wup_hbm, wdn_hbm, out_ref, wup_buf, wdn_buf, sems):
    del nused_ref
    i = pl.program_id(0)
    expert = be_ref[i]
    slot = slot_ref[i]
    first_of_run = (i == 0) | (expert != be_ref[jnp.maximum(i - 1, 0)])

    def weight_copies(e, s):
        return (pltpu.make_async_copy(wup_hbm.at[e], wup_buf.at[s], sems.at[s]),
                pltpu.make_async_copy(wdn_hbm.at[e], wdn_buf.at[s], sems.at[s]))

    @pl.when(i == 0)
    def _():
        for copy in weight_copies(expert, slot):
            copy.start()

    @pl.when(first_of_run & (next_ref[i] >= 0))
    def _():
        for copy in weight_copies(next_ref[i], 1 - slot):
            copy.start()

    @pl.when(first_of_run)
    def _():
        for copy in weight_copies(expert, slot):
            copy.wait()

    wup = wup_buf.at[slot]
    wdn = wdn_buf.at[slot]

    def first_rows(n):
        if n:
            x = _load_row_tiles(xs_ref, 0, n)
            gate = _dot(x, wup[:, :D_EXPERT]) + bup_ref[:, :D_EXPERT]
            up = _dot(x, wup[:, D_EXPERT:]) + bup_ref[:, D_EXPERT:]
            gate = jnp.minimum(gate, SWIGLU_LIMIT)
            up = jnp.clip(up, -SWIGLU_LIMIT, SWIGLU_LIMIT)
            act = (up + 1.0) * (gate * _sigmoid(SWIGLU_ALPHA * gate))
            o = _dot(act, wdn[...]) + bdn_ref[...]
            _store_row_tiles(out_ref, (), n, o)
        if n < BLK:
            out_ref[n * ROW_TILES:, :] = jnp.zeros(((BLK - n) * ROW_TILES, LANES), _F32)

    n_valid = nvalid_ref[i]
    for live in range(BLK // SUB + 1):
        lo, hi = (live - 1) * SUB, live * SUB

        @pl.when((n_valid > lo) & (n_valid <= hi) if live else n_valid <= 0)
        def _():
            first_rows(live * SUB)


def _experts(block_expert, nused, nvalid, slot, next_expert, xs, w_up, b_up, w_down, b_down):
    n_blk = block_expert.shape[0]
    rows = lambda i, be, nu, *_: (jnp.minimum(i, nu[0] - 1), 0)
    expert = lambda i, be, *_: (be[i], 0, 0)
    d, f2 = w_up.shape[1], w_up.shape[2]
    grid_spec = pltpu.PrefetchScalarGridSpec(
        num_scalar_prefetch=5,
        grid=(n_blk,),
        in_specs=[
            pl.BlockSpec((BLK * ROW_TILES, LANES), rows),
            pl.BlockSpec((None, 1, f2), expert),
            pl.BlockSpec((None, 1, d), expert),
            pl.BlockSpec(memory_space=pl.ANY),
            pl.BlockSpec(memory_space=pl.ANY),
        ],
        out_specs=pl.BlockSpec((BLK * ROW_TILES, LANES), lambda i, *_: (i, 0)),
        scratch_shapes=[pltpu.VMEM((2, d, f2), _F32), pltpu.VMEM((2, w_down.shape[1], d), _F32),
                        pltpu.SemaphoreType.DMA((2,))],
    )
    return pl.pallas_call(
        _experts_kernel,
        grid_spec=grid_spec,
        out_shape=jax.ShapeDtypeStruct(xs.shape, _F32),
        compiler_params=pltpu.CompilerParams(
            dimension_semantics=("arbitrary",), vmem_limit_bytes=VMEM_LIMIT),
        name="experts",
    )(block_expert, nused, nvalid, slot, next_expert, xs, b_up, b_down, w_up, w_down)


def _combine_kernel(n, dest_ref, dest_next_ref, x1_ref, mf_ref, gfin_ref, outs_hbm, y_ref, buf,
                    sems):
    i = pl.program_id(0)

    def gather(d_ref, slot):
        def issue(t, carry):
            for k in range(TOP_K):
                dst = buf.at[slot, pl.ds(pl.multiple_of((k * TD + t) * ROW_TILES, ROW_TILES),
                                         ROW_TILES), :]
                pltpu.make_async_copy(_row_tile(outs_hbm, d_ref[0, 0, t * TOP_K + k]), dst,
                                      sems.at[slot]).start(priority=k % 2)
            return carry

        lax.fori_loop(0, TD, issue, 0)

    slot = i % 2

    @pl.when(i == 0)
    def _():
        gather(dest_ref, 0)

    @pl.when(i + 1 < n)
    def _():
        gather(dest_next_ref, 1 - slot)

    pltpu.make_async_copy(outs_hbm.at[pl.ds(0, buf.shape[1]), :], buf.at[slot], sems.at[slot]).wait()

    y = x1_ref[...]
    gates = mf_ref[...]
    rows = buf.at[slot]
    for k in range(TOP_K):
        y = y + gates[:, k:k + 1] * _load_row_tiles(rows, k * TD * ROW_TILES, TD)
    y_ref[...] = _rms(y, gfin_ref[...])


def _combine(dest, x1, meta_f, g_final, outs):
    n_tok, d = x1.shape
    n_tiles = n_tok // TD
    tok = lambda i: (i, 0)
    return pl.pallas_call(
        functools.partial(_combine_kernel, n_tiles),
        grid=(n_tiles,),
        in_specs=[
            pl.BlockSpec((1, 1, TD * TOP_K), lambda i: (i, 0, 0), memory_space=pltpu.SMEM),
            pl.BlockSpec((1, 1, TD * TOP_K), lambda i: (jnp.minimum(i + 1, n_tiles - 1), 0, 0),
                         memory_space=pltpu.SMEM),
            pl.BlockSpec((TD, d), tok),
            pl.BlockSpec((TD, LANES), tok),
            _full(g_final.shape),
            pl.BlockSpec(memory_space=pl.ANY),
        ],
        out_specs=pl.BlockSpec((TD, d), tok),
        out_shape=jax.ShapeDtypeStruct((n_tok, d), _F32),
        scratch_shapes=[pltpu.VMEM((2, TOP_K * TD * ROW_TILES, LANES), _F32),
                        pltpu.SemaphoreType.DMA((2,))],
        compiler_params=pltpu.CompilerParams(
            dimension_semantics=("arbitrary",), vmem_limit_bytes=VMEM_LIMIT),
        name="combine",
    )(dest, dest, x1, meta_f, g_final, outs)


def _prep_weights(norm_mix_g, w_in, ln_v_g, ln_v_b, w_spatial, b_spatial, w_branch_a, conv_w,
                  conv_b, ln_conv_g, ln_conv_b, w_branch_b, b_branch_b, b_gate_a, b_gate_b, w_out,
                  norm_ffn_g, w_router, b_router, n_new):
    row = lambda a: a.reshape(1, -1)
    pad = LANES - N_EXPERTS
    wr = jnp.pad(w_router, ((0, 0), (0, pad)))
    wr_hi = wr.astype(_BF16)
    wr_lo = (wr - wr_hi.astype(_F32)).astype(_BF16)
    wcs = jnp.repeat(w_spatial[:, :n_new, :n_new].transpose(1, 2, 0).reshape(n_new * n_new, A_GROUPS),
                     LANES, axis=1)
    return dict(
        gmix=row(norm_mix_g), win=w_in.astype(_BF16), lnvg=row(ln_v_g), lnvb=row(ln_v_b),
        wsp=w_spatial, bsp=jnp.repeat(b_spatial.T, LANES, axis=1), wcs=wcs,
        bcs=jnp.repeat(b_spatial[:, :n_new].T, LANES, axis=1),
        wa=w_branch_a.astype(_BF16), cw=conv_w, cb=row(conv_b), lncg=row(ln_conv_g),
        lncb=row(ln_conv_b), wb=w_branch_b.astype(_BF16), bb=row(b_branch_b), bga=row(b_gate_a),
        bgb=row(b_gate_b), wout=w_out.astype(_BF16), gffn=row(norm_ffn_g),
        wr=jnp.concatenate([wr_hi, wr_lo], axis=1), br=jnp.pad(row(b_router), ((0, 0), (0, pad))))


def kernel(x_prompt, x_sample, cache_conv, norm_mix_g, w_in, ln_v_g, ln_v_b, w_spatial, b_spatial, w_branch_a, conv_w, conv_b, ln_conv_g, ln_conv_b, w_branch_b, b_branch_b, b_gate_a, b_gate_b, w_out, norm_ffn_g, w_router, b_router, w_up, b_up, w_down, b_down, norm_final_g):
    assert w_in.shape[0] == 1, "single trunk layer"
    bsz, seq, d = x_prompt.shape
    nseq, n_new, _ = x_sample.shape
    assert d == D_MODEL and seq % TT == 0 and TT % CHUNK == 0 and nseq % SB == 0
    assert n_new <= CHUNK and (nseq * n_new) % TD == 0 and (bsz * seq) % TD == 0
    w = _prep_weights(norm_mix_g[0], w_in[0], ln_v_g[0], ln_v_b[0], w_spatial[0], b_spatial[0],
                      w_branch_a[0], conv_w[0], conv_b[0], ln_conv_g[0], ln_conv_b[0],
                      w_branch_b[0], b_branch_b[0], b_gate_a[0], b_gate_b[0], w_out[0],
                      norm_ffn_g[0], w_router[0], b_router[0], n_new)

    x1_p, h2_p, mi_p, mf_p, cnt_p, cst_p = _mixer_prompt(x_prompt, w)
    x_t = x_sample.transpose(1, 0, 2)
    cache_t = cache_conv[0].transpose(1, 0, 2)
    x1_s, h2_s, mi_s, mf_s, cnt, cst_s, v_s = _mixer_sample(x_t, cache_t, cnt_p, w)
    n_p, n_s = bsz * seq, nseq * n_new
    x1_s = x1_s.reshape(n_s, d)
    h2_s = h2_s.reshape(n_s, d)
    mi_s = mi_s.reshape(n_s, META)
    mf_s = mf_s.reshape(n_s, LANES)

    counts = cnt[0, :N_EXPERTS].astype(jnp.int32)
    padded = ((counts + BLK - 1) // BLK) * BLK
    pend = jnp.cumsum(padded)
    pstart = pend - padded
    n_assign = (n_p + n_s) * TOP_K
    n_blk = -(-(n_assign + N_EXPERTS * (BLK - 1)) // BLK)
    nused = (pend[-1] // BLK).astype(jnp.int32)
    blk_ids = jnp.arange(n_blk, dtype=jnp.int32)
    first_row = jnp.minimum(blk_ids, nused - 1) * BLK
    be = jnp.sum((pend[None, :] <= first_row[:, None]).astype(jnp.int32), axis=1)
    be = jnp.minimum(be, N_EXPERTS - 1)
    experts_row = jnp.arange(N_EXPERTS, dtype=jnp.int32)

    def dest_of(mi):
        sel = mi[:, :TOP_K, None] == experts_row[None, None, :]
        dest = jnp.sum(jnp.where(sel, pstart[None, None, :], 0), axis=-1) + mi[:, TOP_K:2 * TOP_K]
        return dest.reshape(-1, 1, TD * TOP_K)

    dest_p, dest_s = dest_of(mi_p), dest_of(mi_s)
    xs = _dispatch(pstart + counts, padded - counts, nused.reshape(1),
                   jnp.concatenate([dest_p, dest_s], axis=0), h2_p, h2_s, n_blk)
    of_block = be[:, None] == experts_row[None, :]
    row_in_expert = blk_ids * BLK - jnp.sum(jnp.where(of_block, pstart[None, :], 0), axis=1)
    nvalid = jnp.clip(jnp.sum(jnp.where(of_block, counts[None, :], 0), axis=1) - row_in_expert,
                      0, BLK)
    nvalid = jnp.where(blk_ids < nused, nvalid, 0).astype(jnp.int32)
    used = counts > 0
    slot_e = (jnp.cumsum(used.astype(jnp.int32)) - 1) % 2
    later = jnp.where(used[None, :] & (experts_row[None, :] > experts_row[:, None]),
                      experts_row[None, :], N_EXPERTS)
    next_e = jnp.min(later, axis=1)
    next_e = jnp.where(next_e < N_EXPERTS, next_e, -1)
    slot = jnp.sum(jnp.where(of_block, slot_e[None, :], 0), axis=1).astype(jnp.int32)
    next_expert = jnp.sum(jnp.where(of_block, next_e[None, :], 0), axis=1).astype(jnp.int32)
    outs = _experts(be, nused.reshape(1), nvalid, slot, next_expert, xs, w_up[0],
                    b_up[0][:, None, :], w_down[0], b_down[0][:, None, :])
    g_final = norm_final_g.reshape(1, d)
    y_p = _combine(dest_p, x1_p, mf_p, g_final, outs)
    y_s = _combine(dest_s, x1_s, mf_s, g_final, outs)

    y_prompt = y_p.reshape(bsz, seq, d)
    y_sample = y_s.reshape(n_new, nseq, d).transpose(1, 0, 2)
    conv_state_prompt = cst_p[None]
    conv_state_sample = cst_s.transpose(1, 0, 2)[None]
    chunk_v_sample = v_s.transpose(1, 0, 2)[None]
    return (y_prompt, y_sample, conv_state_prompt, conv_state_sample, chunk_v_sample)
```

```python
import functools

import jax
import jax.numpy as jnp
from jax import lax
from jax.experimental import pallas as pl
from jax.experimental.pallas import tpu as pltpu

D_MODEL = 1024
CHUNK = 128
A_GROUPS = 8
CONV_WIDTH = 31
HIST = 32
N_EXPERTS = 32
TOP_K = 4
META = 2 * TOP_K
D_EXPERT = 1024
SWIGLU_LIMIT = 7.0
SWIGLU_ALPHA = 1.702
EPS = 1e-5
LANES = 128
SUBLANES = 8
ROW_TILES = D_MODEL // LANES
TT = 512
TD = 512
CONV_ROWS = 32
SB = 32
BLK = 512
SUB = 128
VMEM_LIMIT = 56 * 1024 * 1024

_F32 = jnp.float32
_BF16 = jnp.bfloat16
_INV_SQRT2 = 0.7071067811865476


def _rms(x, g):
    return x * lax.rsqrt(jnp.mean(x * x, axis=-1, keepdims=True) + EPS) * g


def _ln(x, g, b):
    mu = jnp.mean(x, axis=-1, keepdims=True)
    xc = x - mu
    var = jnp.mean(xc * xc, axis=-1, keepdims=True)
    return xc * lax.rsqrt(var + EPS) * g + b


def _gelu(x):
    return 0.5 * x * (1.0 + lax.erf(x * _INV_SQRT2))


def _sigmoid(x):
    return 1.0 / (1.0 + jnp.exp(-x))


def _dot(a, b):
    return jnp.dot(a, b, preferred_element_type=_F32)


def _store_row_tiles(ref, lead, n, val, start=0):
    for j in range(ROW_TILES):
        ref[lead + (pl.ds(start + j, n, stride=ROW_TILES), slice(None))] = (
            val[:, j * LANES:(j + 1) * LANES])


def _load_row_tiles(ref, start, n):
    return jnp.concatenate(
        [ref[pl.ds(start + j, n, stride=ROW_TILES), :] for j in range(ROW_TILES)], axis=1)


def _route(h2, wr_ref, br_ref, carry_ref):
    n = h2.shape[0]
    h_hi = h2.astype(_BF16)
    h_lo = (h2 - h_hi.astype(_F32)).astype(_BF16)
    by_hi = _dot(h_hi, wr_ref[...])
    logits = (by_hi[:, :LANES] + by_hi[:, LANES:] + _dot(h_lo, wr_ref[:, :LANES])) + br_ref[...]
    lane = lax.broadcasted_iota(jnp.int32, (n, LANES), 1)
    l = jnp.where(lane < N_EXPERTS, logits, -jnp.inf)
    onehots, vals, idxs = [], [], []
    for _ in range(TOP_K):
        m = jnp.max(l, axis=-1, keepdims=True)
        idx = jnp.min(jnp.where(l == m, lane, LANES), axis=-1, keepdims=True)
        oh = lane == idx
        onehots.append(oh)
        vals.append(m)
        idxs.append(idx)
        l = jnp.where(oh, -jnp.inf, l)
    exps = [jnp.exp(v - vals[0]) for v in vals]
    denom = exps[0] + exps[1] + exps[2] + exps[3]
    gates = [e / denom for e in exps]

    oh_all = jnp.zeros((n, LANES), _F32)
    for oh in onehots:
        oh_all = oh_all + jnp.where(oh, 1.0, 0.0)
    ri = lax.broadcasted_iota(jnp.int32, (n, n), 0)
    ci = lax.broadcasted_iota(jnp.int32, (n, n), 1)
    tri = jnp.where(ci < ri, 1.0, 0.0).astype(_BF16)
    base = _dot(tri, oh_all.astype(_BF16)) + carry_ref[0:1, :]
    ranks = [jnp.sum(jnp.where(oh, base, 0.0), axis=-1, keepdims=True) for oh in onehots]
    carry_ref[...] = carry_ref[...] + jnp.sum(oh_all, axis=0, keepdims=True)

    meta_i = jnp.zeros((n, LANES), jnp.int32)
    meta_f = jnp.zeros((n, LANES), _F32)
    for k in range(TOP_K):
        meta_i = jnp.where(lane == k, idxs[k], meta_i)
        meta_i = jnp.where(lane == TOP_K + k, ranks[k].astype(jnp.int32), meta_i)
        meta_f = jnp.where(lane == k, gates[k], meta_f)
    return meta_i, meta_f


def _mixer_prompt_kernel(x_ref, gmix_ref, win_ref, lnvg_ref, lnvb_ref, wsp_ref, bsp_ref, wa_ref,
                         cw_ref, cb_ref, lncg_ref, lncb_ref, wb_ref, bb_ref, bga_ref, bgb_ref,
                         wout_ref, gffn_ref, wr_ref, br_ref,
                         x1_ref, h2_ref, mi_ref, mf_ref, cnt_ref, cst_ref,
                         cbuf, carry):
    b = pl.program_id(0)
    t = pl.program_id(1)

    @pl.when((b == 0) & (t == 0))
    def _():
        carry[...] = jnp.zeros_like(carry)

    @pl.when(t == 0)
    def _():
        cbuf[0:HIST, :] = jnp.zeros((HIST, D_MODEL), _F32)
        cbuf[HIST + TT:, :] = jnp.zeros((SUBLANES, D_MODEL), _F32)

    x = x_ref[...]
    hb = _rms(x, gmix_ref[...]).astype(_BF16)

    def proj(s):
        return _dot(hb, win_ref[:, s * D_MODEL:(s + 1) * D_MODEL])

    glu = proj(2) * _sigmoid(proj(3))
    cbuf[HIST:HIST + TT, :] = glu
    off = HIST - (CONV_WIDTH - 1)
    span = HIST + CONV_ROWS + SUBLANES
    conv_cols = []
    other, half, pieces = (0, 1, 4, 5), D_MODEL // 2, []
    order_after = None
    for c in range(D_MODEL // LANES):
        cols = slice(c * LANES, (c + 1) * LANES)
        accs = []
        for r in range(TT // CONV_ROWS):
            acc = jnp.zeros((CONV_ROWS, LANES), _F32) + cb_ref[:, cols]
            window = cbuf[r * CONV_ROWS:r * CONV_ROWS + span, cols]
            if order_after is not None:
                zero = pltpu.bitcast(lax.shift_right_logical(
                    pltpu.bitcast(order_after, jnp.uint32), jnp.uint32(32)), _F32)
                window = window + jnp.concatenate([zero] * (span // SUBLANES), axis=0)
            for sh in range(SUBLANES):
                shifted = window if sh == 0 else pltpu.roll(window, span - sh, axis=0)
                for a in range(HIST // SUBLANES + 1):
                    k = a * SUBLANES + sh - off
                    if 0 <= k < CONV_WIDTH:
                        acc = acc + (cw_ref[k:k + 1, cols]
                                     * shifted[a * SUBLANES:a * SUBLANES + CONV_ROWS, :])
            accs.append(acc)
            order_after = acc[:SUBLANES, :]
        conv_cols.append(jnp.concatenate(accs, axis=0))
        lo = other[c // 2] * D_MODEL + (c % 2) * half
        pieces.append(_dot(hb, win_ref[:, lo:lo + half]))
    conv = jnp.concatenate(conv_cols, axis=1)
    p_u, p_v, p_ga, p_gb = [jnp.concatenate(pieces[2 * i:2 * i + 2], axis=1) for i in range(4)]
    cbuf[0:HIST, :] = cbuf[TT:TT + HIST, :]
    cn = _ln(conv, lncg_ref[...], lncb_ref[...])
    y_b = _dot((cn * _sigmoid(cn)).astype(_BF16), wb_ref[...]) + bb_ref[...]

    u = _gelu(p_u)
    v = _ln(_gelu(p_v), lnvg_ref[...], lnvb_ref[...])
    ri = lax.broadcasted_iota(jnp.int32, (CHUNK, CHUNK), 0)
    ci = lax.broadcasted_iota(jnp.int32, (CHUNK, CHUNK), 1)
    n_chunks = TT // CHUNK
    vb = v.astype(_BF16)
    s_groups = []
    for g in range(A_GROUPS):
        wm = jnp.where(ci <= ri, wsp_ref[g], 0.0).astype(_BF16)
        cols = slice(g * LANES, (g + 1) * LANES)
        s_groups.append(_dot(wm, jnp.concatenate(
            [vb[c * CHUNK:(c + 1) * CHUNK, cols] for c in range(n_chunks)], axis=1)))
    s = jnp.concatenate(
        [jnp.concatenate([sg[:, c * LANES:(c + 1) * LANES] for sg in s_groups], axis=1)
         + bsp_ref[...] for c in range(n_chunks)], axis=0)
    y_a = _dot((u * s).astype(_BF16), wa_ref[...])

    merged = _sigmoid(p_ga + bga_ref[...]) * y_a + _sigmoid(p_gb + bgb_ref[...]) * y_b
    x1 = x + _dot(merged.astype(_BF16), wout_ref[...])
    x1_ref[...] = x1

    h2 = _rms(x1, gffn_ref[...])
    _store_row_tiles(h2_ref, (), TT, h2)
    meta_i, meta_f = _route(h2, wr_ref, br_ref, carry)
    mi_ref[...] = meta_i[:, :META]
    mf_ref[...] = meta_f
    cnt_ref[...] = carry[...]

    @pl.when(t == pl.num_programs(1) - 1)
    def _():
        cst_ref[...] = cbuf[HIST - (CONV_WIDTH - 1):HIST, :]


def _full(shape):
    return pl.BlockSpec(shape, lambda *_: (0,) * len(shape))


def _mixer_prompt(x, w):
    bsz, seq, d = x.shape
    n_tok = bsz * seq
    nt = seq // TT
    tok = lambda b, t: (b * nt + t, 0)
    weights = (w["gmix"], w["win"], w["lnvg"], w["lnvb"], w["wsp"], w["bsp"], w["wa"], w["cw"],
               w["cb"], w["lncg"], w["lncb"], w["wb"], w["bb"], w["bga"], w["bgb"], w["wout"],
               w["gffn"], w["wr"], w["br"])
    return pl.pallas_call(
        _mixer_prompt_kernel,
        grid=(bsz, nt),
        in_specs=[pl.BlockSpec((None, TT, d), lambda b, t: (b, t, 0))]
                 + [_full(a.shape) for a in weights],
        out_specs=[
            pl.BlockSpec((TT, d), tok),
            pl.BlockSpec((TT * ROW_TILES, LANES), tok),
            pl.BlockSpec((TT, META), tok),
            pl.BlockSpec((TT, LANES), tok),
            _full((SUBLANES, LANES)),
            pl.BlockSpec((None, CONV_WIDTH - 1, d), lambda b, t: (b, 0, 0)),
        ],
        out_shape=[
            jax.ShapeDtypeStruct((n_tok, d), _F32),
            jax.ShapeDtypeStruct((n_tok * ROW_TILES, LANES), _F32),
            jax.ShapeDtypeStruct((n_tok, META), jnp.int32),
            jax.ShapeDtypeStruct((n_tok, LANES), _F32),
            jax.ShapeDtypeStruct((SUBLANES, LANES), _F32),
            jax.ShapeDtypeStruct((bsz, CONV_WIDTH - 1, d), _F32),
        ],
        scratch_shapes=[pltpu.VMEM((HIST + TT + SUBLANES, d), _F32),
                        pltpu.VMEM((SUBLANES, LANES), _F32)],
        compiler_params=pltpu.CompilerParams(
            dimension_semantics=("arbitrary", "arbitrary"), vmem_limit_bytes=VMEM_LIMIT),
        name="mixer_prompt",
    )(x, *weights)


def _mixer_sample_kernel(x_ref, cache_ref, cnt0_ref, gmix_ref, win_ref, lnvg_ref, lnvb_ref,
                         wcs_ref, bcs_ref, wa_ref, cw_ref, cb_ref, lncg_ref, lncb_ref, wb_ref,
                         bb_ref, bga_ref, bgb_ref, wout_ref, gffn_ref, wr_ref, br_ref,
                         x1_ref, h2_ref, mi_ref, mf_ref, cnt_ref, cst_ref, v_ref,
                         carry):
    nt = x_ref.shape[0]
    n = nt * SB

    @pl.when(pl.program_id(0) == 0)
    def _():
        carry[...] = cnt0_ref[...]

    x = x_ref[...].reshape(n, D_MODEL)
    hb = _rms(x, gmix_ref[...]).astype(_BF16)

    def proj(s):
        return _dot(hb, win_ref[:, s * D_MODEL:(s + 1) * D_MODEL])

    u = _gelu(proj(0))
    v = _ln(_gelu(proj(1)), lnvg_ref[...], lnvb_ref[...])
    v_ref[...] = v.reshape(nt, SB, D_MODEL)
    s_rows = []
    for i in range(nt):
        acc = jnp.zeros((SB, D_MODEL), _F32) + bcs_ref[i:i + 1, :]
        for j in range(i + 1):
            acc = acc + wcs_ref[i * nt + j:i * nt + j + 1, :] * v[j * SB:(j + 1) * SB, :]
        s_rows.append(acc)
    s = jnp.concatenate(s_rows, axis=0)
    y_a = _dot((u * s).astype(_BF16), wa_ref[...])

    glu = proj(2) * _sigmoid(proj(3))
    hist = CONV_WIDTH - 1
    cst_ref[0:hist - nt] = cache_ref[nt:hist]
    cst_ref[hist - nt:hist] = glu.reshape(nt, SB, D_MODEL)
    conv_rows = []
    for i in range(nt):
        acc = jnp.zeros((SB, D_MODEL), _F32) + cb_ref[...]
        for k in range(CONV_WIDTH):
            p = i + k
            src = cache_ref[p] if p < hist else glu[(p - hist) * SB:(p - hist + 1) * SB, :]
            acc = acc + cw_ref[k:k + 1, :] * src
        conv_rows.append(acc)
    cn = _ln(jnp.concatenate(conv_rows, axis=0), lncg_ref[...], lncb_ref[...])
    y_b = _dot((cn * _sigmoid(cn)).astype(_BF16), wb_ref[...]) + bb_ref[...]

    merged = (_sigmoid(proj(4) + bga_ref[...]) * y_a + _sigmoid(proj(5) + bgb_ref[...]) * y_b)
    x1 = x + _dot(merged.astype(_BF16), wout_ref[...])
    x1_ref[...] = x1.reshape(nt, SB, D_MODEL)

    h2 = _rms(x1, gffn_ref[...])
    for i in range(nt):
        _store_row_tiles(h2_ref, (i,), SB, h2[i * SB:(i + 1) * SB, :])
    meta_i, meta_f = _route(h2, wr_ref, br_ref, carry)
    mi_ref[...] = meta_i[:, :META].reshape(nt, SB, META)
    mf_ref[...] = meta_f.reshape(nt, SB, LANES)
    cnt_ref[...] = carry[...]


def _mixer_sample(x_t, cache_t, cnt0, w):
    nt, nseq, d = x_t.shape
    hist = cache_t.shape[0]
    weights = (w["gmix"], w["win"], w["lnvg"], w["lnvb"], w["wcs"], w["bcs"], w["wa"], w["cw"],
               w["cb"], w["lncg"], w["lncb"], w["wb"], w["bb"], w["bga"], w["bgb"], w["wout"],
               w["gffn"], w["wr"], w["br"])
    seqs = lambda i: (0, i, 0)
    return pl.pallas_call(
        _mixer_sample_kernel,
        grid=(nseq // SB,),
        in_specs=[pl.BlockSpec((nt, SB, d), seqs), pl.BlockSpec((hist, SB, d), seqs),
                  _full(cnt0.shape)] + [_full(a.shape) for a in weights],
        out_specs=[
            pl.BlockSpec((nt, SB, d), seqs),
            pl.BlockSpec((nt, SB * ROW_TILES, LANES), seqs),
            pl.BlockSpec((nt, SB, META), seqs),
            pl.BlockSpec((nt, SB, LANES), seqs),
            _full((SUBLANES, LANES)),
            pl.BlockSpec((hist, SB, d), seqs),
            pl.BlockSpec((nt, SB, d), seqs),
        ],
        out_shape=[
            jax.ShapeDtypeStruct((nt, nseq, d), _F32),
            jax.ShapeDtypeStruct((nt, nseq * ROW_TILES, LANES), _F32),
            jax.ShapeDtypeStruct((nt, nseq, META), jnp.int32),
            jax.ShapeDtypeStruct((nt, nseq, LANES), _F32),
            jax.ShapeDtypeStruct((SUBLANES, LANES), _F32),
            jax.ShapeDtypeStruct((hist, nseq, d), _F32),
            jax.ShapeDtypeStruct((nt, nseq, d), _F32),
        ],
        scratch_shapes=[pltpu.VMEM((SUBLANES, LANES), _F32)],
        compiler_params=pltpu.CompilerParams(
            dimension_semantics=("arbitrary",), vmem_limit_bytes=VMEM_LIMIT),
        name="mixer_sample",
    )(x_t, cache_t, cnt0, *weights)


def _row_tile(ref, row):
    return ref.at[pl.ds(pl.multiple_of(row * ROW_TILES, ROW_TILES), ROW_TILES), :]


def _dispatch_kernel(n_first, n_tiles, n_blk, pad_start_ref, pad_len_ref, nused_ref, dest_ref,
                     h2a_ref, h2b_ref, xs_hbm, zbuf, sem, zsem):
    i = pl.program_id(0)

    def fill(start):
        def go(copy):
            copy.start() if start else copy.wait()

        def per_expert(e, carry):
            first, length = pad_start_ref[e], pad_len_ref[e]
            for s in range(1, BLK.bit_length()):
                p = BLK >> s

                @pl.when((length & p) != 0)
                def _():
                    row = first + (length & (-2 * p))
                    go(pltpu.make_async_copy(
                        zbuf.at[pl.ds(0, p * ROW_TILES), :],
                        xs_hbm.at[pl.ds(pl.multiple_of(row * ROW_TILES, ROW_TILES), p * ROW_TILES), :],
                        zsem))
            return carry

        def per_tail_block(b, carry):
            go(pltpu.make_async_copy(
                zbuf, xs_hbm.at[pl.ds(pl.multiple_of(b * BLK * ROW_TILES, ROW_TILES),
                                      BLK * ROW_TILES), :], zsem))
            return carry

        lax.fori_loop(0, N_EXPERTS, per_expert, 0)
        lax.fori_loop(nused_ref[0], n_blk, per_tail_block, 0)

    @pl.when(i == 0)
    def _():
        zbuf[...] = jnp.zeros_like(zbuf)
        fill(True)

    def scatter(h2_ref):
        def issue(t, carry):
            src = _row_tile(h2_ref, t)
            for k in range(TOP_K):
                pltpu.make_async_copy(src, _row_tile(xs_hbm, dest_ref[0, 0, t * TOP_K + k]),
                                      sem).start(priority=k % 2)
            return carry

        lax.fori_loop(0, TD, issue, 0)
        for _ in range(TOP_K):
            pltpu.make_async_copy(h2_ref, xs_hbm.at[pl.ds(0, TD * ROW_TILES), :], sem).wait()

    @pl.when(i < n_first)
    def _():
        scatter(h2a_ref)

    @pl.when(i >= n_first)
    def _():
        scatter(h2b_ref)

    @pl.when(i == n_tiles - 1)
    def _():
        fill(False)


def _dispatch(pad_start, pad_len, nused, dest, h2_a, h2_b, n_blk):
    n_tiles = dest.shape[0]
    n_first = h2_a.shape[0] // (TD * ROW_TILES)
    tile = (TD * ROW_TILES, LANES)
    grid_spec = pltpu.PrefetchScalarGridSpec(
        num_scalar_prefetch=3,
        grid=(n_tiles,),
        in_specs=[
            pl.BlockSpec((1, 1, TD * TOP_K), lambda i, *_: (i, 0, 0), memory_space=pltpu.SMEM),
            pl.BlockSpec(tile, lambda i, *_: (jnp.minimum(i, n_first - 1), 0)),
            pl.BlockSpec(tile, lambda i, *_: (jnp.maximum(i - n_first, 0), 0)),
        ],
        out_specs=pl.BlockSpec(memory_space=pl.ANY),
        scratch_shapes=[pltpu.VMEM((BLK * ROW_TILES, LANES), _F32), pltpu.SemaphoreType.DMA,
                        pltpu.SemaphoreType.DMA],
    )
    return pl.pallas_call(
        functools.partial(_dispatch_kernel, n_first, n_tiles, n_blk),
        grid_spec=grid_spec,
        out_shape=jax.ShapeDtypeStruct((n_blk * BLK * ROW_TILES, LANES), _F32),
        compiler_params=pltpu.CompilerParams(
            dimension_semantics=("arbitrary",), has_side_effects=True),
        name="dispatch",
    )(pad_start, pad_len, nused, dest, h2_a, h2_b)


def _experts_kernel(be_ref, nused_ref, nvalid_ref, slot_ref, next_ref, xs_ref, bup_ref, bdn_ref,
                    wup_hbm, wdn_hbm, out_ref, wup_buf, wdn_buf, sems):
    del nused_ref
    i = pl.program_id(0)
    expert = be_ref[i]
    slot = slot_ref[i]
    first_of_run = (i == 0) | (expert != be_ref[jnp.maximum(i - 1, 0)])

    def weight_copies(e, s):
        return (pltpu.make_async_copy(wup_hbm.at[e], wup_buf.at[s], sems.at[s]),
                pltpu.make_async_copy(wdn_hbm.at[e], wdn_buf.at[s], sems.at[s]))

    @pl.when(i == 0)
    def _():
        for copy in weight_copies(expert, slot):
            copy.start()

    @pl.when(first_of_run & (next_ref[i] >= 0))
    def _():
        for copy in weight_copies(next_ref[i], 1 - slot):
            copy.start()

    @pl.when(first_of_run)
    def _():
        for copy in weight_copies(expert, slot):
            copy.wait()

    wup = wup_buf.at[slot]
    wdn = wdn_buf.at[slot]

    def first_rows(n):
        if n:
            x = _load_row_tiles(xs_ref, 0, n)
            gate = _dot(x, wup[:, :D_EXPERT]) + bup_ref[:, :D_EXPERT]
            up = _dot(x, wup[:, D_EXPERT:]) + bup_ref[:, D_EXPERT:]
            gate = jnp.minimum(gate, SWIGLU_LIMIT)
            up = jnp.clip(up, -SWIGLU_LIMIT, SWIGLU_LIMIT)
            act = (up + 1.0) * (gate * _sigmoid(SWIGLU_ALPHA * gate))
            o = _dot(act, wdn[...]) + bdn_ref[...]
            _store_row_tiles(out_ref, (), n, o)
        if n < BLK:
            out_ref[n * ROW_TILES:, :] = jnp.zeros(((BLK - n) * ROW_TILES, LANES), _F32)

    n_valid = nvalid_ref[i]
    for live in range(BLK // SUB + 1):
        lo, hi = (live - 1) * SUB, live * SUB

        @pl.when((n_valid > lo) & (n_valid <= hi) if live else n_valid <= 0)
        def _():
            first_rows(live * SUB)


def _experts(block_expert, nused, nvalid, slot, next_expert, xs, w_up, b_up, w_down, b_down):
    n_blk = block_expert.shape[0]
    rows = lambda i, be, nu, *_: (jnp.minimum(i, nu[0] - 1), 0)
    expert = lambda i, be, *_: (be[i], 0, 0)
    d, f2 = w_up.shape[1], w_up.shape[2]
    grid_spec = pltpu.PrefetchScalarGridSpec(
        num_scalar_prefetch=5,
        grid=(n_blk,),
        in_specs=[
            pl.BlockSpec((BLK * ROW_TILES, LANES), rows),
            pl.BlockSpec((None, 1, f2), expert),
            pl.BlockSpec((None, 1, d), expert),
            pl.BlockSpec(memory_space=pl.ANY),
            pl.BlockSpec(memory_space=pl.ANY),
        ],
        out_specs=pl.BlockSpec((BLK * ROW_TILES, LANES), lambda i, *_: (i, 0)),
        scratch_shapes=[pltpu.VMEM((2, d, f2), _F32), pltpu.VMEM((2, w_down.shape[1], d), _F32),
                        pltpu.SemaphoreType.DMA((2,))],
    )
    return pl.pallas_call(
        _experts_kernel,
        grid_spec=grid_spec,
        out_shape=jax.ShapeDtypeStruct(xs.shape, _F32),
        compiler_params=pltpu.CompilerParams(
            dimension_semantics=("arbitrary",), vmem_limit_bytes=VMEM_LIMIT),
        name="experts",
    )(block_expert, nused, nvalid, slot, next_expert, xs, b_up, b_down, w_up, w_down)


def _combine_kernel(n, dest_ref, dest_next_ref, x1_ref, mf_ref, gfin_ref, outs_hbm, y_ref, buf,
                    sems):
    i = pl.program_id(0)

    def gather(d_ref, slot):
        def issue(t, carry):
            for k in range(TOP_K):
                dst = buf.at[slot, pl.ds(pl.multiple_of((k * TD + t) * ROW_TILES, ROW_TILES),
                                         ROW_TILES), :]
                pltpu.make_async_copy(_row_tile(outs_hbm, d_ref[0, 0, t * TOP_K + k]), dst,
                                      sems.at[slot]).start(priority=k % 2)
            return carry

        lax.fori_loop(0, TD, issue, 0)

    slot = i % 2

    @pl.when(i == 0)
    def _():
        gather(dest_ref, 0)

    @pl.when(i + 1 < n)
    def _():
        gather(dest_next_ref, 1 - slot)

    pltpu.make_async_copy(outs_hbm.at[pl.ds(0, buf.shape[1]), :], buf.at[slot], sems.at[slot]).wait()

    y = x1_ref[...]
    gates = mf_ref[...]
    rows = buf.at[slot]
    for k in range(TOP_K):
        y = y + gates[:, k:k + 1] * _load_row_tiles(rows, k * TD * ROW_TILES, TD)
    y_ref[...] = _rms(y, gfin_ref[...])


def _combine(dest, x1, meta_f, g_final, outs):
    n_tok, d = x1.shape
    n_tiles = n_tok // TD
    tok = lambda i: (i, 0)
    return pl.pallas_call(
        functools.partial(_combine_kernel, n_tiles),
        grid=(n_tiles,),
        in_specs=[
            pl.BlockSpec((1, 1, TD * TOP_K), lambda i: (i, 0, 0), memory_space=pltpu.SMEM),
            pl.BlockSpec((1, 1, TD * TOP_K), lambda i: (jnp.minimum(i + 1, n_tiles - 1), 0, 0),
                         memory_space=pltpu.SMEM),
            pl.BlockSpec((TD, d), tok),
            pl.BlockSpec((TD, LANES), tok),
            _full(g_final.shape),
            pl.BlockSpec(memory_space=pl.ANY),
        ],
        out_specs=pl.BlockSpec((TD, d), tok),
        out_shape=jax.ShapeDtypeStruct((n_tok, d), _F32),
        scratch_shapes=[pltpu.VMEM((2, TOP_K * TD * ROW_TILES, LANES), _F32),
                        pltpu.SemaphoreType.DMA((2,))],
        compiler_params=pltpu.CompilerParams(
            dimension_semantics=("arbitrary",), vmem_limit_bytes=VMEM_LIMIT),
        name="combine",
    )(dest, dest, x1, meta_f, g_final, outs)


def _prep_weights(norm_mix_g, w_in, ln_v_g, ln_v_b, w_spatial, b_spatial, w_branch_a, conv_w,
                  conv_b, ln_conv_g, ln_conv_b, w_branch_b, b_branch_b, b_gate_a, b_gate_b, w_out,
                  norm_ffn_g, w_router, b_router, n_new):
    row = lambda a: a.reshape(1, -1)
    pad = LANES - N_EXPERTS
    wr = jnp.pad(w_router, ((0, 0), (0, pad)))
    wr_hi = wr.astype(_BF16)
    wr_lo = (wr - wr_hi.astype(_F32)).astype(_BF16)
    wcs = jnp.repeat(w_spatial[:, :n_new, :n_new].transpose(1, 2, 0).reshape(n_new * n_new, A_GROUPS),
                     LANES, axis=1)
    return dict(
        gmix=row(norm_mix_g), win=w_in.astype(_BF16), lnvg=row(ln_v_g), lnvb=row(ln_v_b),
        wsp=w_spatial, bsp=jnp.repeat(b_spatial.T, LANES, axis=1), wcs=wcs,
        bcs=jnp.repeat(b_spatial[:, :n_new].T, LANES, axis=1),
        wa=w_branch_a.astype(_BF16), cw=conv_w, cb=row(conv_b), lncg=row(ln_conv_g),
        lncb=row(ln_conv_b), wb=w_branch_b.astype(_BF16), bb=row(b_branch_b), bga=row(b_gate_a),
        bgb=row(b_gate_b), wout=w_out.astype(_BF16), gffn=row(norm_ffn_g),
        wr=jnp.concatenate([wr_hi, wr_lo], axis=1), br=jnp.pad(row(b_router), ((0, 0), (0, pad))))


def kernel(x_prompt, x_sample, cache_conv, norm_mix_g, w_in, ln_v_g, ln_v_b, w_spatial, b_spatial, w_branch_a, conv_w, conv_b, ln_conv_g, ln_conv_b, w_branch_b, b_branch_b, b_gate_a, b_gate_b, w_out, norm_ffn_g, w_router, b_router, w_up, b_up, w_down, b_down, norm_final_g):
    assert w_in.shape[0] == 1, "single trunk layer"
    bsz, seq, d = x_prompt.shape
    nseq, n_new, _ = x_sample.shape
    assert d == D_MODEL and seq % TT == 0 and TT % CHUNK == 0 and nseq % SB == 0
    assert n_new <= CHUNK and (nseq * n_new) % TD == 0 and (bsz * seq) % TD == 0
    w = _prep_weights(norm_mix_g[0], w_in[0], ln_v_g[0], ln_v_b[0], w_spatial[0], b_spatial[0],
                      w_branch_a[0], conv_w[0], conv_b[0], ln_conv_g[0], ln_conv_b[0],
                      w_branch_b[0], b_branch_b[0], b_gate_a[0], b_gate_b[0], w_out[0],
                      norm_ffn_g[0], w_router[0], b_router[0], n_new)

    x1_p, h2_p, mi_p, mf_p, cnt_p, cst_p = _mixer_prompt(x_prompt, w)
    x_t = x_sample.transpose(1, 0, 2)
    cache_t = cache_conv[0].transpose(1, 0, 2)
    x1_s, h2_s, mi_s, mf_s, cnt, cst_s, v_s = _mixer_sample(x_t, cache_t, cnt_p, w)
    n_p, n_s = bsz * seq, nseq * n_new
    x1_s = x1_s.reshape(n_s, d)
    h2_s = h2_s.reshape(n_s * ROW_TILES, LANES)
    mi_s = mi_s.reshape(n_s, META)
    mf_s = mf_s.reshape(n_s, LANES)

    counts = cnt[0, :N_EXPERTS].astype(jnp.int32)
    padded = ((counts + BLK - 1) // BLK) * BLK
    pend = jnp.cumsum(padded)
    pstart = pend - padded
    n_assign = (n_p + n_s) * TOP_K
    n_blk = -(-(n_assign + N_EXPERTS * (BLK - 1)) // BLK)
    nused = (pend[-1] // BLK).astype(jnp.int32)
    blk_ids = jnp.arange(n_blk, dtype=jnp.int32)
    first_row = jnp.minimum(blk_ids, nused - 1) * BLK
    be = jnp.sum((pend[None, :] <= first_row[:, None]).astype(jnp.int32), axis=1)
    be = jnp.minimum(be, N_EXPERTS - 1)
    experts_row = jnp.arange(N_EXPERTS, dtype=jnp.int32)

    def dest_of(mi):
        sel = mi[:, :TOP_K, None] == experts_row[None, None, :]
        dest = jnp.sum(jnp.where(sel, pstart[None, None, :], 0), axis=-1) + mi[:, TOP_K:2 * TOP_K]
        return dest.reshape(-1, 1, TD * TOP_K)

    dest_p, dest_s = dest_of(mi_p), dest_of(mi_s)
    xs = _dispatch(pstart + counts, padded - counts, nused.reshape(1),
                   jnp.concatenate([dest_p, dest_s], axis=0), h2_p, h2_s, n_blk)
    of_block = be[:, None] == experts_row[None, :]
    row_in_expert = blk_ids * BLK - jnp.sum(jnp.where(of_block, pstart[None, :], 0), axis=1)
    nvalid = jnp.clip(jnp.sum(jnp.where(of_block, counts[None, :], 0), axis=1) - row_in_expert,
                      0, BLK)
    nvalid = jnp.where(blk_ids < nused, nvalid, 0).astype(jnp.int32)
    used = counts > 0
    slot_e = (jnp.cumsum(used.astype(jnp.int32)) - 1) % 2
    later = jnp.where(used[None, :] & (experts_row[None, :] > experts_row[:, None]),
                      experts_row[None, :], N_EXPERTS)
    next_e = jnp.min(later, axis=1)
    next_e = jnp.where(next_e < N_EXPERTS, next_e, -1)
    slot = jnp.sum(jnp.where(of_block, slot_e[None, :], 0), axis=1).astype(jnp.int32)
    next_expert = jnp.sum(jnp.where(of_block, next_e[None, :], 0), axis=1).astype(jnp.int32)
    outs = _experts(be, nused.reshape(1), nvalid, slot, next_expert, xs, w_up[0],
                    b_up[0][:, None, :], w_down[0], b_down[0][:, None, :])
    g_final = norm_final_g.reshape(1, d)
    y_p = _combine(dest_p, x1_p, mf_p, g_final, outs)
    y_s = _combine(dest_s, x1_s, mf_s, g_final, outs)

    y_prompt = y_p.reshape(bsz, seq, d)
    y_sample = y_s.reshape(n_new, nseq, d).transpose(1, 0, 2)
    conv_state_prompt = cst_p[None]
    conv_state_sample = cst_s.transpose(1, 0, 2)[None]
    chunk_v_sample = v_s.transpose(1, 0, 2)[None]
    return (y_prompt, y_sample, conv_state_prompt, conv_state_sample, chunk_v_sample)
```
